```python
import jax
import jax.numpy as jnp
from jax import lax
import numpy as np

D_MODEL = 2048
BATCH = 2
SEQ = 8192
DEPTH = 4

GRID_W = 64
HEAD_DIM = 128
NA_HEADS = 8
NA_WIN_ROWS = 8
NA_WIN_COLS = 16
NA_BIAS_ROWS = 2 * NA_WIN_ROWS - 1
NA_BIAS_COLS = 2 * NA_WIN_COLS - 1
GQA_Q_HEADS = 8
GQA_KV_HEADS = 2
Q_BLOCK = 128
ROPE_THETA = 10000.0
CONV_CH = 1024
CONV_WIDTH = 3
FFN_HIDDEN = -(-8 * D_MODEL // (3 * 256)) * 256
RMS_EPS = 1e-6

NA_WIDTH = NA_HEADS * HEAD_DIM
GQA_Q_WIDTH = GQA_Q_HEADS * HEAD_DIM
GQA_KV_WIDTH = GQA_KV_HEADS * HEAD_DIM
IN_SIZES = (NA_WIDTH, NA_WIDTH, NA_WIDTH, GQA_Q_WIDTH, GQA_KV_WIDTH, GQA_KV_WIDTH,
            CONV_CH, CONV_CH, CONV_CH, D_MODEL, D_MODEL, D_MODEL)
IN_COLS = sum(IN_SIZES)
IN_OFFSETS = tuple(int(o) for o in np.cumsum(IN_SIZES)[:-1])

kernel_name = 'hybrid_na_gqa_shortconv_encoder'


def rms_norm(x, g):
    xf = x.astype(jnp.float32)
    y = xf * lax.rsqrt(jnp.mean(xf * xf, axis=-1, keepdims=True) + RMS_EPS)
    return (y * g.astype(jnp.float32)).astype(x.dtype)


def neighbourhood_attention(q, k, v, rpb):
    bsz, s_len, n_h, d = q.shape
    rows = s_len // GRID_W
    kr = min(NA_WIN_ROWS, rows)
    kc = NA_WIN_COLS
    qg = q.reshape(bsz, rows, GRID_W, n_h, d)
    kg = k.reshape(bsz, rows, GRID_W, n_h, d)
    vg = v.reshape(bsz, rows, GRID_W, n_h, d)
    cols = jnp.arange(GRID_W)
    col_start = jnp.clip(cols - kc // 2, 0, GRID_W - kc)
    col_idx = col_start[:, None] + jnp.arange(kc)[None, :]
    dc = col_idx - cols[:, None]
    scale = d ** -0.5

    def row_step(r):
        row_start = jnp.clip(r - kr // 2, 0, rows - kr)
        k_band = lax.dynamic_slice_in_dim(kg, row_start, kr, axis=1)
        v_band = lax.dynamic_slice_in_dim(vg, row_start, kr, axis=1)
        k_nb = k_band[:, :, col_idx]
        v_nb = v_band[:, :, col_idx]
        q_row = lax.dynamic_index_in_dim(qg, r, axis=1, keepdims=False)
        dr = row_start + jnp.arange(kr) - r
        bias = rpb[:, (dr + NA_WIN_ROWS - 1)[None, :, None],
                   (dc + NA_WIN_COLS - 1)[:, None, :]]
        s = jnp.einsum('bwhd,brwchd->bhwrc', q_row, k_nb).astype(jnp.float32) * scale
        s = s + bias.astype(jnp.float32)[None]
        p = jax.nn.softmax(s.reshape(bsz, n_h, GRID_W, kr * kc), axis=-1)
        p = p.reshape(bsz, n_h, GRID_W, kr, kc).astype(v.dtype)
        return jnp.einsum('bhwrc,brwchd->bwhd', p, v_nb)

    out = lax.map(row_step, jnp.arange(rows))
    return out.transpose(1, 0, 2, 3, 4).reshape(bsz, s_len, n_h * d)


def axial_rope(x, pos_row, pos_col):
    d = x.shape[-1]
    half = d // 2
    quarter = half // 2
    inv_freq = 1.0 / (ROPE_THETA ** (jnp.arange(quarter, dtype=jnp.float32) / quarter))
    xf = x.astype(jnp.float32)

    def rot(xp, pos):
        ang = pos.astype(jnp.float32)[:, None] * inv_freq[None, :]
        c = jnp.cos(ang)[None, :, None, :]
        s = jnp.sin(ang)[None, :, None, :]
        x1, x2 = xp[..., :quarter], xp[..., quarter:]
        return jnp.concatenate([x1 * c - x2 * s, x1 * s + x2 * c], axis=-1)

    out = jnp.concatenate([rot(xf[..., :half], pos_row), rot(xf[..., half:], pos_col)], axis=-1)
    return out.astype(x.dtype)


def gqa_attention(q, k, v, q_norm_g, k_norm_g):
    bsz, s_len, n_q, d = q.shape
    n_kv = k.shape[2]
    grp = n_q // n_kv
    t = jnp.arange(s_len)
    pos_row, pos_col = t // GRID_W, t % GRID_W
    q = axial_rope(rms_norm(q, q_norm_g), pos_row, pos_col)
    k = axial_rope(rms_norm(k, k_norm_g), pos_row, pos_col)
    n_blk = s_len // Q_BLOCK
    qb = q.reshape(bsz, n_blk, Q_BLOCK, n_kv, grp, d).transpose(1, 0, 2, 3, 4, 5)
    scale = d ** -0.5

    def block_step(q_blk):
        s = jnp.einsum('bqkgd,bskd->bkgqs', q_blk, k).astype(jnp.float32) * scale
        p = jax.nn.softmax(s, axis=-1).astype(v.dtype)
        return jnp.einsum('bkgqs,bskd->bqkgd', p, v)

    out = lax.map(block_step, qb)
    return out.transpose(1, 0, 2, 3, 4, 5).reshape(bsz, s_len, n_q * d)


def short_conv_mixer(h, b_gate, c_gate, conv_w, conv_b):
    s_len = h.shape[1]
    u = c_gate * h
    pad = CONV_WIDTH // 2
    up = jnp.pad(u, ((0, 0), (pad, pad), (0, 0)))
    y = conv_b + sum(up[:, j:j + s_len] * conv_w[j] for j in range(CONV_WIDTH))
    return b_gate * y


def hybrid_layer(x, w_in, na_rpb, q_norm_g, k_norm_g, conv_w, conv_b, w_br_na, w_br_gqa,
                 w_br_conv, w_out, pre_mix_g, post_mix_g, pre_ffn_g, post_ffn_g,
                 w_ffn_gate, w_ffn_up, w_ffn_down):
    bsz, s_len, _ = x.shape
    h = rms_norm(x, pre_mix_g)
    proj = h @ w_in
    (na_q, na_k, na_v, g_q, g_k, g_v, cv_h, cv_bg, cv_cg,
     gate_na, gate_gqa, gate_conv) = jnp.split(proj, IN_OFFSETS, axis=-1)

    def heads(t, n):
        return t.reshape(bsz, s_len, n, HEAD_DIM)

    y_na = neighbourhood_attention(heads(na_q, NA_HEADS), heads(na_k, NA_HEADS),
                                   heads(na_v, NA_HEADS), na_rpb) @ w_br_na
    y_gqa = gqa_attention(heads(g_q, GQA_Q_HEADS), heads(g_k, GQA_KV_HEADS),
                          heads(g_v, GQA_KV_HEADS), q_norm_g, k_norm_g) @ w_br_gqa
    y_conv = short_conv_mixer(cv_h, cv_bg, cv_cg, conv_w, conv_b) @ w_br_conv
    merged = (jax.nn.sigmoid(gate_na) * y_na + jax.nn.sigmoid(gate_gqa) * y_gqa
              + jax.nn.sigmoid(gate_conv) * y_conv)
    x = x + rms_norm(merged @ w_out, post_mix_g)

    h = rms_norm(x, pre_ffn_g)
    f = (jax.nn.silu(h @ w_ffn_gate) * (h @ w_ffn_up)) @ w_ffn_down
    return x + rms_norm(f, post_ffn_g)


def setup_inputs(seed: int = 0) -> dict:
    key = jax.random.key(seed)
    ks = jax.random.split(key, 19)

    def nrm(k, shape, scale):
        return jax.random.normal(k, shape, jnp.float32) * scale

    def gain(k, shape):
        return 1.0 + nrm(k, shape, 0.05)

    L, D = DEPTH, D_MODEL
    return {
        'x': nrm(ks[0], (BATCH, SEQ, D), 1.0),
        'w_in': nrm(ks[1], (L, D, IN_COLS), D ** -0.5),
        'na_rpb': nrm(ks[2], (L, NA_HEADS, NA_BIAS_ROWS, NA_BIAS_COLS), 0.5),
        'q_norm_g': gain(ks[3], (L, HEAD_DIM)),
        'k_norm_g': gain(ks[4], (L, HEAD_DIM)),
        'conv_w': nrm(ks[5], (L, CONV_WIDTH, CONV_CH), CONV_WIDTH ** -0.5),
        'conv_b': nrm(ks[6], (L, CONV_CH), 0.02),
        'w_br_na': nrm(ks[7], (L, NA_WIDTH, D), NA_WIDTH ** -0.5),
        'w_br_gqa': nrm(ks[8], (L, GQA_Q_WIDTH, D), GQA_Q_WIDTH ** -0.5),
        'w_br_conv': nrm(ks[9], (L, CONV_CH, D), CONV_CH ** -0.5),
        'w_out': nrm(ks[10], (L, D, D), D ** -0.5),
        'pre_mix_g': gain(ks[11], (L, D)),
        'post_mix_g': gain(ks[12], (L, D)),
        'pre_ffn_g': gain(ks[13], (L, D)),
        'post_ffn_g': gain(ks[14], (L, D)),
        'w_ffn_gate': nrm(ks[15], (L, D, FFN_HIDDEN), D ** -0.5),
        'w_ffn_up': nrm(ks[16], (L, D, FFN_HIDDEN), D ** -0.5),
        'w_ffn_down': nrm(ks[17], (L, FFN_HIDDEN, D), FFN_HIDDEN ** -0.5),
    }


def reference(x, w_in, na_rpb, q_norm_g, k_norm_g, conv_w, conv_b, w_br_na, w_br_gqa,
              w_br_conv, w_out, pre_mix_g, post_mix_g, pre_ffn_g, post_ffn_g,
              w_ffn_gate, w_ffn_up, w_ffn_down):
    for l in range(DEPTH):
        x = hybrid_layer(x, w_in[l], na_rpb[l], q_norm_g[l], k_norm_g[l], conv_w[l], conv_b[l],
                         w_br_na[l], w_br_gqa[l], w_br_conv[l], w_out[l],
                         pre_mix_g[l], post_mix_g[l], pre_ffn_g[l], post_ffn_g[l],
                         w_ffn_gate[l], w_ffn_up[l], w_ffn_down[l])
    return x
```

```python
import functools
import math

import numpy as np
import jax
import jax.numpy as jnp
from jax import lax
from jax.experimental import pallas as pl
from jax.experimental.pallas import tpu as pltpu

GRID_W = 64
HEAD_DIM = 128
NA_HEADS = 8
NA_WIN_ROWS = 8
NA_WIN_COLS = 16
GQA_Q_HEADS = 8
GQA_KV_HEADS = 2
GQA_GROUP = GQA_Q_HEADS // GQA_KV_HEADS
ROPE_THETA = 10000.0
CONV_WIDTH = 3
RMS_EPS = 1e-6

NA_WIDTH = NA_HEADS * HEAD_DIM
GQA_Q_WIDTH = GQA_Q_HEADS * HEAD_DIM
GQA_KV_WIDTH = GQA_KV_HEADS * HEAD_DIM
OFF_NA_Q = 0
OFF_NA_K = NA_WIDTH
OFF_NA_V = 2 * NA_WIDTH
OFF_GQA_Q = 3 * NA_WIDTH
OFF_GQA_K = OFF_GQA_Q + GQA_Q_WIDTH
OFF_GQA_V = OFF_GQA_K + GQA_KV_WIDTH
OFF_CONV = OFF_GQA_V + GQA_KV_WIDTH

NA_GROUP_ROWS = 8
NA_BAND_ROWS = NA_GROUP_ROWS + NA_WIN_ROWS
MASK_VALUE = -1e30
LOG2E = math.log2(math.e)

V7X_VMEM_BYTES = 64 * 1024 * 1024
VMEM_LIMIT = 56 * 1024 * 1024

BF16 = jnp.bfloat16
F32 = jnp.float32


def _params(semantics):
    return pltpu.CompilerParams(dimension_semantics=semantics, vmem_limit_bytes=VMEM_LIMIT)


def _pick(n, candidates):
    for c in candidates:
        if n % c == 0:
            return c
    raise ValueError(f"no tile in {candidates} divides {n}")


def _rms_rows(x, g):
    ms = jnp.mean(x * x, axis=-1, keepdims=True)
    return x * lax.rsqrt(ms + RMS_EPS) * g


def _norm_matmul_kernel(x_ref, g_ref, w_ref, o_ref, h_ref, *, row_chunk):
    @pl.when(pl.program_id(1) == 0)
    def _():
        def body(c, carry):
            rows = pl.ds(pl.multiple_of(c * row_chunk, row_chunk), row_chunk)
            h_ref[rows, :] = _rms_rows(x_ref[rows, :], g_ref[...]).astype(BF16)
            return carry
        lax.fori_loop(0, x_ref.shape[0] // row_chunk, body, 0)

    o_ref[...] = jnp.dot(h_ref[...], w_ref[...], preferred_element_type=F32).astype(o_ref.dtype)


def _norm_matmul(x, g, w):
    t, d = x.shape
    n = w.shape[1]
    tm = _pick(t, (1024, 512, 256))
    tn = _pick(n, (1536, 1024, 768, 512, 256, 128))
    return pl.pallas_call(
        functools.partial(_norm_matmul_kernel, row_chunk=min(tm, 256)),
        out_shape=jax.ShapeDtypeStruct((t, n), BF16),
        grid=(t // tm, n // tn),
        in_specs=[
            pl.BlockSpec((tm, d), lambda i, j: (i, 0)),
            pl.BlockSpec((1, d), lambda i, j: (0, 0)),
            pl.BlockSpec((d, tn), lambda i, j: (0, j)),
        ],
        out_specs=pl.BlockSpec((tm, tn), lambda i, j: (i, j)),
        scratch_shapes=[pltpu.VMEM((tm, d), BF16)],
        compiler_params=_params(("parallel", "arbitrary")),
        name="norm_in_proj",
    )(x, g.reshape(1, d), w)


def _rope_tables(seq):
    half = HEAD_DIM // 2
    quarter = half // 2
    inv_freq = 1.0 / (ROPE_THETA ** (np.arange(quarter, dtype=np.float64) / quarter))
    t = np.arange(seq)
    ang_r = (t // GRID_W)[:, None] * inv_freq[None, :]
    ang_c = (t % GRID_W)[:, None] * inv_freq[None, :]
    cos = np.concatenate([np.cos(ang_r), np.cos(ang_r), np.cos(ang_c), np.cos(ang_c)], axis=1)
    sin = np.concatenate([-np.sin(ang_r), np.sin(ang_r), -np.sin(ang_c), np.sin(ang_c)], axis=1)
    return jnp.asarray(cos, F32), jnp.asarray(sin, F32)


def _qk_prep_kernel(x_ref, g_ref, cos_ref, sin_ref, o_ref):
    heads = x_ref.shape[1] // HEAD_DIM
    quarter = HEAD_DIM // 4
    cos = cos_ref[...]
    sin = sin_ref[...]
    lane = lax.broadcasted_iota(jnp.int32, cos.shape, 1)
    first = (lane % (2 * quarter)) < quarter
    for h in range(heads):
        cols = slice(h * HEAD_DIM, (h + 1) * HEAD_DIM)
        y = _rms_rows(x_ref[:, cols].astype(F32), g_ref[0, :, cols])
        partner = jnp.where(first, pltpu.roll(y, HEAD_DIM - quarter, 1), pltpu.roll(y, quarter, 1))
        o_ref[:, cols] = (y * cos + partner * sin).astype(o_ref.dtype)


def _qk_prep(proj, gains, cos, sin, seq):
    t = proj.shape[0]
    n_heads = GQA_Q_HEADS + GQA_KV_HEADS
    width = 2 * HEAD_DIM
    assert OFF_GQA_Q % width == 0 and n_heads % 2 == 0
    tm = _pick(seq, (2048, 1024, 512, 256))
    blocks_per_seq = seq // tm
    col0 = OFF_GQA_Q // width
    return pl.pallas_call(
        _qk_prep_kernel,
        out_shape=jax.ShapeDtypeStruct((t, n_heads * HEAD_DIM), BF16),
        grid=(t // tm, n_heads // 2),
        in_specs=[
            pl.BlockSpec((tm, width), lambda i, j: (i, col0 + j)),
            pl.BlockSpec((1, 1, width), lambda i, j: (j, 0, 0)),
            pl.BlockSpec((tm, HEAD_DIM), lambda i, j: (i % blocks_per_seq, 0)),
            pl.BlockSpec((tm, HEAD_DIM), lambda i, j: (i % blocks_per_seq, 0)),
        ],
        out_specs=pl.BlockSpec((tm, width), lambda i, j: (i, j)),
        compiler_params=_params(("parallel", "parallel")),
        name="gqa_qk_prep",
    )(proj, gains, cos, sin)


def _gqa_kernel(q_ref, k_ref, v_ref, o_ref, qs_ref, vaug_ref, acc_ref, *, tk):
    tq = q_ref.shape[0]
    seq = k_ref.shape[0]

    @pl.when(pl.program_id(2) == 0)
    def _():
        vaug_ref[:, :HEAD_DIM] = v_ref[...]
        vaug_ref[:, HEAD_DIM:] = jnp.ones((seq, HEAD_DIM), BF16)

    for h in range(GQA_GROUP):
        qs_ref[h * tq:(h + 1) * tq, :] = q_ref[:, h * HEAD_DIM:(h + 1) * HEAD_DIM]
    acc_ref[...] = jnp.zeros_like(acc_ref)

    def body(c, m):
        rows = pl.ds(pl.multiple_of(c * tk, tk), tk)
        s = lax.dot_general(qs_ref[...], k_ref[rows, :], (((1,), (1,)), ((), ())),
                            preferred_element_type=F32)
        m_new = jnp.maximum(m, jnp.max(s, axis=-1, keepdims=True))
        p = jnp.exp2(s - m_new).astype(BF16)
        pv = jnp.dot(p, vaug_ref[rows, :], preferred_element_type=F32)
        acc_ref[...] = acc_ref[...] * jnp.exp2(m - m_new) + pv
        return m_new

    lax.fori_loop(0, seq // tk, body, jnp.full((GQA_GROUP * tq, 1), -jnp.inf, F32))

    for h in range(GQA_GROUP):
        a = acc_ref[h * tq:(h + 1) * tq, :]
        o_ref[:, h * HEAD_DIM:(h + 1) * HEAD_DIM] = (a[:, :HEAD_DIM] / a[:, HEAD_DIM:]).astype(o_ref.dtype)


def _gqa_attention(qk, proj, batch, seq):
    t = qk.shape[0]
    tq = _pick(seq, (256, 128))
    tk = _pick(seq, (512, 256, 128))
    gw = GQA_GROUP * HEAD_DIM
    k_col0 = GQA_Q_WIDTH // HEAD_DIM
    v_col0 = OFF_GQA_V // HEAD_DIM
    q_blocks = seq // tq
    return pl.pallas_call(
        functools.partial(_gqa_kernel, tk=tk),
        out_shape=jax.ShapeDtypeStruct((t, GQA_Q_WIDTH), BF16),
        grid=(batch, GQA_KV_HEADS, q_blocks),
        in_specs=[
            pl.BlockSpec((tq, gw), lambda b, g, i: (b * q_blocks + i, g)),
            pl.BlockSpec((seq, HEAD_DIM), lambda b, g, i: (b, k_col0 + g)),
            pl.BlockSpec((seq, HEAD_DIM), lambda b, g, i: (b, v_col0 + g)),
        ],
        out_specs=pl.BlockSpec((tq, gw), lambda b, g, i: (b * q_blocks + i, g)),
        scratch_shapes=[
            pltpu.VMEM((GQA_GROUP * tq, HEAD_DIM), BF16),
            pltpu.VMEM((seq, 2 * HEAD_DIM), BF16),
            pltpu.VMEM((GQA_GROUP * tq, 2 * HEAD_DIM), F32),
        ],
        compiler_params=_params(("parallel", "parallel", "arbitrary")),
        name="gqa_flash",
    )(qk, qk, proj)


def _na_band_start(first_row, rows):
    return np.clip(first_row - NA_WIN_ROWS // 2, 0, rows - NA_BAND_ROWS)


def _na_bias_index(rows):
    gr, br = NA_GROUP_ROWS, NA_BAND_ROWS
    i = np.arange(gr)[:, None, None, None]
    w = np.arange(GRID_W)[None, :, None, None]
    j = np.arange(br)[None, None, :, None]
    c = np.arange(GRID_W)[None, None, None, :]
    col_start = np.clip(w - NA_WIN_COLS // 2, 0, GRID_W - NA_WIN_COLS)
    col_ok = (c >= col_start) & (c < col_start + NA_WIN_COLS)
    dc = c - w
    idx_r, idx_c, valid = [], [], []
    for r0 in (0, min(gr, rows - gr), rows - gr):
        r = r0 + i
        key_row = _na_band_start(r0, rows) + j
        row_start = np.clip(r - NA_WIN_ROWS // 2, 0, rows - NA_WIN_ROWS)
        ok = (key_row >= row_start) & (key_row < row_start + NA_WIN_ROWS) & col_ok
        dr = key_row - r
        shape = (gr * GRID_W, br * GRID_W)
        valid.append(np.broadcast_to(ok, (gr, GRID_W, br, GRID_W)).reshape(shape))
        idx_r.append(np.broadcast_to(np.clip(dr + NA_WIN_ROWS - 1, 0, 2 * NA_WIN_ROWS - 2),
                                     (gr, GRID_W, br, GRID_W)).reshape(shape))
        idx_c.append(np.broadcast_to(np.clip(dc + NA_WIN_COLS - 1, 0, 2 * NA_WIN_COLS - 2),
                                     (gr, GRID_W, br, GRID_W)).reshape(shape))
    return np.stack(idx_r), np.stack(idx_c), np.stack(valid)


def _na_bias_table(rpb, rows):
    idx_r, idx_c, valid = _na_bias_index(rows)
    table = rpb.astype(F32)[:, idx_r, idx_c]
    table = jnp.where(valid[None], table, MASK_VALUE)
    return table.transpose(1, 0, 2, 3)


def _na_kernel(q_ref, k_ref, v_ref, bias_ref, o_ref, *, rows):
    g = pl.program_id(2)
    n_groups = rows // NA_GROUP_ROWS
    band_rows = NA_BAND_ROWS * GRID_W
    start_row = jnp.clip(g * NA_GROUP_ROWS - NA_WIN_ROWS // 2, 0, rows - NA_BAND_ROWS)
    band = pl.ds(pl.multiple_of(start_row * GRID_W, GRID_W), band_rows)
    variant = jnp.where(g == 0, 0, jnp.where(g == n_groups - 1, 2, 1))
    scale = HEAD_DIM ** -0.5
    s = lax.dot_general(q_ref[...], k_ref[band, :], (((1,), (1,)), ((), ())),
                        preferred_element_type=F32)
    s = s * scale + bias_ref[variant]
    m = jnp.max(s, axis=-1, keepdims=True)
    p = jnp.exp(s - m)
    l = jnp.sum(p, axis=-1, keepdims=True)
    o = jnp.dot(p.astype(BF16), v_ref[band, :], preferred_element_type=F32)
    o_ref[...] = (o / l).astype(o_ref.dtype)


def _na_attention(proj, bias, batch, seq):
    t = proj.shape[0]
    rows = seq // GRID_W
    assert rows % NA_GROUP_ROWS == 0 and rows >= NA_BAND_ROWS
    n_groups = rows // NA_GROUP_ROWS
    gq = NA_GROUP_ROWS * GRID_W
    gk = NA_BAND_ROWS * GRID_W
    k_col0 = OFF_NA_K // HEAD_DIM
    v_col0 = OFF_NA_V // HEAD_DIM
    return pl.pallas_call(
        functools.partial(_na_kernel, rows=rows),
        out_shape=jax.ShapeDtypeStruct((t, NA_WIDTH), BF16),
        grid=(batch, NA_HEADS, n_groups),
        in_specs=[
            pl.BlockSpec((gq, HEAD_DIM), lambda b, h, g: (b * n_groups + g, h)),
            pl.BlockSpec((seq, HEAD_DIM), lambda b, h, g: (b, k_col0 + h)),
            pl.BlockSpec((seq, HEAD_DIM), lambda b, h, g: (b, v_col0 + h)),
            pl.BlockSpec((3, None, gq, gk), lambda b, h, g: (0, h, 0, 0)),
        ],
        out_specs=pl.BlockSpec((gq, HEAD_DIM), lambda b, h, g: (b * n_groups + g, h)),
        compiler_params=_params(("parallel", "parallel", "arbitrary")),
        name="na_attention",
    )(proj, proj, proj, bias)


def _conv_kernel(h_ref, b_ref, c_ref, hp_ref, cp_ref, hn_ref, cn_ref, w_ref, bias_ref, o_ref,
                 *, blocks_per_seq):
    i = pl.program_id(0)
    tm = h_ref.shape[0]
    u = c_ref[...].astype(F32) * h_ref[...].astype(F32)
    halo = hp_ref.shape[0]
    pos = i % blocks_per_seq
    u_prev_row = (cp_ref[halo - 1:halo, :].astype(F32) * hp_ref[halo - 1:halo, :].astype(F32))
    u_prev_row = jnp.where(pos == 0, 0.0, u_prev_row)
    u_next_row = (cn_ref[0:1, :].astype(F32) * hn_ref[0:1, :].astype(F32))
    u_next_row = jnp.where(pos == blocks_per_seq - 1, 0.0, u_next_row)
    row = lax.broadcasted_iota(jnp.int32, u.shape, 0)
    u_prev = jnp.where(row == 0, u_prev_row, pltpu.roll(u, 1, 0))
    u_next = jnp.where(row == tm - 1, u_next_row, pltpu.roll(u, tm - 1, 0))
    y = bias_ref[...] + u_prev * w_ref[0:1, :] + u * w_ref[1:2, :] + u_next * w_ref[2:3, :]
    o_ref[...] = (b_ref[...].astype(F32) * y).astype(o_ref.dtype)


def _short_conv(proj, conv_w, conv_b, seq):
    t = proj.shape[0]
    ch = conv_w.shape[1]
    assert CONV_WIDTH == 3
    tc = _pick(ch, (512, 256, 128))
    assert OFF_CONV % tc == 0
    tm = _pick(seq, (1024, 512, 256))
    halo = 16
    col_h = OFF_CONV // tc
    col_b = (OFF_CONV + ch) // tc
    col_c = (OFF_CONV + 2 * ch) // tc
    per_halo = tm // halo
    last_halo = t // halo - 1

    def prev_map(col):
        return lambda i, j: (jnp.maximum(i * per_halo - 1, 0), col + j)

    def next_map(col):
        return lambda i, j: (jnp.minimum((i + 1) * per_halo, last_halo), col + j)

    return pl.pallas_call(
        functools.partial(_conv_kernel, blocks_per_seq=seq // tm),
        out_shape=jax.ShapeDtypeStruct((t, ch), BF16),
        grid=(t // tm, ch // tc),
        in_specs=[
            pl.BlockSpec((tm, tc), lambda i, j: (i, col_h + j)),
            pl.BlockSpec((tm, tc), lambda i, j: (i, col_b + j)),
            pl.BlockSpec((tm, tc), lambda i, j: (i, col_c + j)),
            pl.BlockSpec((halo, tc), prev_map(col_h)),
            pl.BlockSpec((halo, tc), prev_map(col_c)),
            pl.BlockSpec((halo, tc), next_map(col_h)),
            pl.BlockSpec((halo, tc), next_map(col_c)),
            pl.BlockSpec((CONV_WIDTH, tc), lambda i, j: (0, j)),
            pl.BlockSpec((1, tc), lambda i, j: (0, j)),
        ],
        out_specs=pl.BlockSpec((tm, tc), lambda i, j: (i, j)),
        compiler_params=_params(("parallel", "parallel")),
        name="short_conv",
    )(proj, proj, proj, proj, proj, proj, proj, conv_w.astype(F32), conv_b.reshape(1, ch).astype(F32))


def _merge_kernel(x_ref, a_na_ref, a_gqa_ref, a_conv_ref, g_na_ref, g_gqa_ref, g_conv_ref,
                  w_na_ref, w_gqa_ref, w_conv_ref, w_out_ref, gain_ref, o_ref, merged_ref):
    c = pl.program_id(1)
    tn = w_na_ref.shape[1]

    def branch(a_ref, w_ref, gate_ref):
        y = jnp.dot(a_ref[...], w_ref[...], preferred_element_type=F32)
        return jax.nn.sigmoid(gate_ref[...].astype(F32)) * y

    merged = (branch(a_na_ref, w_na_ref, g_na_ref) + branch(a_gqa_ref, w_gqa_ref, g_gqa_ref)
              + branch(a_conv_ref, w_conv_ref, g_conv_ref))
    merged_ref[:, pl.ds(pl.multiple_of(c * tn, tn), tn)] = merged.astype(BF16)

    @pl.when(c == pl.num_programs(1) - 1)
    def _():
        y = jnp.dot(merged_ref[...], w_out_ref[...], preferred_element_type=F32)
        o_ref[...] = x_ref[...] + _rms_rows(y, gain_ref[...])


def _merge(x, a_na, a_gqa, a_conv, proj, w_na, w_gqa, w_conv, w_out, gain, gate_off):
    t, d = x.shape
    ch = a_conv.shape[1]
    tm = _pick(t, (512, 256))
    tn = _pick(d, (512, 256, 128))
    assert gate_off % tn == 0
    g0 = gate_off // tn
    nc = d // tn
    const = lambda i, c: (0, 0)
    return pl.pallas_call(
        _merge_kernel,
        out_shape=jax.ShapeDtypeStruct((t, d), F32),
        grid=(t // tm, nc),
        in_specs=[
            pl.BlockSpec((tm, d), lambda i, c: (i, 0)),
            pl.BlockSpec((tm, NA_WIDTH), lambda i, c: (i, 0)),
            pl.BlockSpec((tm, GQA_Q_WIDTH), lambda i, c: (i, 0)),
            pl.BlockSpec((tm, ch), lambda i, c: (i, 0)),
            pl.BlockSpec((tm, tn), lambda i, c: (i, g0 + c)),
            pl.BlockSpec((tm, tn), lambda i, c: (i, g0 + nc + c)),
            pl.BlockSpec((tm, tn), lambda i, c: (i, g0 + 2 * nc + c)),
            pl.BlockSpec((NA_WIDTH, tn), lambda i, c: (0, c)),
            pl.BlockSpec((GQA_Q_WIDTH, tn), lambda i, c: (0, c)),
            pl.BlockSpec((ch, tn), lambda i, c: (0, c)),
            pl.BlockSpec((d, d), const, pipeline_mode=pl.Buffered(1)),
            pl.BlockSpec((1, d), const),
        ],
        out_specs=pl.BlockSpec((tm, d), lambda i, c: (i, 0)),
        scratch_shapes=[pltpu.VMEM((tm, d), BF16)],
        compiler_params=_params(("parallel", "arbitrary")),
        name="merge_out_proj",
    )(x, a_na, a_gqa, a_conv, proj, proj, proj, w_na, w_gqa, w_conv, w_out, gain.reshape(1, d))


def _ffn_kernel(x_ref, g_pre_ref, wg_ref, wu_ref, wd_ref, g_post_ref, o_ref, h_ref, acc_ref,
                *, row_chunk):
    f = pl.program_id(1)

    @pl.when(f == 0)
    def _():
        def body(c, carry):
            rows = pl.ds(pl.multiple_of(c * row_chunk, row_chunk), row_chunk)
            h_ref[rows, :] = _rms_rows(x_ref[rows, :], g_pre_ref[...]).astype(BF16)
            return carry
        lax.fori_loop(0, x_ref.shape[0] // row_chunk, body, 0)
        acc_ref[...] = jnp.zeros_like(acc_ref)

    h = h_ref[...]
    a = jnp.dot(h, wg_ref[...], preferred_element_type=F32)
    b = jnp.dot(h, wu_ref[...], preferred_element_type=F32)
    act = (a * jax.nn.sigmoid(a) * b).astype(BF16)
    acc_ref[...] += jnp.dot(act, wd_ref[...], preferred_element_type=F32)

    @pl.when(f == pl.num_programs(1) - 1)
    def _():
        o_ref[...] = x_ref[...] + _rms_rows(acc_ref[...], g_post_ref[...])


def _ffn(x, g_pre, wg, wu, wd, g_post):
    t, d = x.shape
    hidden = wg.shape[1]
    tm = _pick(t, (512, 256))
    tf = _pick(hidden, (512, 256, 128))
    return pl.pallas_call(
        functools.partial(_ffn_kernel, row_chunk=min(tm, 256)),
        out_shape=jax.ShapeDtypeStruct((t, d), F32),
        grid=(t // tm, hidden // tf),
        in_specs=[
            pl.BlockSpec((tm, d), lambda i, f: (i, 0)),
            pl.BlockSpec((1, d), lambda i, f: (0, 0)),
            pl.BlockSpec((d, tf), lambda i, f: (0, f)),
            pl.BlockSpec((d, tf), lambda i, f: (0, f)),
            pl.BlockSpec((tf, d), lambda i, f: (f, 0)),
            pl.BlockSpec((1, d), lambda i, f: (0, 0)),
        ],
        out_specs=pl.BlockSpec((tm, d), lambda i, f: (i, 0)),
        scratch_shapes=[pltpu.VMEM((tm, d), BF16), pltpu.VMEM((tm, d), F32)],
        compiler_params=_params(("parallel", "arbitrary")),
        name="ffn_swiglu",
    )(x, g_pre.reshape(1, d), wg, wu, wd, g_post.reshape(1, d))


def _layer(x, batch, seq, cos, sin, w_in, na_rpb, q_norm_g, k_norm_g, conv_w, conv_b, w_br_na,
           w_br_gqa, w_br_conv, w_out, pre_mix_g, post_mix_g, pre_ffn_g, post_ffn_g,
           w_ffn_gate, w_ffn_up, w_ffn_down):
    ch = conv_w.shape[1]
    proj = _norm_matmul(x, pre_mix_g, w_in.astype(BF16))

    q_gain = q_norm_g.astype(F32) * (HEAD_DIM ** -0.5 * LOG2E)
    gains = jnp.concatenate([jnp.tile(q_gain, GQA_Q_HEADS), jnp.tile(k_norm_g.astype(F32), GQA_KV_HEADS)])
    gains = gains.reshape(-1, 1, 2 * HEAD_DIM)
    qk = _qk_prep(proj, gains, cos, sin, seq)
    a_gqa = _gqa_attention(qk, proj, batch, seq)

    a_na = _na_attention(proj, _na_bias_table(na_rpb, seq // GRID_W), batch, seq)
    a_conv = _short_conv(proj, conv_w, conv_b, seq)

    x = _merge(x, a_na, a_gqa, a_conv, proj, w_br_na.astype(BF16), w_br_gqa.astype(BF16),
               w_br_conv.astype(BF16), w_out.astype(BF16), post_mix_g, OFF_CONV + 3 * ch)
    return _ffn(x, pre_ffn_g, w_ffn_gate.astype(BF16), w_ffn_up.astype(BF16),
                w_ffn_down.astype(BF16), post_ffn_g)


def kernel(x, w_in, na_rpb, q_norm_g, k_norm_g, conv_w, conv_b, w_br_na, w_br_gqa, w_br_conv, w_out,
           pre_mix_g, post_mix_g, pre_ffn_g, post_ffn_g, w_ffn_gate, w_ffn_up, w_ffn_down):
    batch, seq, d = x.shape
    cos, sin = _rope_tables(seq)
    y = x.reshape(batch * seq, d)
    for l in range(w_in.shape[0]):
        y = _layer(y, batch, seq, cos, sin, w_in[l], na_rpb[l], q_norm_g[l], k_norm_g[l], conv_w[l],
                   conv_b[l], w_br_na[l], w_br_gqa[l], w_br_conv[l], w_out[l], pre_mix_g[l],
                   post_mix_g[l], pre_ffn_g[l], post_ffn_g[l], w_ffn_gate[l], w_ffn_up[l],
                   w_ffn_down[l])
    return y.reshape(batch, seq, d)
```

```python
import functools
import math

import numpy as np
import jax
import jax.numpy as jnp
from jax import lax
from jax.experimental import pallas as pl
from jax.experimental.pallas import tpu as pltpu

GRID_W = 64
HEAD_DIM = 128
NA_HEADS = 8
NA_WIN_ROWS = 8
NA_WIN_COLS = 16
GQA_Q_HEADS = 8
GQA_KV_HEADS = 2
GQA_GROUP = GQA_Q_HEADS // GQA_KV_HEADS
ROPE_THETA = 10000.0
CONV_WIDTH = 3
RMS_EPS = 1e-6

NA_WIDTH = NA_HEADS * HEAD_DIM
GQA_Q_WIDTH = GQA_Q_HEADS * HEAD_DIM
GQA_KV_WIDTH = GQA_KV_HEADS * HEAD_DIM
OFF_NA_Q = 0
OFF_NA_K = NA_WIDTH
OFF_NA_V = 2 * NA_WIDTH
OFF_GQA_Q = 3 * NA_WIDTH
OFF_GQA_K = OFF_GQA_Q + GQA_Q_WIDTH
OFF_GQA_V = OFF_GQA_K + GQA_KV_WIDTH
OFF_CONV = OFF_GQA_V + GQA_KV_WIDTH

NA_GROUP_ROWS = 4
NA_BAND_ROWS = NA_GROUP_ROWS + NA_WIN_ROWS
NA_GROUPS_PER_STEP = 4
MASK_VALUE = -1e30
LOG2E = math.log2(math.e)

V7X_VMEM_BYTES = 64 * 1024 * 1024
VMEM_LIMIT = 56 * 1024 * 1024

BF16 = jnp.bfloat16
F32 = jnp.float32


def _params(semantics):
    return pltpu.CompilerParams(dimension_semantics=semantics, vmem_limit_bytes=VMEM_LIMIT)


def _pick(n, candidates):
    for c in candidates:
        if n % c == 0:
            return c
    raise ValueError(f"no tile in {candidates} divides {n}")


def _rms_rows(x, g):
    ms = jnp.mean(x * x, axis=-1, keepdims=True)
    return x * lax.rsqrt(ms + RMS_EPS) * g


def _norm_matmul_kernel(x_ref, g_ref, w_ref, o_ref, h_ref, *, row_chunk):
    @pl.when(pl.program_id(1) == 0)
    def _():
        def body(c, carry):
            rows = pl.ds(pl.multiple_of(c * row_chunk, row_chunk), row_chunk)
            h_ref[rows, :] = _rms_rows(x_ref[rows, :], g_ref[...]).astype(BF16)
            return carry
        lax.fori_loop(0, x_ref.shape[0] // row_chunk, body, 0)

    o_ref[...] = jnp.dot(h_ref[...], w_ref[...], preferred_element_type=F32).astype(o_ref.dtype)


def _norm_matmul(x, g, w):
    t, d = x.shape
    n = w.shape[1]
    tm = _pick(t, (1024, 512, 256))
    tn = _pick(n, (1536, 1024, 768, 512, 256, 128))
    return pl.pallas_call(
        functools.partial(_norm_matmul_kernel, row_chunk=min(tm, 256)),
        out_shape=jax.ShapeDtypeStruct((t, n), BF16),
        grid=(t // tm, n // tn),
        in_specs=[
            pl.BlockSpec((tm, d), lambda i, j: (i, 0)),
            pl.BlockSpec((1, d), lambda i, j: (0, 0)),
            pl.BlockSpec((d, tn), lambda i, j: (0, j)),
        ],
        out_specs=pl.BlockSpec((tm, tn), lambda i, j: (i, j)),
        scratch_shapes=[pltpu.VMEM((tm, d), BF16)],
        compiler_params=_params(("parallel", "arbitrary")),
        name="norm_in_proj",
    )(x, g.reshape(1, d), w)


def _rope_tables(seq):
    half = HEAD_DIM // 2
    quarter = half // 2
    inv_freq = 1.0 / (ROPE_THETA ** (np.arange(quarter, dtype=np.float64) / quarter))
    t = np.arange(seq)
    ang_r = (t // GRID_W)[:, None] * inv_freq[None, :]
    ang_c = (t % GRID_W)[:, None] * inv_freq[None, :]
    cos = np.concatenate([np.cos(ang_r), np.cos(ang_r), np.cos(ang_c), np.cos(ang_c)], axis=1)
    sin = np.concatenate([-np.sin(ang_r), np.sin(ang_r), -np.sin(ang_c), np.sin(ang_c)], axis=1)
    return jnp.asarray(cos, F32), jnp.asarray(sin, F32)


def _qk_prep_kernel(x_ref, g_ref, cos_ref, sin_ref, o_ref):
    heads = x_ref.shape[1] // HEAD_DIM
    quarter = HEAD_DIM // 4
    cos = cos_ref[...]
    sin = sin_ref[...]
    lane = lax.broadcasted_iota(jnp.int32, cos.shape, 1)
    first = (lane % (2 * quarter)) < quarter
    for h in range(heads):
        cols = slice(h * HEAD_DIM, (h + 1) * HEAD_DIM)
        y = _rms_rows(x_ref[:, cols].astype(F32), g_ref[0, :, cols])
        partner = jnp.where(first, pltpu.roll(y, HEAD_DIM - quarter, 1), pltpu.roll(y, quarter, 1))
        o_ref[:, cols] = (y * cos + partner * sin).astype(o_ref.dtype)


def _qk_prep(proj, gains, cos, sin, seq):
    t = proj.shape[0]
    n_heads = GQA_Q_HEADS + GQA_KV_HEADS
    width = 2 * HEAD_DIM
    assert OFF_GQA_Q % width == 0 and n_heads % 2 == 0
    tm = _pick(seq, (2048, 1024, 512, 256))
    blocks_per_seq = seq // tm
    col0 = OFF_GQA_Q // width
    return pl.pallas_call(
        _qk_prep_kernel,
        out_shape=jax.ShapeDtypeStruct((t, n_heads * HEAD_DIM), BF16),
        grid=(t // tm, n_heads // 2),
        in_specs=[
            pl.BlockSpec((tm, width), lambda i, j: (i, col0 + j)),
            pl.BlockSpec((1, 1, width), lambda i, j: (j, 0, 0)),
            pl.BlockSpec((tm, HEAD_DIM), lambda i, j: (i % blocks_per_seq, 0)),
            pl.BlockSpec((tm, HEAD_DIM), lambda i, j: (i % blocks_per_seq, 0)),
        ],
        out_specs=pl.BlockSpec((tm, width), lambda i, j: (i, j)),
        compiler_params=_params(("parallel", "parallel")),
        name="gqa_qk_prep",
    )(proj, gains, cos, sin)


def _gqa_kernel(q_ref, k_ref, v_ref, o_ref, qs_ref, vaug_ref, acc_ref, s_ref, m_ref, *, tk):
    tq = q_ref.shape[0]
    seq = k_ref.shape[0]
    n_chunks = seq // tk

    @pl.when(pl.program_id(2) == 0)
    def _():
        vaug_ref[:, :HEAD_DIM] = v_ref[...]
        vaug_ref[:, HEAD_DIM:] = jnp.ones((seq, HEAD_DIM), BF16)

    for h in range(GQA_GROUP):
        qs_ref[h * tq:(h + 1) * tq, :] = q_ref[:, h * HEAD_DIM:(h + 1) * HEAD_DIM]
    acc_ref[...] = jnp.zeros_like(acc_ref)
    m_ref[...] = jnp.full_like(m_ref, -jnp.inf)

    def chunk(c):
        return pl.ds(pl.multiple_of(c * tk, tk), tk)

    def scores(c):
        return lax.dot_general(qs_ref[...], k_ref[chunk(c), :], (((1,), (1,)), ((), ())),
                               preferred_element_type=F32)

    def accumulate(c, slot):
        blocks = [s_ref[slot, :, j * HEAD_DIM:(j + 1) * HEAD_DIM] for j in range(tk // HEAD_DIM)]
        m_old = m_ref[...]
        m_new = jnp.maximum(m_old, jnp.max(functools.reduce(jnp.maximum, blocks), axis=-1, keepdims=True))
        m_ref[...] = m_new
        p = jnp.concatenate([jnp.exp2(b - m_new).astype(BF16) for b in blocks], axis=-1)
        pv = jnp.dot(p, vaug_ref[chunk(c), :], preferred_element_type=F32)
        alpha = jnp.exp2(m_old - m_new)
        for j in range(2):
            cols = slice(j * HEAD_DIM, (j + 1) * HEAD_DIM)
            acc_ref[:, cols] = acc_ref[:, cols] * alpha + pv[:, cols]

    assert n_chunks % 2 == 0
    s_ref[0] = scores(0)

    def body(t, carry):
        s_ref[1] = scores(2 * t + 1)
        accumulate(2 * t, 0)
        s_ref[0] = scores(2 * t + 2)
        accumulate(2 * t + 1, 1)
        return carry

    lax.fori_loop(0, n_chunks // 2 - 1, body, 0)
    s_ref[1] = scores(n_chunks - 1)
    accumulate(n_chunks - 2, 0)
    accumulate(n_chunks - 1, 1)

    for h in range(GQA_GROUP):
        a = acc_ref[h * tq:(h + 1) * tq, :]
        o_ref[:, h * HEAD_DIM:(h + 1) * HEAD_DIM] = (a[:, :HEAD_DIM] / a[:, HEAD_DIM:]).astype(o_ref.dtype)


def _gqa_attention(qk, proj, batch, seq):
    t = qk.shape[0]
    tq = _pick(seq, (256, 128))
    tk = _pick(seq, (512, 256, 128))
    gw = GQA_GROUP * HEAD_DIM
    k_col0 = GQA_Q_WIDTH // HEAD_DIM
    v_col0 = OFF_GQA_V // HEAD_DIM
    q_blocks = seq // tq
    return pl.pallas_call(
        functools.partial(_gqa_kernel, tk=tk),
        out_shape=jax.ShapeDtypeStruct((t, GQA_Q_WIDTH), BF16),
        grid=(batch, GQA_KV_HEADS, q_blocks),
        in_specs=[
            pl.BlockSpec((tq, gw), lambda b, g, i: (b * q_blocks + i, g)),
            pl.BlockSpec((seq, HEAD_DIM), lambda b, g, i: (b, k_col0 + g)),
            pl.BlockSpec((seq, HEAD_DIM), lambda b, g, i: (b, v_col0 + g)),
        ],
        out_specs=pl.BlockSpec((tq, gw), lambda b, g, i: (b * q_blocks + i, g)),
        scratch_shapes=[
            pltpu.VMEM((GQA_GROUP * tq, HEAD_DIM), BF16),
            pltpu.VMEM((seq, 2 * HEAD_DIM), BF16),
            pltpu.VMEM((GQA_GROUP * tq, 2 * HEAD_DIM), F32),
            pltpu.VMEM((2, GQA_GROUP * tq, tk), F32),
            pltpu.VMEM((GQA_GROUP * tq, HEAD_DIM), F32),
        ],
        compiler_params=_params(("parallel", "parallel", "arbitrary")),
        name="gqa_flash",
    )(qk, qk, proj)


NA_PAIR_ROWS = 2 * NA_WIN_ROWS
NA_MASK_BOTH, NA_MASK_LEFT, NA_MASK_RIGHT = 0, 1, 2


def _na_block_plan(rows):
    gr, br = NA_GROUP_ROWS, NA_BAND_ROWS
    plan = []
    for r0 in (0, min(gr, rows - gr), rows - gr):
        band_start = int(np.clip(r0 - NA_WIN_ROWS // 2, 0, rows - br))
        per_row = []
        for i in range(gr):
            r = r0 + i
            row_start = int(np.clip(r - NA_WIN_ROWS // 2, 0, rows - NA_WIN_ROWS))
            blocks = []
            for jp in range(br // 2):
                key_row = band_start + 2 * jp
                ok_l = row_start <= key_row < row_start + NA_WIN_ROWS
                ok_r = row_start <= key_row + 1 < row_start + NA_WIN_ROWS
                if not (ok_l or ok_r):
                    blocks.append(None)
                    continue
                kind = NA_MASK_BOTH if (ok_l and ok_r) else (NA_MASK_LEFT if ok_l else NA_MASK_RIGHT)
                blocks.append((key_row - r + NA_WIN_ROWS, kind))
            per_row.append(blocks)
        plan.append(per_row)
    return plan


def _na_pair_table(rpb):
    h, nr, nc = rpb.shape
    rpb = jnp.pad(rpb.astype(F32), ((0, 0), (0, 0), (0, GRID_W - nc)))
    zero = jnp.zeros((h, 1, GRID_W), F32)
    left = jnp.concatenate([zero, rpb], axis=1)
    right = jnp.concatenate([rpb, zero], axis=1)
    return jnp.concatenate([left, right], axis=2)


def _na_col_masks():
    w = np.arange(GRID_W)[:, None]
    c = np.arange(2 * GRID_W)[None, :] % GRID_W
    col_start = np.clip(w - NA_WIN_COLS // 2, 0, GRID_W - NA_WIN_COLS)
    base = np.where((c >= col_start) & (c < col_start + NA_WIN_COLS), 0.0, MASK_VALUE)
    left_half = np.arange(2 * GRID_W)[None, :] < GRID_W
    masks = np.stack([base, np.where(left_half, base, MASK_VALUE), np.where(left_half, MASK_VALUE, base)])
    return jnp.asarray(masks, F32)


def _na_kernel(q_ref, k_ref, v_ref, pair_ref, mask_ref, o_ref, tile_ref, table_ref, *, rows):
    g = pl.program_id(2)
    n_groups = rows // NA_GROUP_ROWS
    blk_w = 2 * GRID_W

    @pl.when(g == 0)
    def _():
        for p in range(NA_PAIR_ROWS):
            row = jnp.broadcast_to(pair_ref[p:p + 1, :], (GRID_W, blk_w))
            tile_ref[p] = pltpu.roll(row, blk_w - (NA_WIN_COLS - 1), 1, stride=1, stride_axis=0)
        for v, per_row in enumerate(_na_block_plan(rows)):
            for i, blocks in enumerate(per_row):
                for jp, blk in enumerate(blocks):
                    dst = (v, slice(i * GRID_W, (i + 1) * GRID_W), slice(jp * blk_w, (jp + 1) * blk_w))
                    if blk is None:
                        table_ref[dst] = jnp.full((GRID_W, blk_w), MASK_VALUE, F32)
                    else:
                        table_ref[dst] = (tile_ref[blk[0]] + mask_ref[blk[1]]) * LOG2E

    gq = NA_GROUP_ROWS * GRID_W
    band_rows = NA_BAND_ROWS * GRID_W
    for u in range(q_ref.shape[0] // gq):
        gi = g * (q_ref.shape[0] // gq) + u
        start_row = jnp.clip(gi * NA_GROUP_ROWS - NA_WIN_ROWS // 2, 0, rows - NA_BAND_ROWS)
        band = pl.ds(pl.multiple_of(start_row * GRID_W, GRID_W), band_rows)
        variant = jnp.where(gi == 0, 0, jnp.where(gi == n_groups - 1, 2, 1))
        s = lax.dot_general(q_ref[u * gq:(u + 1) * gq, :], k_ref[band, :], (((1,), (1,)), ((), ())),
                            preferred_element_type=F32)
        s = s * (HEAD_DIM ** -0.5 * LOG2E) + table_ref[variant]
        p = jnp.exp2(s - jnp.max(s, axis=-1, keepdims=True))
        l = jnp.sum(p, axis=-1, keepdims=True)
        o = jnp.dot(p.astype(BF16), v_ref[band, :], preferred_element_type=F32)
        o_ref[u * gq:(u + 1) * gq, :] = (o / l).astype(o_ref.dtype)


def _na_attention(proj, rpb, batch, seq):
    t = proj.shape[0]
    rows = seq // GRID_W
    assert NA_GROUP_ROWS % 4 == 0 and rows % NA_GROUP_ROWS == 0 and rows >= NA_BAND_ROWS
    assert 2 * GRID_W == HEAD_DIM
    n_groups = rows // NA_GROUP_ROWS
    gq = NA_GROUP_ROWS * GRID_W
    gk = NA_BAND_ROWS * GRID_W
    per_step = _pick(n_groups, (NA_GROUPS_PER_STEP, 2, 1))
    n_steps = n_groups // per_step
    k_col0 = OFF_NA_K // HEAD_DIM
    v_col0 = OFF_NA_V // HEAD_DIM
    return pl.pallas_call(
        functools.partial(_na_kernel, rows=rows),
        out_shape=jax.ShapeDtypeStruct((t, NA_WIDTH), BF16),
        grid=(batch, NA_HEADS, n_steps),
        in_specs=[
            pl.BlockSpec((per_step * gq, HEAD_DIM), lambda b, h, g: (b * n_steps + g, h)),
            pl.BlockSpec((seq, HEAD_DIM), lambda b, h, g: (b, k_col0 + h)),
            pl.BlockSpec((seq, HEAD_DIM), lambda b, h, g: (b, v_col0 + h)),
            pl.BlockSpec((None, NA_PAIR_ROWS, 2 * GRID_W), lambda b, h, g: (h, 0, 0)),
            pl.BlockSpec((3, GRID_W, 2 * GRID_W), lambda b, h, g: (0, 0, 0)),
        ],
        out_specs=pl.BlockSpec((per_step * gq, HEAD_DIM), lambda b, h, g: (b * n_steps + g, h)),
        scratch_shapes=[
            pltpu.VMEM((NA_PAIR_ROWS, GRID_W, 2 * GRID_W), F32),
            pltpu.VMEM((3, gq, gk), F32),
        ],
        compiler_params=_params(("parallel", "parallel", "arbitrary")),
        name="na_attention",
    )(proj, proj, proj, _na_pair_table(rpb), _na_col_masks())


def _conv_kernel(h_ref, b_ref, c_ref, hp_ref, cp_ref, hn_ref, cn_ref, w_ref, bias_ref, o_ref,
                 *, blocks_per_seq):
    i = pl.program_id(0)
    tm = h_ref.shape[0]
    u = c_ref[...].astype(F32) * h_ref[...].astype(F32)
    halo = hp_ref.shape[0]
    pos = i % blocks_per_seq
    u_prev_row = (cp_ref[halo - 1:halo, :].astype(F32) * hp_ref[halo - 1:halo, :].astype(F32))
    u_prev_row = jnp.where(pos == 0, 0.0, u_prev_row)
    u_next_row = (cn_ref[0:1, :].astype(F32) * hn_ref[0:1, :].astype(F32))
    u_next_row = jnp.where(pos == blocks_per_seq - 1, 0.0, u_next_row)
    row = lax.broadcasted_iota(jnp.int32, u.shape, 0)
    u_prev = jnp.where(row == 0, u_prev_row, pltpu.roll(u, 1, 0))
    u_next = jnp.where(row == tm - 1, u_next_row, pltpu.roll(u, tm - 1, 0))
    y = bias_ref[...] + u_prev * w_ref[0:1, :] + u * w_ref[1:2, :] + u_next * w_ref[2:3, :]
    o_ref[...] = (b_ref[...].astype(F32) * y).astype(o_ref.dtype)


def _short_conv(proj, conv_w, conv_b, seq):
    t = proj.shape[0]
    ch = conv_w.shape[1]
    assert CONV_WIDTH == 3
    tc = _pick(ch, (512, 256, 128))
    assert OFF_CONV % tc == 0
    tm = _pick(seq, (1024, 512, 256))
    halo = 16
    col_h = OFF_CONV // tc
    col_b = (OFF_CONV + ch) // tc
    col_c = (OFF_CONV + 2 * ch) // tc
    per_halo = tm // halo
    last_halo = t // halo - 1

    def prev_map(col):
        return lambda i, j: (jnp.maximum(i * per_halo - 1, 0), col + j)

    def next_map(col):
        return lambda i, j: (jnp.minimum((i + 1) * per_halo, last_halo), col + j)

    return pl.pallas_call(
        functools.partial(_conv_kernel, blocks_per_seq=seq // tm),
        out_shape=jax.ShapeDtypeStruct((t, ch), BF16),
        grid=(t // tm, ch // tc),
        in_specs=[
            pl.BlockSpec((tm, tc), lambda i, j: (i, col_h + j)),
            pl.BlockSpec((tm, tc), lambda i, j: (i, col_b + j)),
            pl.BlockSpec((tm, tc), lambda i, j: (i, col_c + j)),
            pl.BlockSpec((halo, tc), prev_map(col_h)),
            pl.BlockSpec((halo, tc), prev_map(col_c)),
            pl.BlockSpec((halo, tc), next_map(col_h)),
            pl.BlockSpec((halo, tc), next_map(col_c)),
            pl.BlockSpec((CONV_WIDTH, tc), lambda i, j: (0, j)),
            pl.BlockSpec((1, tc), lambda i, j: (0, j)),
        ],
        out_specs=pl.BlockSpec((tm, tc), lambda i, j: (i, j)),
        compiler_params=_params(("parallel", "parallel")),
        name="short_conv",
    )(proj, proj, proj, proj, proj, proj, proj, conv_w.astype(F32), conv_b.reshape(1, ch).astype(F32))


def _merge_kernel(x_ref, a_na_ref, a_gqa_ref, a_conv_ref, g_na_ref, g_gqa_ref, g_conv_ref,
                  w_na_ref, w_gqa_ref, w_conv_ref, w_out_ref, gain_ref, o_ref, merged_ref):
    c = pl.program_id(1)
    tn = w_na_ref.shape[1]

    def branch(a_ref, w_ref, gate_ref):
        y = jnp.dot(a_ref[...], w_ref[...], preferred_element_type=F32)
        return jax.nn.sigmoid(gate_ref[...].astype(F32)) * y

    merged = (branch(a_na_ref, w_na_ref, g_na_ref) + branch(a_gqa_ref, w_gqa_ref, g_gqa_ref)
              + branch(a_conv_ref, w_conv_ref, g_conv_ref))
    merged_ref[:, pl.ds(pl.multiple_of(c * tn, tn), tn)] = merged.astype(BF16)

    @pl.when(c == pl.num_programs(1) - 1)
    def _():
        y = jnp.dot(merged_ref[...], w_out_ref[...], preferred_element_type=F32)
        o_ref[...] = x_ref[...] + _rms_rows(y, gain_ref[...])


def _merge(x, a_na, a_gqa, a_conv, proj, w_na, w_gqa, w_conv, w_out, gain, gate_off):
    t, d = x.shape
    ch = a_conv.shape[1]
    tm = _pick(t, (512, 256))
    tn = _pick(d, (512, 256, 128))
    assert gate_off % tn == 0
    g0 = gate_off // tn
    nc = d // tn
    const = lambda i, c: (0, 0)
    return pl.pallas_call(
        _merge_kernel,
        out_shape=jax.ShapeDtypeStruct((t, d), F32),
        grid=(t // tm, nc),
        in_specs=[
            pl.BlockSpec((tm, d), lambda i, c: (i, 0)),
            pl.BlockSpec((tm, NA_WIDTH), lambda i, c: (i, 0)),
            pl.BlockSpec((tm, GQA_Q_WIDTH), lambda i, c: (i, 0)),
            pl.BlockSpec((tm, ch), lambda i, c: (i, 0)),
            pl.BlockSpec((tm, tn), lambda i, c: (i, g0 + c)),
            pl.BlockSpec((tm, tn), lambda i, c: (i, g0 + nc + c)),
            pl.BlockSpec((tm, tn), lambda i, c: (i, g0 + 2 * nc + c)),
            pl.BlockSpec((NA_WIDTH, tn), lambda i, c: (0, c)),
            pl.BlockSpec((GQA_Q_WIDTH, tn), lambda i, c: (0, c)),
            pl.BlockSpec((ch, tn), lambda i, c: (0, c)),
            pl.BlockSpec((d, d), const, pipeline_mode=pl.Buffered(1)),
            pl.BlockSpec((1, d), const),
        ],
        out_specs=pl.BlockSpec((tm, d), lambda i, c: (i, 0)),
        scratch_shapes=[pltpu.VMEM((tm, d), BF16)],
        compiler_params=_params(("parallel", "arbitrary")),
        name="merge_out_proj",
    )(x, a_na, a_gqa, a_conv, proj, proj, proj, w_na, w_gqa, w_conv, w_out, gain.reshape(1, d))


def _ffn_kernel(x_ref, g_pre_ref, wg_ref, wu_ref, wd_ref, g_post_ref, o_ref, h_ref, acc_ref,
                *, row_chunk):
    f = pl.program_id(1)

    @pl.when(f == 0)
    def _():
        def body(c, carry):
            rows = pl.ds(pl.multiple_of(c * row_chunk, row_chunk), row_chunk)
            h_ref[rows, :] = _rms_rows(x_ref[rows, :], g_pre_ref[...]).astype(BF16)
            return carry
        lax.fori_loop(0, x_ref.shape[0] // row_chunk, body, 0)
        acc_ref[...] = jnp.zeros_like(acc_ref)

    h = h_ref[...]
    a = jnp.dot(h, wg_ref[...], preferred_element_type=F32)
    b = jnp.dot(h, wu_ref[...], preferred_element_type=F32)
    act = (a * jax.nn.sigmoid(a) * b).astype(BF16)
    acc_ref[...] += jnp.dot(act, wd_ref[...], preferred_element_type=F32)

    @pl.when(f == pl.num_programs(1) - 1)
    def _():
        o_ref[...] = x_ref[...] + _rms_rows(acc_ref[...], g_post_ref[...])


def _ffn(x, g_pre, wg, wu, wd, g_post):
    t, d = x.shape
    hidden = wg.shape[1]
    tm = _pick(t, (512, 256))
    tf = _pick(hidden, (512, 256, 128))
    return pl.pallas_call(
        functools.partial(_ffn_kernel, row_chunk=min(tm, 256)),
        out_shape=jax.ShapeDtypeStruct((t, d), F32),
        grid=(t // tm, hidden // tf),
        in_specs=[
            pl.BlockSpec((tm, d), lambda i, f: (i, 0)),
            pl.BlockSpec((1, d), lambda i, f: (0, 0)),
            pl.BlockSpec((d, tf), lambda i, f: (0, f)),
            pl.BlockSpec((d, tf), lambda i, f: (0, f)),
            pl.BlockSpec((tf, d), lambda i, f: (f, 0)),
            pl.BlockSpec((1, d), lambda i, f: (0, 0)),
        ],
        out_specs=pl.BlockSpec((tm, d), lambda i, f: (i, 0)),
        scratch_shapes=[pltpu.VMEM((tm, d), BF16), pltpu.VMEM((tm, d), F32)],
        compiler_params=_params(("parallel", "arbitrary")),
        name="ffn_swiglu",
    )(x, g_pre.reshape(1, d), wg, wu, wd, g_post.reshape(1, d))


def _layer(x, batch, seq, cos, sin, w_in, na_rpb, q_norm_g, k_norm_g, conv_w, conv_b, w_br_na,
           w_br_gqa, w_br_conv, w_out, pre_mix_g, post_mix_g, pre_ffn_g, post_ffn_g,
           w_ffn_gate, w_ffn_up, w_ffn_down):
    ch = conv_w.shape[1]
    proj = _norm_matmul(x, pre_mix_g, w_in.astype(BF16))

    q_gain = q_norm_g.astype(F32) * (HEAD_DIM ** -0.5 * LOG2E)
    gains = jnp.concatenate([jnp.tile(q_gain, GQA_Q_HEADS), jnp.tile(k_norm_g.astype(F32), GQA_KV_HEADS)])
    gains = gains.reshape(-1, 1, 2 * HEAD_DIM)
    qk = _qk_prep(proj, gains, cos, sin, seq)
    a_gqa = _gqa_attention(qk, proj, batch, seq)

    a_na = _na_attention(proj, na_rpb, batch, seq)
    a_conv = _short_conv(proj, conv_w, conv_b, seq)

    x = _merge(x, a_na, a_gqa, a_conv, proj, w_br_na.astype(BF16), w_br_gqa.astype(BF16),
               w_br_conv.astype(BF16), w_out.astype(BF16), post_mix_g, OFF_CONV + 3 * ch)
    return _ffn(x, pre_ffn_g, w_ffn_gate.astype(BF16), w_ffn_up.astype(BF16),
                w_ffn_down.astype(BF16), post_ffn_g)


def kernel(x, w_in, na_rpb, q_norm_g, k_norm_g, conv_w, conv_b, w_br_na, w_br_gqa, w_br_conv, w_out,
           pre_mix_g, post_mix_g, pre_ffn_g, post_ffn_g, w_ffn_gate, w_ffn_up, w_ffn_down):
    batch, seq, d = x.shape
    cos, sin = _rope_tables(seq)
    y = x.reshape(batch * seq, d)
    for l in range(w_in.shape[0]):
        y = _layer(y, batch, seq, cos, sin, w_in[l], na_rpb[l], q_norm_g[l], k_norm_g[l], conv_w[l],
                   conv_b[l], w_br_na[l], w_br_gqa[l], w_br_conv[l], w_out[l], pre_mix_g[l],
                   post_mix_g[l], pre_ffn_g[l], post_ffn_g[l], w_ffn_gate[l], w_ffn_up[l],
                   w_ffn_down[l])
    return y.reshape(batch, seq, d)
```

```python
import functools
import math

import numpy as np
import jax
import jax.numpy as jnp
from jax import lax
from jax.experimental import pallas as pl
from jax.experimental.pallas import tpu as pltpu

GRID_W = 64
HEAD_DIM = 128
NA_HEADS = 8
NA_WIN_ROWS = 8
NA_WIN_COLS = 16
GQA_Q_HEADS = 8
GQA_KV_HEADS = 2
GQA_GROUP = GQA_Q_HEADS // GQA_KV_HEADS
ROPE_THETA = 10000.0
CONV_WIDTH = 3
RMS_EPS = 1e-6

NA_WIDTH = NA_HEADS * HEAD_DIM
GQA_Q_WIDTH = GQA_Q_HEADS * HEAD_DIM
GQA_KV_WIDTH = GQA_KV_HEADS * HEAD_DIM
OFF_NA_Q = 0
OFF_NA_K = NA_WIDTH
OFF_NA_V = 2 * NA_WIDTH
OFF_GQA_Q = 3 * NA_WIDTH
OFF_GQA_K = OFF_GQA_Q + GQA_Q_WIDTH
OFF_GQA_V = OFF_GQA_K + GQA_KV_WIDTH
OFF_CONV = OFF_GQA_V + GQA_KV_WIDTH

NA_GROUP_ROWS = 4
NA_BAND_ROWS = NA_GROUP_ROWS + NA_WIN_ROWS
NA_GROUPS_PER_STEP = 4
IN_PROJ_COLS = 1536
MASK_VALUE = -1e30
LOG2E = math.log2(math.e)

V7X_VMEM_BYTES = 64 * 1024 * 1024
VMEM_LIMIT = 56 * 1024 * 1024

BF16 = jnp.bfloat16
F32 = jnp.float32


def _params(semantics):
    return pltpu.CompilerParams(dimension_semantics=semantics, vmem_limit_bytes=VMEM_LIMIT)


def _pick(n, candidates):
    for c in candidates:
        if n % c == 0:
            return c
    raise ValueError(f"no tile in {candidates} divides {n}")


def _rms_rows(x, g):
    ms = jnp.mean(x * x, axis=-1, keepdims=True)
    return x * lax.rsqrt(ms + RMS_EPS) * g


def _norm_kernel(x_ref, g_ref, o_ref):
    o_ref[...] = _rms_rows(x_ref[...], g_ref[...]).astype(o_ref.dtype)


def _pre_norm(x, gains, layer):
    t, d = x.shape
    tm = _pick(t, (512, 256))
    return pl.pallas_call(
        _norm_kernel,
        out_shape=jax.ShapeDtypeStruct((t, d), BF16),
        grid=(t // tm,),
        in_specs=[
            pl.BlockSpec((tm, d), lambda i: (i, 0)),
            pl.BlockSpec((None, 1, d), lambda i: (layer, 0, 0)),
        ],
        out_specs=pl.BlockSpec((tm, d), lambda i: (i, 0)),
        compiler_params=_params(("parallel",)),
        name="pre_norm",
    )(x, gains)


def _proj_kernel(h_ref, w_ref, o_ref, wb_ref, *, row_chunk):
    @pl.when(pl.program_id(1) == 0)
    def _():
        def body(c, carry):
            rows = pl.ds(pl.multiple_of(c * row_chunk, row_chunk), row_chunk)
            wb_ref[rows, :] = w_ref[rows, :].astype(BF16)
            return carry
        lax.fori_loop(0, w_ref.shape[0] // row_chunk, body, 0)

    o_ref[...] = jnp.dot(h_ref[...], wb_ref[...], preferred_element_type=F32).astype(o_ref.dtype)


def _in_proj(h, w_in, layer, col_off, n_cols, name):
    t, d = h.shape
    tm = _pick(t, (1024, 512, 256))
    tn = IN_PROJ_COLS
    assert col_off % tn == 0 and n_cols % tn == 0
    col0 = col_off // tn
    return pl.pallas_call(
        functools.partial(_proj_kernel, row_chunk=min(d, 256)),
        out_shape=jax.ShapeDtypeStruct((t, n_cols), BF16),
        grid=(n_cols // tn, t // tm),
        in_specs=[
            pl.BlockSpec((tm, d), lambda j, i: (i, 0)),
            pl.BlockSpec((None, d, tn), lambda j, i: (layer, 0, col0 + j)),
        ],
        out_specs=pl.BlockSpec((tm, tn), lambda j, i: (i, j)),
        scratch_shapes=[pltpu.VMEM((d, tn), BF16)],
        compiler_params=_params(("parallel", "arbitrary")),
        name=name,
    )(h, w_in)


def _rope_tables(seq):
    half = HEAD_DIM // 2
    quarter = half // 2
    inv_freq = 1.0 / (ROPE_THETA ** (np.arange(quarter, dtype=np.float64) / quarter))
    t = np.arange(seq)
    ang_r = (t // GRID_W)[:, None] * inv_freq[None, :]
    ang_c = (t % GRID_W)[:, None] * inv_freq[None, :]
    cos = np.concatenate([np.cos(ang_r), np.cos(ang_r), np.cos(ang_c), np.cos(ang_c)], axis=1)
    sin = np.concatenate([-np.sin(ang_r), np.sin(ang_r), -np.sin(ang_c), np.sin(ang_c)], axis=1)
    return jnp.asarray(cos, F32), jnp.asarray(sin, F32)


def _qk_prep_kernel(x_ref, g_ref, cos_ref, sin_ref, o_ref):
    heads = x_ref.shape[1] // HEAD_DIM
    quarter = HEAD_DIM // 4
    cos = cos_ref[...]
    sin = sin_ref[...]
    lane = lax.broadcasted_iota(jnp.int32, cos.shape, 1)
    first = (lane % (2 * quarter)) < quarter
    for h in range(heads):
        cols = slice(h * HEAD_DIM, (h + 1) * HEAD_DIM)
        y = _rms_rows(x_ref[:, cols].astype(F32), g_ref[0, :, cols])
        partner = jnp.where(first, pltpu.roll(y, HEAD_DIM - quarter, 1), pltpu.roll(y, quarter, 1))
        o_ref[:, cols] = (y * cos + partner * sin).astype(o_ref.dtype)


def _qk_prep(proj, gains, cos, sin, seq):
    t = proj.shape[0]
    n_heads = GQA_Q_HEADS + GQA_KV_HEADS
    width = 2 * HEAD_DIM
    assert OFF_GQA_Q % width == 0 and n_heads % 2 == 0
    tm = _pick(seq, (2048, 1024, 512, 256))
    blocks_per_seq = seq // tm
    col0 = OFF_GQA_Q // width
    return pl.pallas_call(
        _qk_prep_kernel,
        out_shape=jax.ShapeDtypeStruct((t, n_heads * HEAD_DIM), BF16),
        grid=(t // tm, n_heads // 2),
        in_specs=[
            pl.BlockSpec((tm, width), lambda i, j: (i, col0 + j)),
            pl.BlockSpec((1, 1, width), lambda i, j: (j, 0, 0)),
            pl.BlockSpec((tm, HEAD_DIM), lambda i, j: (i % blocks_per_seq, 0)),
            pl.BlockSpec((tm, HEAD_DIM), lambda i, j: (i % blocks_per_seq, 0)),
        ],
        out_specs=pl.BlockSpec((tm, width), lambda i, j: (i, j)),
        compiler_params=_params(("parallel", "parallel")),
        name="gqa_qk_prep",
    )(proj, gains, cos, sin)


def _gqa_kernel(q_ref, k_ref, v_ref, o_ref, qs_ref, vaug_ref, acc_ref, s_ref, m_ref, *, tk):
    tq = q_ref.shape[0]
    seq = k_ref.shape[0]
    n_chunks = seq // tk

    @pl.when(pl.program_id(2) == 0)
    def _():
        vaug_ref[:, :HEAD_DIM] = v_ref[...]
        vaug_ref[:, HEAD_DIM:] = jnp.ones((seq, HEAD_DIM), BF16)

    for h in range(GQA_GROUP):
        qs_ref[h * tq:(h + 1) * tq, :] = q_ref[:, h * HEAD_DIM:(h + 1) * HEAD_DIM]
    acc_ref[...] = jnp.zeros_like(acc_ref)
    m_ref[...] = jnp.full_like(m_ref, -jnp.inf)

    def chunk(c):
        return pl.ds(pl.multiple_of(c * tk, tk), tk)

    def scores(c):
        return lax.dot_general(qs_ref[...], k_ref[chunk(c), :], (((1,), (1,)), ((), ())),
                               preferred_element_type=F32)

    def accumulate(c, slot):
        blocks = [s_ref[slot, :, j * HEAD_DIM:(j + 1) * HEAD_DIM] for j in range(tk // HEAD_DIM)]
        m_old = m_ref[...]
        m_new = jnp.maximum(m_old, jnp.max(functools.reduce(jnp.maximum, blocks), axis=-1, keepdims=True))
        m_ref[...] = m_new
        p = jnp.concatenate([jnp.exp2(b - m_new).astype(BF16) for b in blocks], axis=-1)
        pv = jnp.dot(p, vaug_ref[chunk(c), :], preferred_element_type=F32)
        alpha = jnp.exp2(m_old - m_new)
        for j in range(2):
            cols = slice(j * HEAD_DIM, (j + 1) * HEAD_DIM)
            acc_ref[:, cols] = acc_ref[:, cols] * alpha + pv[:, cols]

    assert n_chunks % 2 == 0
    s_ref[0] = scores(0)

    def body(t, carry):
        s_ref[1] = scores(2 * t + 1)
        accumulate(2 * t, 0)
        s_ref[0] = scores(2 * t + 2)
        accumulate(2 * t + 1, 1)
        return carry

    lax.fori_loop(0, n_chunks // 2 - 1, body, 0)
    s_ref[1] = scores(n_chunks - 1)
    accumulate(n_chunks - 2, 0)
    accumulate(n_chunks - 1, 1)

    for h in range(GQA_GROUP):
        a = acc_ref[h * tq:(h + 1) * tq, :]
        o_ref[:, h * HEAD_DIM:(h + 1) * HEAD_DIM] = (a[:, :HEAD_DIM] / a[:, HEAD_DIM:]).astype(o_ref.dtype)


def _gqa_attention(qk, proj, batch, seq):
    t = qk.shape[0]
    tq = _pick(seq, (256, 128))
    tk = _pick(seq, (512, 256, 128))
    gw = GQA_GROUP * HEAD_DIM
    k_col0 = GQA_Q_WIDTH // HEAD_DIM
    v_col0 = OFF_GQA_V // HEAD_DIM
    q_blocks = seq // tq
    return pl.pallas_call(
        functools.partial(_gqa_kernel, tk=tk),
        out_shape=jax.ShapeDtypeStruct((t, GQA_Q_WIDTH), BF16),
        grid=(batch, GQA_KV_HEADS, q_blocks),
        in_specs=[
            pl.BlockSpec((tq, gw), lambda b, g, i: (b * q_blocks + i, g)),
            pl.BlockSpec((seq, HEAD_DIM), lambda b, g, i: (b, k_col0 + g)),
            pl.BlockSpec((seq, HEAD_DIM), lambda b, g, i: (b, v_col0 + g)),
        ],
        out_specs=pl.BlockSpec((tq, gw), lambda b, g, i: (b * q_blocks + i, g)),
        scratch_shapes=[
            pltpu.VMEM((GQA_GROUP * tq, HEAD_DIM), BF16),
            pltpu.VMEM((seq, 2 * HEAD_DIM), BF16),
            pltpu.VMEM((GQA_GROUP * tq, 2 * HEAD_DIM), F32),
            pltpu.VMEM((2, GQA_GROUP * tq, tk), F32),
            pltpu.VMEM((GQA_GROUP * tq, HEAD_DIM), F32),
        ],
        compiler_params=_params(("parallel", "parallel", "arbitrary")),
        name="gqa_flash",
    )(qk, qk, proj)


NA_PAIR_ROWS = 2 * NA_WIN_ROWS
NA_MASK_BOTH, NA_MASK_LEFT, NA_MASK_RIGHT = 0, 1, 2


def _na_block_plan(rows):
    gr, br = NA_GROUP_ROWS, NA_BAND_ROWS
    plan = []
    for r0 in (0, min(gr, rows - gr), rows - gr):
        band_start = int(np.clip(r0 - NA_WIN_ROWS // 2, 0, rows - br))
        per_row = []
        for i in range(gr):
            r = r0 + i
            row_start = int(np.clip(r - NA_WIN_ROWS // 2, 0, rows - NA_WIN_ROWS))
            blocks = []
            for jp in range(br // 2):
                key_row = band_start + 2 * jp
                ok_l = row_start <= key_row < row_start + NA_WIN_ROWS
                ok_r = row_start <= key_row + 1 < row_start + NA_WIN_ROWS
                if not (ok_l or ok_r):
                    blocks.append(None)
                    continue
                kind = NA_MASK_BOTH if (ok_l and ok_r) else (NA_MASK_LEFT if ok_l else NA_MASK_RIGHT)
                blocks.append((key_row - r + NA_WIN_ROWS, kind))
            per_row.append(blocks)
        plan.append(per_row)
    return plan


def _na_pair_table(rpb):
    h, nr, nc = rpb.shape
    rpb = jnp.pad(rpb.astype(F32), ((0, 0), (0, 0), (0, GRID_W - nc)))
    zero = jnp.zeros((h, 1, GRID_W), F32)
    left = jnp.concatenate([zero, rpb], axis=1)
    right = jnp.concatenate([rpb, zero], axis=1)
    return jnp.concatenate([left, right], axis=2)


def _na_col_masks():
    w = np.arange(GRID_W)[:, None]
    c = np.arange(2 * GRID_W)[None, :] % GRID_W
    col_start = np.clip(w - NA_WIN_COLS // 2, 0, GRID_W - NA_WIN_COLS)
    base = np.where((c >= col_start) & (c < col_start + NA_WIN_COLS), 0.0, MASK_VALUE)
    left_half = np.arange(2 * GRID_W)[None, :] < GRID_W
    masks = np.stack([base, np.where(left_half, base, MASK_VALUE), np.where(left_half, MASK_VALUE, base)])
    return jnp.asarray(masks, F32)


def _na_kernel(q_ref, k_ref, v_ref, pair_ref, mask_ref, o_ref, tile_ref, table_ref, *, rows):
    g = pl.program_id(2)
    n_groups = rows // NA_GROUP_ROWS
    blk_w = 2 * GRID_W

    @pl.when(g == 0)
    def _():
        for p in range(NA_PAIR_ROWS):
            row = jnp.broadcast_to(pair_ref[p:p + 1, :], (GRID_W, blk_w))
            tile_ref[p] = pltpu.roll(row, blk_w - (NA_WIN_COLS - 1), 1, stride=1, stride_axis=0)
        for v, per_row in enumerate(_na_block_plan(rows)):
            for i, blocks in enumerate(per_row):
                for jp, blk in enumerate(blocks):
                    dst = (v, slice(i * GRID_W, (i + 1) * GRID_W), slice(jp * blk_w, (jp + 1) * blk_w))
                    if blk is None:
                        table_ref[dst] = jnp.full((GRID_W, blk_w), MASK_VALUE, F32)
                    else:
                        table_ref[dst] = (tile_ref[blk[0]] + mask_ref[blk[1]]) * LOG2E

    gq = NA_GROUP_ROWS * GRID_W
    band_rows = NA_BAND_ROWS * GRID_W
    for u in range(q_ref.shape[0] // gq):
        gi = g * (q_ref.shape[0] // gq) + u
        start_row = jnp.clip(gi * NA_GROUP_ROWS - NA_WIN_ROWS // 2, 0, rows - NA_BAND_ROWS)
        band = pl.ds(pl.multiple_of(start_row * GRID_W, GRID_W), band_rows)
        variant = jnp.where(gi == 0, 0, jnp.where(gi == n_groups - 1, 2, 1))
        s = lax.dot_general(q_ref[u * gq:(u + 1) * gq, :], k_ref[band, :], (((1,), (1,)), ((), ())),
                            preferred_element_type=F32)
        s = s * (HEAD_DIM ** -0.5 * LOG2E) + table_ref[variant]
        p = jnp.exp2(s - jnp.max(s, axis=-1, keepdims=True))
        l = jnp.sum(p, axis=-1, keepdims=True)
        o = jnp.dot(p.astype(BF16), v_ref[band, :], preferred_element_type=F32)
        o_ref[u * gq:(u + 1) * gq, :] = (o / l).astype(o_ref.dtype)


def _na_attention(proj, rpb, batch, seq):
    t = proj.shape[0]
    rows = seq // GRID_W
    assert NA_GROUP_ROWS % 4 == 0 and rows % NA_GROUP_ROWS == 0 and rows >= NA_BAND_ROWS
    assert 2 * GRID_W == HEAD_DIM
    n_groups = rows // NA_GROUP_ROWS
    gq = NA_GROUP_ROWS * GRID_W
    gk = NA_BAND_ROWS * GRID_W
    per_step = _pick(n_groups, (NA_GROUPS_PER_STEP, 2, 1))
    n_steps = n_groups // per_step
    k_col0 = OFF_NA_K // HEAD_DIM
    v_col0 = OFF_NA_V // HEAD_DIM
    return pl.pallas_call(
        functools.partial(_na_kernel, rows=rows),
        out_shape=jax.ShapeDtypeStruct((t, NA_WIDTH), BF16),
        grid=(batch, NA_HEADS, n_steps),
        in_specs=[
            pl.BlockSpec((per_step * gq, HEAD_DIM), lambda b, h, g: (b * n_steps + g, h)),
            pl.BlockSpec((seq, HEAD_DIM), lambda b, h, g: (b, k_col0 + h)),
            pl.BlockSpec((seq, HEAD_DIM), lambda b, h, g: (b, v_col0 + h)),
            pl.BlockSpec((None, NA_PAIR_ROWS, 2 * GRID_W), lambda b, h, g: (h, 0, 0)),
            pl.BlockSpec((3, GRID_W, 2 * GRID_W), lambda b, h, g: (0, 0, 0)),
        ],
        out_specs=pl.BlockSpec((per_step * gq, HEAD_DIM), lambda b, h, g: (b * n_steps + g, h)),
        scratch_shapes=[
            pltpu.VMEM((NA_PAIR_ROWS, GRID_W, 2 * GRID_W), F32),
            pltpu.VMEM((3, gq, gk), F32),
        ],
        compiler_params=_params(("parallel", "parallel", "arbitrary")),
        name="na_attention",
    )(proj, proj, proj, _na_pair_table(rpb), _na_col_masks())


CONV_HALO = 16


def _branch_merge_kernel(a_na_ref, a_gqa_ref, h_ref, b_ref, c_ref, hp_ref, cp_ref, hn_ref, cn_ref,
                         g_na_ref, g_gqa_ref, g_conv_ref, cw_ref, cb_ref, w_na_ref, w_gqa_ref,
                         w_conv_ref, o_ref, a_conv_ref, *, blocks_per_seq, col_chunk):
    tm = h_ref.shape[0]
    pos = pl.program_id(0) % blocks_per_seq

    u = c_ref[...].astype(F32) * h_ref[...].astype(F32)
    u_prev_row = cp_ref[CONV_HALO - 1:CONV_HALO, :].astype(F32) * hp_ref[CONV_HALO - 1:CONV_HALO, :].astype(F32)
    u_prev_row = jnp.where(pos == 0, 0.0, u_prev_row)
    u_next_row = cn_ref[0:1, :].astype(F32) * hn_ref[0:1, :].astype(F32)
    u_next_row = jnp.where(pos == blocks_per_seq - 1, 0.0, u_next_row)
    row = lax.broadcasted_iota(jnp.int32, u.shape, 0)
    u_prev = jnp.where(row == 0, u_prev_row, pltpu.roll(u, 1, 0))
    u_next = jnp.where(row == tm - 1, u_next_row, pltpu.roll(u, tm - 1, 0))
    y = cb_ref[...] + u_prev * cw_ref[0:1, :] + u * cw_ref[1:2, :] + u_next * cw_ref[2:3, :]
    a_conv_ref[...] = (b_ref[...].astype(F32) * y).astype(BF16)

    for c in range(o_ref.shape[1] // col_chunk):
        cols = slice(c * col_chunk, (c + 1) * col_chunk)

        def branch(a_ref, w_ref, gate_ref):
            y = jnp.dot(a_ref[...], w_ref[:, cols], preferred_element_type=F32)
            return jax.nn.sigmoid(gate_ref[:, cols].astype(F32)) * y

        merged = (branch(a_na_ref, w_na_ref, g_na_ref) + branch(a_gqa_ref, w_gqa_ref, g_gqa_ref)
                  + branch(a_conv_ref, w_conv_ref, g_conv_ref))
        o_ref[:, cols] = merged.astype(o_ref.dtype)


def _branch_merge(a_na, a_gqa, p_conv, p_gate, conv_w, conv_b, w_na, w_gqa, w_conv, layer, seq):
    t = a_na.shape[0]
    ch = conv_w.shape[2]
    d = w_na.shape[2]
    assert CONV_WIDTH == 3
    tm = _pick(seq, (512, 256))
    per_halo = tm // CONV_HALO
    last_halo = t // CONV_HALO - 1
    prev_map = lambda col: (lambda i: (jnp.maximum(i * per_halo - 1, 0), col))
    next_map = lambda col: (lambda i: (jnp.minimum((i + 1) * per_halo, last_halo), col))
    resident = lambda shape: pl.BlockSpec((None,) + shape, lambda i: (layer, 0, 0),
                                          pipeline_mode=pl.Buffered(1))
    return pl.pallas_call(
        functools.partial(_branch_merge_kernel, blocks_per_seq=seq // tm,
                          col_chunk=_pick(d, (512, 256, 128))),
        out_shape=jax.ShapeDtypeStruct((t, d), BF16),
        grid=(t // tm,),
        in_specs=[
            pl.BlockSpec((tm, NA_WIDTH), lambda i: (i, 0)),
            pl.BlockSpec((tm, GQA_Q_WIDTH), lambda i: (i, 0)),
            pl.BlockSpec((tm, ch), lambda i: (i, 0)),
            pl.BlockSpec((tm, ch), lambda i: (i, 1)),
            pl.BlockSpec((tm, ch), lambda i: (i, 2)),
            pl.BlockSpec((CONV_HALO, ch), prev_map(0)),
            pl.BlockSpec((CONV_HALO, ch), prev_map(2)),
            pl.BlockSpec((CONV_HALO, ch), next_map(0)),
            pl.BlockSpec((CONV_HALO, ch), next_map(2)),
            pl.BlockSpec((tm, d), lambda i: (i, 0)),
            pl.BlockSpec((tm, d), lambda i: (i, 1)),
            pl.BlockSpec((tm, d), lambda i: (i, 2)),
            pl.BlockSpec((None, CONV_WIDTH, ch), lambda i: (layer, 0, 0)),
            pl.BlockSpec((None, 1, ch), lambda i: (layer, 0, 0)),
            resident((NA_WIDTH, d)),
            resident((GQA_Q_WIDTH, d)),
            resident((ch, d)),
        ],
        out_specs=pl.BlockSpec((tm, d), lambda i: (i, 0)),
        scratch_shapes=[pltpu.VMEM((tm, ch), BF16)],
        compiler_params=_params(("parallel",)),
        name="branch_merge",
    )(a_na, a_gqa, p_conv, p_conv, p_conv, p_conv, p_conv, p_conv, p_conv, p_gate, p_gate, p_gate,
      conv_w, conv_b, w_na, w_gqa, w_conv)


def _out_proj_kernel(x_ref, m_ref, w_ref, g_post_ref, g_next_ref, o_ref, h_ref):
    half = x_ref.shape[0] // 2
    for r in range(2):
        rows = slice(r * half, (r + 1) * half)
        y = jnp.dot(m_ref[rows, :], w_ref[...], preferred_element_type=F32)
        x_new = x_ref[rows, :] + _rms_rows(y, g_post_ref[...])
        o_ref[rows, :] = x_new
        h_ref[rows, :] = _rms_rows(x_new, g_next_ref[...]).astype(h_ref.dtype)


def _out_proj(x, merged, w_out, post_gains, next_gains, layer):
    t, d = x.shape
    tm = _pick(t, (512, 256))
    gain = lambda: pl.BlockSpec((None, 1, d), lambda i: (layer, 0, 0))
    return pl.pallas_call(
        _out_proj_kernel,
        out_shape=(jax.ShapeDtypeStruct((t, d), F32), jax.ShapeDtypeStruct((t, d), BF16)),
        grid=(t // tm,),
        in_specs=[
            pl.BlockSpec((tm, d), lambda i: (i, 0)),
            pl.BlockSpec((tm, d), lambda i: (i, 0)),
            pl.BlockSpec((None, d, d), lambda i: (layer, 0, 0), pipeline_mode=pl.Buffered(1)),
            gain(),
            gain(),
        ],
        out_specs=(pl.BlockSpec((tm, d), lambda i: (i, 0)), pl.BlockSpec((tm, d), lambda i: (i, 0))),
        compiler_params=_params(("parallel",)),
        name="out_proj_norm",
    )(x, merged, w_out, post_gains, next_gains)


def _ffn_kernel(x_ref, h_ref, wg_ref, wu_ref, wd_ref, g_post_ref, g_next_ref, o_ref, hn_ref, acc_ref):
    f = pl.program_id(1)

    @pl.when(f == 0)
    def _():
        acc_ref[...] = jnp.zeros_like(acc_ref)

    h = h_ref[...]
    a = jnp.dot(h, wg_ref[...], preferred_element_type=F32)
    b = jnp.dot(h, wu_ref[...], preferred_element_type=F32)
    act = (a * jax.nn.sigmoid(a) * b).astype(BF16)
    acc_ref[...] += jnp.dot(act, wd_ref[...], preferred_element_type=F32)

    @pl.when(f == pl.num_programs(1) - 1)
    def _():
        x_new = x_ref[...] + _rms_rows(acc_ref[...], g_post_ref[...])
        o_ref[...] = x_new
        hn_ref[...] = _rms_rows(x_new, g_next_ref[...]).astype(hn_ref.dtype)


def _ffn(x, h, wg, wu, wd, post_gains, next_gains, layer, next_layer):
    t, d = x.shape
    hidden = wg.shape[2]
    tm = _pick(t, (512, 256))
    tf = _pick(hidden, (512, 256, 128))
    return pl.pallas_call(
        _ffn_kernel,
        out_shape=(jax.ShapeDtypeStruct((t, d), F32), jax.ShapeDtypeStruct((t, d), BF16)),
        grid=(t // tm, hidden // tf),
        in_specs=[
            pl.BlockSpec((tm, d), lambda i, f: (i, 0)),
            pl.BlockSpec((tm, d), lambda i, f: (i, 0)),
            pl.BlockSpec((None, d, tf), lambda i, f: (layer, 0, f)),
            pl.BlockSpec((None, d, tf), lambda i, f: (layer, 0, f)),
            pl.BlockSpec((None, tf, d), lambda i, f: (layer, f, 0)),
            pl.BlockSpec((None, 1, d), lambda i, f: (layer, 0, 0)),
            pl.BlockSpec((None, 1, d), lambda i, f: (next_layer, 0, 0)),
        ],
        out_specs=(pl.BlockSpec((tm, d), lambda i, f: (i, 0)), pl.BlockSpec((tm, d), lambda i, f: (i, 0))),
        scratch_shapes=[pltpu.VMEM((tm, d), F32)],
        compiler_params=_params(("parallel", "arbitrary")),
        name="ffn_swiglu",
    )(x, h, wg, wu, wd, post_gains, next_gains)


def kernel(x, w_in, na_rpb, q_norm_g, k_norm_g, conv_w, conv_b, w_br_na, w_br_gqa, w_br_conv, w_out,
           pre_mix_g, post_mix_g, pre_ffn_g, post_ffn_g, w_ffn_gate, w_ffn_up, w_ffn_down):
    batch, seq, d = x.shape
    depth = w_in.shape[0]
    ch = conv_w.shape[2]
    cos, sin = _rope_tables(seq)

    gain3 = lambda g: g.astype(F32).reshape(depth, 1, -1)
    pre_mix_g, post_mix_g, pre_ffn_g, post_ffn_g = map(gain3, (pre_mix_g, post_mix_g, pre_ffn_g, post_ffn_g))
    conv_w = conv_w.astype(F32)
    conv_b = conv_b.astype(F32).reshape(depth, 1, ch)
    w_br_na, w_br_gqa, w_br_conv, w_out = (w.astype(BF16) for w in (w_br_na, w_br_gqa, w_br_conv, w_out))
    w_ffn_gate, w_ffn_up, w_ffn_down = (w.astype(BF16) for w in (w_ffn_gate, w_ffn_up, w_ffn_down))
    q_gain = q_norm_g.astype(F32) * (HEAD_DIM ** -0.5 * LOG2E)
    qk_gains = jnp.concatenate([jnp.tile(q_gain, (1, GQA_Q_HEADS)),
                                jnp.tile(k_norm_g.astype(F32), (1, GQA_KV_HEADS))], axis=1)
    qk_gains = qk_gains.reshape(depth, -1, 1, 2 * HEAD_DIM)

    y = x.reshape(batch * seq, d)
    h = _pre_norm(y, pre_mix_g, 0)
    for l in range(depth):
        p_attn = _in_proj(h, w_in, l, 0, OFF_CONV, "in_proj_attn")
        p_conv = _in_proj(h, w_in, l, OFF_CONV, 3 * ch, "in_proj_conv")
        p_gate = _in_proj(h, w_in, l, OFF_CONV + 3 * ch, 3 * d, "in_proj_gate")

        qk = _qk_prep(p_attn, qk_gains[l], cos, sin, seq)
        a_gqa = _gqa_attention(qk, p_attn, batch, seq)
        a_na = _na_attention(p_attn, na_rpb[l], batch, seq)

        merged = _branch_merge(a_na, a_gqa, p_conv, p_gate, conv_w, conv_b, w_br_na, w_br_gqa,
                               w_br_conv, l, seq)
        y, h = _out_proj(y, merged, w_out, post_mix_g, pre_ffn_g, l)
        y, h = _ffn(y, h, w_ffn_gate, w_ffn_up, w_ffn_down, post_ffn_g, pre_mix_g, l, (l + 1) % depth)
    return y.reshape(batch, seq, d)
```

```python
import functools
import math

import numpy as np
import jax
import jax.numpy as jnp
from jax import lax
from jax.experimental import pallas as pl
from jax.experimental.pallas import tpu as pltpu

GRID_W = 64
HEAD_DIM = 128
NA_HEADS = 8
NA_WIN_ROWS = 8
NA_WIN_COLS = 16
GQA_Q_HEADS = 8
GQA_KV_HEADS = 2
GQA_GROUP = GQA_Q_HEADS // GQA_KV_HEADS
ROPE_THETA = 10000.0
CONV_WIDTH = 3
RMS_EPS = 1e-6

NA_WIDTH = NA_HEADS * HEAD_DIM
GQA_Q_WIDTH = GQA_Q_HEADS * HEAD_DIM
GQA_KV_WIDTH = GQA_KV_HEADS * HEAD_DIM
OFF_NA_Q = 0
OFF_NA_K = NA_WIDTH
OFF_NA_V = 2 * NA_WIDTH
OFF_GQA_Q = 3 * NA_WIDTH
OFF_GQA_K = OFF_GQA_Q + GQA_Q_WIDTH
OFF_GQA_V = OFF_GQA_K + GQA_KV_WIDTH
OFF_CONV = OFF_GQA_V + GQA_KV_WIDTH

NA_GROUP_ROWS = 4
NA_BAND_ROWS = NA_GROUP_ROWS + NA_WIN_ROWS
NA_GROUPS_PER_STEP = 4
IN_PROJ_COLS = 1536
MASK_VALUE = -1e30
LOG2E = math.log2(math.e)

V7X_VMEM_BYTES = 64 * 1024 * 1024
VMEM_LIMIT = 56 * 1024 * 1024

BF16 = jnp.bfloat16
F32 = jnp.float32


def _params(semantics):
    return pltpu.CompilerParams(dimension_semantics=semantics, vmem_limit_bytes=VMEM_LIMIT)


def _pick(n, candidates):
    for c in candidates:
        if n % c == 0:
            return c
    raise ValueError(f"no tile in {candidates} divides {n}")


def _rms_rows(x, g):
    ms = jnp.mean(x * x, axis=-1, keepdims=True)
    return x * lax.rsqrt(ms + RMS_EPS) * g


def _norm_kernel(x_ref, g_ref, o_ref):
    o_ref[...] = _rms_rows(x_ref[...], g_ref[...]).astype(o_ref.dtype)


def _pre_norm(x, gains, layer):
    t, d = x.shape
    tm = _pick(t, (512, 256))
    return pl.pallas_call(
        _norm_kernel,
        out_shape=jax.ShapeDtypeStruct((t, d), BF16),
        grid=(t // tm,),
        in_specs=[
            pl.BlockSpec((tm, d), lambda i: (i, 0)),
            pl.BlockSpec((None, 1, d), lambda i: (layer, 0, 0)),
        ],
        out_specs=pl.BlockSpec((tm, d), lambda i: (i, 0)),
        compiler_params=_params(("parallel",)),
        name="pre_norm",
    )(x, gains)


def _proj_kernel(h_ref, w_ref, o_ref, wb_ref, *, row_chunk):
    @pl.when(pl.program_id(1) == 0)
    def _():
        def body(c, carry):
            rows = pl.ds(pl.multiple_of(c * row_chunk, row_chunk), row_chunk)
            wb_ref[rows, :] = w_ref[rows, :].astype(BF16)
            return carry
        lax.fori_loop(0, w_ref.shape[0] // row_chunk, body, 0)

    o_ref[...] = jnp.dot(h_ref[...], wb_ref[...], preferred_element_type=F32).astype(o_ref.dtype)


def _in_proj(h, w_in, layer, col_off, n_cols, name):
    t, d = h.shape
    tm = _pick(t, (1024, 512, 256))
    tn = IN_PROJ_COLS
    assert col_off % tn == 0 and n_cols % tn == 0
    col0 = col_off // tn
    return pl.pallas_call(
        functools.partial(_proj_kernel, row_chunk=min(d, 256)),
        out_shape=jax.ShapeDtypeStruct((t, n_cols), BF16),
        grid=(n_cols // tn, t // tm),
        in_specs=[
            pl.BlockSpec((tm, d), lambda j, i: (i, 0)),
            pl.BlockSpec((None, d, tn), lambda j, i: (layer, 0, col0 + j)),
        ],
        out_specs=pl.BlockSpec((tm, tn), lambda j, i: (i, j)),
        scratch_shapes=[pltpu.VMEM((d, tn), BF16)],
        compiler_params=_params(("parallel", "arbitrary")),
        name=name,
    )(h, w_in)


def _rope_tables(seq):
    half = HEAD_DIM // 2
    quarter = half // 2
    inv_freq = 1.0 / (ROPE_THETA ** (np.arange(quarter, dtype=np.float64) / quarter))
    t = np.arange(seq)
    ang_r = (t // GRID_W)[:, None] * inv_freq[None, :]
    ang_c = (t % GRID_W)[:, None] * inv_freq[None, :]
    cos = np.concatenate([np.cos(ang_r), np.cos(ang_r), np.cos(ang_c), np.cos(ang_c)], axis=1)
    sin = np.concatenate([-np.sin(ang_r), np.sin(ang_r), -np.sin(ang_c), np.sin(ang_c)], axis=1)
    return jnp.asarray(cos, F32), jnp.asarray(sin, F32)


def _qk_prep_kernel(x_ref, g_ref, cos_ref, sin_ref, o_ref):
    heads = x_ref.shape[1] // HEAD_DIM
    quarter = HEAD_DIM // 4
    cos = cos_ref[...]
    sin = sin_ref[...]
    lane = lax.broadcasted_iota(jnp.int32, cos.shape, 1)
    first = (lane % (2 * quarter)) < quarter
    for h in range(heads):
        cols = slice(h * HEAD_DIM, (h + 1) * HEAD_DIM)
        y = _rms_rows(x_ref[:, cols].astype(F32), g_ref[0, :, cols])
        partner = jnp.where(first, pltpu.roll(y, HEAD_DIM - quarter, 1), pltpu.roll(y, quarter, 1))
        o_ref[:, cols] = (y * cos + partner * sin).astype(o_ref.dtype)


def _qk_prep(proj, gains, cos, sin, seq):
    t = proj.shape[0]
    n_heads = GQA_Q_HEADS + GQA_KV_HEADS
    width = 2 * HEAD_DIM
    assert OFF_GQA_Q % width == 0 and n_heads % 2 == 0
    tm = _pick(seq, (2048, 1024, 512, 256))
    blocks_per_seq = seq // tm
    col0 = OFF_GQA_Q // width
    return pl.pallas_call(
        _qk_prep_kernel,
        out_shape=jax.ShapeDtypeStruct((t, n_heads * HEAD_DIM), BF16),
        grid=(t // tm, n_heads // 2),
        in_specs=[
            pl.BlockSpec((tm, width), lambda i, j: (i, col0 + j)),
            pl.BlockSpec((1, 1, width), lambda i, j: (j, 0, 0)),
            pl.BlockSpec((tm, HEAD_DIM), lambda i, j: (i % blocks_per_seq, 0)),
            pl.BlockSpec((tm, HEAD_DIM), lambda i, j: (i % blocks_per_seq, 0)),
        ],
        out_specs=pl.BlockSpec((tm, width), lambda i, j: (i, j)),
        compiler_params=_params(("parallel", "parallel")),
        name="gqa_qk_prep",
    )(proj, gains, cos, sin)


GQA_ONES_ROWS = 16


def _gqa_kernel(q_ref, k_ref, v_ref, o_ref, qs_ref, vt_ref, acc_ref, s_ref, m_ref, *, tk):
    tq = q_ref.shape[0]
    seq = k_ref.shape[0]
    n_chunks = seq // tk

    def chunk(c):
        return pl.ds(pl.multiple_of(c * tk, tk), tk)

    @pl.when(pl.program_id(2) == 0)
    def _():
        def body(c, carry):
            vt_ref[c, :HEAD_DIM, :] = v_ref[chunk(c), :].astype(F32).T.astype(BF16)
            vt_ref[c, HEAD_DIM:, :] = jnp.ones((GQA_ONES_ROWS, tk), BF16)
            return carry
        lax.fori_loop(0, n_chunks, body, 0)

    for h in range(GQA_GROUP):
        qs_ref[h * tq:(h + 1) * tq, :] = q_ref[:, h * HEAD_DIM:(h + 1) * HEAD_DIM]
    acc_ref[...] = jnp.zeros_like(acc_ref)
    m_ref[...] = jnp.full_like(m_ref, -jnp.inf)

    def scores(c):
        return lax.dot_general(k_ref[chunk(c), :], qs_ref[...], (((1,), (1,)), ((), ())),
                               preferred_element_type=F32)

    def accumulate(c, slot):
        s = s_ref[slot]
        m_old = m_ref[...]
        m_new = jnp.maximum(m_old, jnp.max(s, axis=0, keepdims=True))
        m_ref[...] = m_new
        p = jnp.exp2(s - m_new).astype(BF16)
        pv = jnp.dot(vt_ref[c], p, preferred_element_type=F32)
        acc_ref[...] = acc_ref[...] * jnp.exp2(m_old - m_new) + pv

    assert n_chunks % 2 == 0
    s_ref[0] = scores(0)

    def body(t, carry):
        s_ref[1] = scores(2 * t + 1)
        accumulate(2 * t, 0)
        s_ref[0] = scores(2 * t + 2)
        accumulate(2 * t + 1, 1)
        return carry

    lax.fori_loop(0, n_chunks // 2 - 1, body, 0)
    s_ref[1] = scores(n_chunks - 1)
    accumulate(n_chunks - 2, 0)
    accumulate(n_chunks - 1, 1)

    out_t = acc_ref[:HEAD_DIM, :] / acc_ref[HEAD_DIM:HEAD_DIM + 1, :]
    for h in range(GQA_GROUP):
        o_ref[:, h * HEAD_DIM:(h + 1) * HEAD_DIM] = out_t[:, h * tq:(h + 1) * tq].T.astype(o_ref.dtype)


def _gqa_attention(qk, proj, batch, seq):
    t = qk.shape[0]
    tq = _pick(seq, (256, 128))
    tk = _pick(seq, (512, 256, 128))
    gw = GQA_GROUP * HEAD_DIM
    k_col0 = GQA_Q_WIDTH // HEAD_DIM
    v_col0 = OFF_GQA_V // HEAD_DIM
    q_blocks = seq // tq
    return pl.pallas_call(
        functools.partial(_gqa_kernel, tk=tk),
        out_shape=jax.ShapeDtypeStruct((t, GQA_Q_WIDTH), BF16),
        grid=(batch, GQA_KV_HEADS, q_blocks),
        in_specs=[
            pl.BlockSpec((tq, gw), lambda b, g, i: (b * q_blocks + i, g)),
            pl.BlockSpec((seq, HEAD_DIM), lambda b, g, i: (b, k_col0 + g)),
            pl.BlockSpec((seq, HEAD_DIM), lambda b, g, i: (b, v_col0 + g)),
        ],
        out_specs=pl.BlockSpec((tq, gw), lambda b, g, i: (b * q_blocks + i, g)),
        scratch_shapes=[
            pltpu.VMEM((GQA_GROUP * tq, HEAD_DIM), BF16),
            pltpu.VMEM((seq // tk, HEAD_DIM + GQA_ONES_ROWS, tk), BF16),
            pltpu.VMEM((HEAD_DIM + GQA_ONES_ROWS, GQA_GROUP * tq), F32),
            pltpu.VMEM((2, tk, GQA_GROUP * tq), F32),
            pltpu.VMEM((1, GQA_GROUP * tq), F32),
        ],
        compiler_params=_params(("parallel", "parallel", "arbitrary")),
        name="gqa_flash",
    )(qk, qk, proj)


NA_PAIR_ROWS = 2 * NA_WIN_ROWS
NA_MASK_BOTH, NA_MASK_LEFT, NA_MASK_RIGHT = 0, 1, 2


def _na_block_plan(rows):
    gr, br = NA_GROUP_ROWS, NA_BAND_ROWS
    plan = []
    for r0 in (0, min(gr, rows - gr), rows - gr):
        band_start = int(np.clip(r0 - NA_WIN_ROWS // 2, 0, rows - br))
        per_row = []
        for i in range(gr):
            r = r0 + i
            row_start = int(np.clip(r - NA_WIN_ROWS // 2, 0, rows - NA_WIN_ROWS))
            blocks = []
            for jp in range(br // 2):
                key_row = band_start + 2 * jp
                ok_l = row_start <= key_row < row_start + NA_WIN_ROWS
                ok_r = row_start <= key_row + 1 < row_start + NA_WIN_ROWS
                if not (ok_l or ok_r):
                    blocks.append(None)
                    continue
                kind = NA_MASK_BOTH if (ok_l and ok_r) else (NA_MASK_LEFT if ok_l else NA_MASK_RIGHT)
                blocks.append((key_row - r + NA_WIN_ROWS, kind))
            per_row.append(blocks)
        plan.append(per_row)
    return plan


def _na_pair_table(rpb):
    h, nr, nc = rpb.shape
    rpb = jnp.pad(rpb.astype(F32), ((0, 0), (0, 0), (0, GRID_W - nc)))
    zero = jnp.zeros((h, 1, GRID_W), F32)
    left = jnp.concatenate([zero, rpb], axis=1)
    right = jnp.concatenate([rpb, zero], axis=1)
    return jnp.concatenate([left, right], axis=2)


def _na_col_masks():
    w = np.arange(GRID_W)[:, None]
    c = np.arange(2 * GRID_W)[None, :] % GRID_W
    col_start = np.clip(w - NA_WIN_COLS // 2, 0, GRID_W - NA_WIN_COLS)
    base = np.where((c >= col_start) & (c < col_start + NA_WIN_COLS), 0.0, MASK_VALUE)
    left_half = np.arange(2 * GRID_W)[None, :] < GRID_W
    masks = np.stack([base, np.where(left_half, base, MASK_VALUE), np.where(left_half, MASK_VALUE, base)])
    return jnp.asarray(masks, F32)


def _na_kernel(q_ref, k_ref, v_ref, pair_ref, mask_ref, o_ref, tile_ref, table_ref, *, rows):
    g = pl.program_id(2)
    n_groups = rows // NA_GROUP_ROWS
    blk_w = 2 * GRID_W

    @pl.when(g == 0)
    def _():
        for p in range(NA_PAIR_ROWS):
            row = jnp.broadcast_to(pair_ref[p:p + 1, :], (GRID_W, blk_w))
            tile_ref[p] = pltpu.roll(row, blk_w - (NA_WIN_COLS - 1), 1, stride=1, stride_axis=0)
        for v, per_row in enumerate(_na_block_plan(rows)):
            for i, blocks in enumerate(per_row):
                for jp, blk in enumerate(blocks):
                    dst = (v, slice(i * GRID_W, (i + 1) * GRID_W), slice(jp * blk_w, (jp + 1) * blk_w))
                    if blk is None:
                        table_ref[dst] = jnp.full((GRID_W, blk_w), MASK_VALUE, F32)
                    else:
                        table_ref[dst] = (tile_ref[blk[0]] + mask_ref[blk[1]]) * LOG2E

    gq = NA_GROUP_ROWS * GRID_W
    band_rows = NA_BAND_ROWS * GRID_W
    for u in range(q_ref.shape[0] // gq):
        gi = g * (q_ref.shape[0] // gq) + u
        start_row = jnp.clip(gi * NA_GROUP_ROWS - NA_WIN_ROWS // 2, 0, rows - NA_BAND_ROWS)
        band = pl.ds(pl.multiple_of(start_row * GRID_W, GRID_W), band_rows)
        variant = jnp.where(gi == 0, 0, jnp.where(gi == n_groups - 1, 2, 1))
        s = lax.dot_general(q_ref[u * gq:(u + 1) * gq, :], k_ref[band, :], (((1,), (1,)), ((), ())),
                            preferred_element_type=F32)
        s = s * (HEAD_DIM ** -0.5 * LOG2E) + table_ref[variant]
        p = jnp.exp2(s - jnp.max(s, axis=-1, keepdims=True))
        l = jnp.sum(p, axis=-1, keepdims=True)
        o = jnp.dot(p.astype(BF16), v_ref[band, :], preferred_element_type=F32)
        o_ref[u * gq:(u + 1) * gq, :] = (o / l).astype(o_ref.dtype)


def _na_attention(proj, rpb, batch, seq):
    t = proj.shape[0]
    rows = seq // GRID_W
    assert NA_GROUP_ROWS % 4 == 0 and rows % NA_GROUP_ROWS == 0 and rows >= NA_BAND_ROWS
    assert 2 * GRID_W == HEAD_DIM
    n_groups = rows // NA_GROUP_ROWS
    gq = NA_GROUP_ROWS * GRID_W
    gk = NA_BAND_ROWS * GRID_W
    per_step = _pick(n_groups, (NA_GROUPS_PER_STEP, 2, 1))
    n_steps = n_groups // per_step
    k_col0 = OFF_NA_K // HEAD_DIM
    v_col0 = OFF_NA_V // HEAD_DIM
    return pl.pallas_call(
        functools.partial(_na_kernel, rows=rows),
        out_shape=jax.ShapeDtypeStruct((t, NA_WIDTH), BF16),
        grid=(batch, NA_HEADS, n_steps),
        in_specs=[
            pl.BlockSpec((per_step * gq, HEAD_DIM), lambda b, h, g: (b * n_steps + g, h)),
            pl.BlockSpec((seq, HEAD_DIM), lambda b, h, g: (b, k_col0 + h)),
            pl.BlockSpec((seq, HEAD_DIM), lambda b, h, g: (b, v_col0 + h)),
            pl.BlockSpec((None, NA_PAIR_ROWS, 2 * GRID_W), lambda b, h, g: (h, 0, 0)),
            pl.BlockSpec((3, GRID_W, 2 * GRID_W), lambda b, h, g: (0, 0, 0)),
        ],
        out_specs=pl.BlockSpec((per_step * gq, HEAD_DIM), lambda b, h, g: (b * n_steps + g, h)),
        scratch_shapes=[
            pltpu.VMEM((NA_PAIR_ROWS, GRID_W, 2 * GRID_W), F32),
            pltpu.VMEM((3, gq, gk), F32),
        ],
        compiler_params=_params(("parallel", "parallel", "arbitrary")),
        name="na_attention",
    )(proj, proj, proj, _na_pair_table(rpb), _na_col_masks())


CONV_HALO = 16


def _branch_merge_kernel(a_na_ref, a_gqa_ref, h_ref, b_ref, c_ref, hp_ref, cp_ref, hn_ref, cn_ref,
                         g_na_ref, g_gqa_ref, g_conv_ref, cw_ref, cb_ref, w_na_ref, w_gqa_ref,
                         w_conv_ref, o_ref, a_conv_ref, *, blocks_per_seq, col_chunk):
    tm = h_ref.shape[0]
    pos = pl.program_id(0) % blocks_per_seq

    u = c_ref[...].astype(F32) * h_ref[...].astype(F32)
    u_prev_row = cp_ref[CONV_HALO - 1:CONV_HALO, :].astype(F32) * hp_ref[CONV_HALO - 1:CONV_HALO, :].astype(F32)
    u_prev_row = jnp.where(pos == 0, 0.0, u_prev_row)
    u_next_row = cn_ref[0:1, :].astype(F32) * hn_ref[0:1, :].astype(F32)
    u_next_row = jnp.where(pos == blocks_per_seq - 1, 0.0, u_next_row)
    row = lax.broadcasted_iota(jnp.int32, u.shape, 0)
    u_prev = jnp.where(row == 0, u_prev_row, pltpu.roll(u, 1, 0))
    u_next = jnp.where(row == tm - 1, u_next_row, pltpu.roll(u, tm - 1, 0))
    y = cb_ref[...] + u_prev * cw_ref[0:1, :] + u * cw_ref[1:2, :] + u_next * cw_ref[2:3, :]
    a_conv_ref[...] = (b_ref[...].astype(F32) * y).astype(BF16)

    for c in range(o_ref.shape[1] // col_chunk):
        cols = slice(c * col_chunk, (c + 1) * col_chunk)

        def branch(a_ref, w_ref, gate_ref):
            y = jnp.dot(a_ref[...], w_ref[:, cols], preferred_element_type=F32)
            return jax.nn.sigmoid(gate_ref[:, cols].astype(F32)) * y

        merged = (branch(a_na_ref, w_na_ref, g_na_ref) + branch(a_gqa_ref, w_gqa_ref, g_gqa_ref)
                  + branch(a_conv_ref, w_conv_ref, g_conv_ref))
        o_ref[:, cols] = merged.astype(o_ref.dtype)


def _branch_merge(a_na, a_gqa, p_conv, p_gate, conv_w, conv_b, w_na, w_gqa, w_conv, layer, seq):
    t = a_na.shape[0]
    ch = conv_w.shape[2]
    d = w_na.shape[2]
    assert CONV_WIDTH == 3
    tm = _pick(seq, (512, 256))
    per_halo = tm // CONV_HALO
    last_halo = t // CONV_HALO - 1
    prev_map = lambda col: (lambda i: (jnp.maximum(i * per_halo - 1, 0), col))
    next_map = lambda col: (lambda i: (jnp.minimum((i + 1) * per_halo, last_halo), col))
    resident = lambda shape: pl.BlockSpec((None,) + shape, lambda i: (layer, 0, 0),
                                          pipeline_mode=pl.Buffered(1))
    return pl.pallas_call(
        functools.partial(_branch_merge_kernel, blocks_per_seq=seq // tm,
                          col_chunk=_pick(d, (512, 256, 128))),
        out_shape=jax.ShapeDtypeStruct((t, d), BF16),
        grid=(t // tm,),
        in_specs=[
            pl.BlockSpec((tm, NA_WIDTH), lambda i: (i, 0)),
            pl.BlockSpec((tm, GQA_Q_WIDTH), lambda i: (i, 0)),
            pl.BlockSpec((tm, ch), lambda i: (i, 0)),
            pl.BlockSpec((tm, ch), lambda i: (i, 1)),
            pl.BlockSpec((tm, ch), lambda i: (i, 2)),
            pl.BlockSpec((CONV_HALO, ch), prev_map(0)),
            pl.BlockSpec((CONV_HALO, ch), prev_map(2)),
            pl.BlockSpec((CONV_HALO, ch), next_map(0)),
            pl.BlockSpec((CONV_HALO, ch), next_map(2)),
            pl.BlockSpec((tm, d), lambda i: (i, 0)),
            pl.BlockSpec((tm, d), lambda i: (i, 1)),
            pl.BlockSpec((tm, d), lambda i: (i, 2)),
            pl.BlockSpec((None, CONV_WIDTH, ch), lambda i: (layer, 0, 0)),
            pl.BlockSpec((None, 1, ch), lambda i: (layer, 0, 0)),
            resident((NA_WIDTH, d)),
            resident((GQA_Q_WIDTH, d)),
            resident((ch, d)),
        ],
        out_specs=pl.BlockSpec((tm, d), lambda i: (i, 0)),
        scratch_shapes=[pltpu.VMEM((tm, ch), BF16)],
        compiler_params=_params(("parallel",)),
        name="branch_merge",
    )(a_na, a_gqa, p_conv, p_conv, p_conv, p_conv, p_conv, p_conv, p_conv, p_gate, p_gate, p_gate,
      conv_w, conv_b, w_na, w_gqa, w_conv)


def _out_proj_kernel(x_ref, m_ref, w_ref, g_post_ref, g_next_ref, o_ref, h_ref):
    half = x_ref.shape[0] // 2
    for r in range(2):
        rows = slice(r * half, (r + 1) * half)
        y = jnp.dot(m_ref[rows, :], w_ref[...], preferred_element_type=F32)
        x_new = x_ref[rows, :] + _rms_rows(y, g_post_ref[...])
        o_ref[rows, :] = x_new
        h_ref[rows, :] = _rms_rows(x_new, g_next_ref[...]).astype(h_ref.dtype)


def _out_proj(x, merged, w_out, post_gains, next_gains, layer):
    t, d = x.shape
    tm = _pick(t, (512, 256))
    gain = lambda: pl.BlockSpec((None, 1, d), lambda i: (layer, 0, 0))
    return pl.pallas_call(
        _out_proj_kernel,
        out_shape=(jax.ShapeDtypeStruct((t, d), F32), jax.ShapeDtypeStruct((t, d), BF16)),
        grid=(t // tm,),
        in_specs=[
            pl.BlockSpec((tm, d), lambda i: (i, 0)),
            pl.BlockSpec((tm, d), lambda i: (i, 0)),
            pl.BlockSpec((None, d, d), lambda i: (layer, 0, 0), pipeline_mode=pl.Buffered(1)),
            gain(),
            gain(),
        ],
        out_specs=(pl.BlockSpec((tm, d), lambda i: (i, 0)), pl.BlockSpec((tm, d), lambda i: (i, 0))),
        compiler_params=_params(("parallel",)),
        name="out_proj_norm",
    )(x, merged, w_out, post_gains, next_gains)


def _ffn_kernel(x_ref, h_ref, wg_ref, wu_ref, wd_ref, g_post_ref, g_next_ref, o_ref, hn_ref, acc_ref):
    f = pl.program_id(1)

    @pl.when(f == 0)
    def _():
        acc_ref[...] = jnp.zeros_like(acc_ref)

    h = h_ref[...]
    a = jnp.dot(h, wg_ref[...], preferred_element_type=F32)
    b = jnp.dot(h, wu_ref[...], preferred_element_type=F32)
    act = (a * jax.nn.sigmoid(a) * b).astype(BF16)
    acc_ref[...] += jnp.dot(act, wd_ref[...], preferred_element_type=F32)

    @pl.when(f == pl.num_programs(1) - 1)
    def _():
        x_new = x_ref[...] + _rms_rows(acc_ref[...], g_post_ref[...])
        o_ref[...] = x_new
        hn_ref[...] = _rms_rows(x_new, g_next_ref[...]).astype(hn_ref.dtype)


def _ffn(x, h, wg, wu, wd, post_gains, next_gains, layer, next_layer):
    t, d = x.shape
    hidden = wg.shape[2]
    tm = _pick(t, (512, 256))
    tf = _pick(hidden, (512, 256, 128))
    return pl.pallas_call(
        _ffn_kernel,
        out_shape=(jax.ShapeDtypeStruct((t, d), F32), jax.ShapeDtypeStruct((t, d), BF16)),
        grid=(t // tm, hidden // tf),
        in_specs=[
            pl.BlockSpec((tm, d), lambda i, f: (i, 0)),
            pl.BlockSpec((tm, d), lambda i, f: (i, 0)),
            pl.BlockSpec((None, d, tf), lambda i, f: (layer, 0, f)),
            pl.BlockSpec((None, d, tf), lambda i, f: (layer, 0, f)),
            pl.BlockSpec((None, tf, d), lambda i, f: (layer, f, 0)),
            pl.BlockSpec((None, 1, d), lambda i, f: (layer, 0, 0)),
            pl.BlockSpec((None, 1, d), lambda i, f: (next_layer, 0, 0)),
        ],
        out_specs=(pl.BlockSpec((tm, d), lambda i, f: (i, 0)), pl.BlockSpec((tm, d), lambda i, f: (i, 0))),
        scratch_shapes=[pltpu.VMEM((tm, d), F32)],
        compiler_params=_params(("parallel", "arbitrary")),
        name="ffn_swiglu",
    )(x, h, wg, wu, wd, post_gains, next_gains)


def kernel(x, w_in, na_rpb, q_norm_g, k_norm_g, conv_w, conv_b, w_br_na, w_br_gqa, w_br_conv, w_out,
           pre_mix_g, post_mix_g, pre_ffn_g, post_ffn_g, w_ffn_gate, w_ffn_up, w_ffn_down):
    batch, seq, d = x.shape
    depth = w_in.shape[0]
    ch = conv_w.shape[2]
    cos, sin = _rope_tables(seq)

    gain3 = lambda g: g.astype(F32).reshape(depth, 1, -1)
    pre_mix_g, post_mix_g, pre_ffn_g, post_ffn_g = map(gain3, (pre_mix_g, post_mix_g, pre_ffn_g, post_ffn_g))
    conv_w = conv_w.astype(F32)
    conv_b = conv_b.astype(F32).reshape(depth, 1, ch)
    w_br_na, w_br_gqa, w_br_conv, w_out = (w.astype(BF16) for w in (w_br_na, w_br_gqa, w_br_conv, w_out))
    w_ffn_gate, w_ffn_up, w_ffn_down = (w.astype(BF16) for w in (w_ffn_gate, w_ffn_up, w_ffn_down))
    q_gain = q_norm_g.astype(F32) * (HEAD_DIM ** -0.5 * LOG2E)
    qk_gains = jnp.concatenate([jnp.tile(q_gain, (1, GQA_Q_HEADS)),
                                jnp.tile(k_norm_g.astype(F32), (1, GQA_KV_HEADS))], axis=1)
    qk_gains = qk_gains.reshape(depth, -1, 1, 2 * HEAD_DIM)

    y = x.reshape(batch * seq, d)
    h = _pre_norm(y, pre_mix_g, 0)
    for l in range(depth):
        p_attn = _in_proj(h, w_in, l, 0, OFF_CONV, "in_proj_attn")
        p_conv = _in_proj(h, w_in, l, OFF_CONV, 3 * ch, "in_proj_conv")
        p_gate = _in_proj(h, w_in, l, OFF_CONV + 3 * ch, 3 * d, "in_proj_gate")

        qk = _qk_prep(p_attn, qk_gains[l], cos, sin, seq)
        a_gqa = _gqa_attention(qk, p_attn, batch, seq)
        a_na = _na_attention(p_attn, na_rpb[l], batch, seq)

        merged = _branch_merge(a_na, a_gqa, p_conv, p_gate, conv_w, conv_b, w_br_na, w_br_gqa,
                               w_br_conv, l, seq)
        y, h = _out_proj(y, merged, w_out, post_mix_g, pre_ffn_g, l)
        y, h = _ffn(y, h, w_ffn_gate, w_ffn_up, w_ffn_down, post_ffn_g, pre_mix_g, l, (l + 1) % depth)
    return y.reshape(batch, seq, d)
```

```python
import functools
import math

import numpy as np
import jax
import jax.numpy as jnp
from jax import lax
from jax.experimental import pallas as pl
from jax.experimental.pallas import tpu as pltpu

GRID_W = 64
HEAD_DIM = 128
NA_HEADS = 8
NA_WIN_ROWS = 8
NA_WIN_COLS = 16
GQA_Q_HEADS = 8
GQA_KV_HEADS = 2
GQA_GROUP = GQA_Q_HEADS // GQA_KV_HEADS
ROPE_THETA = 10000.0
CONV_WIDTH = 3
RMS_EPS = 1e-6

NA_WIDTH = NA_HEADS * HEAD_DIM
GQA_Q_WIDTH = GQA_Q_HEADS * HEAD_DIM
GQA_KV_WIDTH = GQA_KV_HEADS * HEAD_DIM
OFF_NA_Q = 0
OFF_NA_K = NA_WIDTH
OFF_NA_V = 2 * NA_WIDTH
OFF_GQA_Q = 3 * NA_WIDTH
OFF_GQA_K = OFF_GQA_Q + GQA_Q_WIDTH
OFF_GQA_V = OFF_GQA_K + GQA_KV_WIDTH
OFF_CONV = OFF_GQA_V + GQA_KV_WIDTH

NA_GROUP_ROWS = 4
NA_BAND_ROWS = NA_GROUP_ROWS + NA_WIN_ROWS
NA_GROUPS_PER_STEP = 4
IN_PROJ_COLS = 1536
MASK_VALUE = -1e30
LOG2E = math.log2(math.e)

V7X_VMEM_BYTES = 64 * 1024 * 1024
VMEM_LIMIT = 56 * 1024 * 1024

BF16 = jnp.bfloat16
F32 = jnp.float32


def _params(semantics):
    return pltpu.CompilerParams(dimension_semantics=semantics, vmem_limit_bytes=VMEM_LIMIT)


def _pick(n, candidates):
    for c in candidates:
        if n % c == 0:
            return c
    raise ValueError(f"no tile in {candidates} divides {n}")


def _rms_rows(x, g):
    ms = jnp.mean(x * x, axis=-1, keepdims=True)
    return x * lax.rsqrt(ms + RMS_EPS) * g


def _norm_kernel(x_ref, g_ref, o_ref):
    o_ref[...] = _rms_rows(x_ref[...], g_ref[...]).astype(o_ref.dtype)


def _pre_norm(x, gains, layer):
    t, d = x.shape
    tm = _pick(t, (512, 256))
    return pl.pallas_call(
        _norm_kernel,
        out_shape=jax.ShapeDtypeStruct((t, d), BF16),
        grid=(t // tm,),
        in_specs=[
            pl.BlockSpec((tm, d), lambda i: (i, 0)),
            pl.BlockSpec((None, 1, d), lambda i: (layer, 0, 0)),
        ],
        out_specs=pl.BlockSpec((tm, d), lambda i: (i, 0)),
        compiler_params=_params(("parallel",)),
        name="pre_norm",
    )(x, gains)


def _proj_kernel(h_ref, w_ref, o_ref, wb_ref, *, row_chunk):
    @pl.when(pl.program_id(1) == 0)
    def _():
        def body(c, carry):
            rows = pl.ds(pl.multiple_of(c * row_chunk, row_chunk), row_chunk)
            wb_ref[rows, :] = w_ref[rows, :].astype(BF16)
            return carry
        lax.fori_loop(0, w_ref.shape[0] // row_chunk, body, 0)

    o_ref[...] = jnp.dot(h_ref[...], wb_ref[...], preferred_element_type=F32).astype(o_ref.dtype)


def _in_proj(h, w_in, layer, col_off, n_cols, name):
    t, d = h.shape
    tm = _pick(t, (1024, 512, 256))
    tn = IN_PROJ_COLS
    assert col_off % tn == 0 and n_cols % tn == 0
    col0 = col_off // tn
    return pl.pallas_call(
        functools.partial(_proj_kernel, row_chunk=min(d, 256)),
        out_shape=jax.ShapeDtypeStruct((t, n_cols), BF16),
        grid=(n_cols // tn, t // tm),
        in_specs=[
            pl.BlockSpec((tm, d), lambda j, i: (i, 0)),
            pl.BlockSpec((None, d, tn), lambda j, i: (layer, 0, col0 + j)),
        ],
        out_specs=pl.BlockSpec((tm, tn), lambda j, i: (i, j)),
        scratch_shapes=[pltpu.VMEM((d, tn), BF16)],
        compiler_params=_params(("parallel", "arbitrary")),
        name=name,
    )(h, w_in)


def _rope_tables(seq):
    half = HEAD_DIM // 2
    quarter = half // 2
    inv_freq = 1.0 / (ROPE_THETA ** (np.arange(quarter, dtype=np.float64) / quarter))
    t = np.arange(seq)
    ang_r = (t // GRID_W)[:, None] * inv_freq[None, :]
    ang_c = (t % GRID_W)[:, None] * inv_freq[None, :]
    cos = np.concatenate([np.cos(ang_r), np.cos(ang_r), np.cos(ang_c), np.cos(ang_c)], axis=1)
    sin = np.concatenate([-np.sin(ang_r), np.sin(ang_r), -np.sin(ang_c), np.sin(ang_c)], axis=1)
    return jnp.asarray(cos, F32), jnp.asarray(sin, F32)


def _qk_prep_kernel(x_ref, g_ref, cos_ref, sin_ref, o_ref):
    heads = x_ref.shape[1] // HEAD_DIM
    quarter = HEAD_DIM // 4
    cos = cos_ref[...]
    sin = sin_ref[...]
    lane = lax.broadcasted_iota(jnp.int32, cos.shape, 1)
    first = (lane % (2 * quarter)) < quarter
    for h in range(heads):
        cols = slice(h * HEAD_DIM, (h + 1) * HEAD_DIM)
        y = _rms_rows(x_ref[:, cols].astype(F32), g_ref[0, :, cols])
        partner = jnp.where(first, pltpu.roll(y, HEAD_DIM - quarter, 1), pltpu.roll(y, quarter, 1))
        o_ref[:, cols] = (y * cos + partner * sin).astype(o_ref.dtype)


def _qk_prep(proj, gains, cos, sin, seq):
    t = proj.shape[0]
    n_heads = GQA_Q_HEADS + GQA_KV_HEADS
    width = 2 * HEAD_DIM
    assert OFF_GQA_Q % width == 0 and n_heads % 2 == 0
    tm = _pick(seq, (2048, 1024, 512, 256))
    blocks_per_seq = seq // tm
    col0 = OFF_GQA_Q // width
    return pl.pallas_call(
        _qk_prep_kernel,
        out_shape=jax.ShapeDtypeStruct((t, n_heads * HEAD_DIM), BF16),
        grid=(t // tm, n_heads // 2),
        in_specs=[
            pl.BlockSpec((tm, width), lambda i, j: (i, col0 + j)),
            pl.BlockSpec((1, 1, width), lambda i, j: (j, 0, 0)),
            pl.BlockSpec((tm, HEAD_DIM), lambda i, j: (i % blocks_per_seq, 0)),
            pl.BlockSpec((tm, HEAD_DIM), lambda i, j: (i % blocks_per_seq, 0)),
        ],
        out_specs=pl.BlockSpec((tm, width), lambda i, j: (i, j)),
        compiler_params=_params(("parallel", "parallel")),
        name="gqa_qk_prep",
    )(proj, gains, cos, sin)


GQA_ONES_ROWS = 16
GQA_PLAIN_EXP_BOUND = 64.0


def _gqa_logit_bound(q_gain, k_gain):
    return 1.02 * HEAD_DIM * jnp.max(jnp.abs(q_gain)) * jnp.max(jnp.abs(k_gain))


def _gqa_kernel(plain_ref, q_ref, k_ref, v_ref, o_ref, qs_ref, vt_ref, acc_ref, s_ref, m_ref, *, tk):
    tq = q_ref.shape[0]
    seq = k_ref.shape[0]
    n_chunks = seq // tk
    assert n_chunks % 2 == 0

    def chunk(c):
        return pl.ds(pl.multiple_of(c * tk, tk), tk)

    @pl.when(pl.program_id(2) == 0)
    def _():
        def body(c, carry):
            vt_ref[c, :HEAD_DIM, :] = v_ref[chunk(c), :].astype(F32).T.astype(BF16)
            vt_ref[c, HEAD_DIM:, :] = jnp.ones((GQA_ONES_ROWS, tk), BF16)
            return carry
        lax.fori_loop(0, n_chunks, body, 0)

    for h in range(GQA_GROUP):
        qs_ref[h * tq:(h + 1) * tq, :] = q_ref[:, h * HEAD_DIM:(h + 1) * HEAD_DIM]
    acc_ref[...] = jnp.zeros_like(acc_ref)

    def scores(c):
        return lax.dot_general(k_ref[chunk(c), :], qs_ref[...], (((1,), (1,)), ((), ())),
                               preferred_element_type=F32)

    @pl.when(plain_ref[0] != 0)
    def _():
        def body(t, carry):
            pv = None
            for u in range(2):
                c = 2 * t + u
                p = jnp.exp2(scores(c)).astype(BF16)
                part = jnp.dot(vt_ref[c], p, preferred_element_type=F32)
                pv = part if pv is None else pv + part
            acc_ref[...] += pv
            return carry
        lax.fori_loop(0, n_chunks // 2, body, 0)

    @pl.when(plain_ref[0] == 0)
    def _():
        m_ref[...] = jnp.full_like(m_ref, -jnp.inf)

        def accumulate(c, slot):
            s = s_ref[slot]
            m_old = m_ref[...]
            m_new = jnp.maximum(m_old, jnp.max(s, axis=0, keepdims=True))
            m_ref[...] = m_new
            p = jnp.exp2(s - m_new).astype(BF16)
            pv = jnp.dot(vt_ref[c], p, preferred_element_type=F32)
            acc_ref[...] = acc_ref[...] * jnp.exp2(m_old - m_new) + pv

        s_ref[0] = scores(0)

        def body(t, carry):
            s_ref[1] = scores(2 * t + 1)
            accumulate(2 * t, 0)
            s_ref[0] = scores(2 * t + 2)
            accumulate(2 * t + 1, 1)
            return carry

        lax.fori_loop(0, n_chunks // 2 - 1, body, 0)
        s_ref[1] = scores(n_chunks - 1)
        accumulate(n_chunks - 2, 0)
        accumulate(n_chunks - 1, 1)

    out_t = acc_ref[:HEAD_DIM, :] / acc_ref[HEAD_DIM:HEAD_DIM + 1, :]
    for h in range(GQA_GROUP):
        o_ref[:, h * HEAD_DIM:(h + 1) * HEAD_DIM] = out_t[:, h * tq:(h + 1) * tq].T.astype(o_ref.dtype)


def _gqa_attention(qk, proj, plain_exp, batch, seq):
    t = qk.shape[0]
    tq = _pick(seq, (256, 128))
    tk = _pick(seq, (512, 256, 128))
    gw = GQA_GROUP * HEAD_DIM
    k_col0 = GQA_Q_WIDTH // HEAD_DIM
    v_col0 = OFF_GQA_V // HEAD_DIM
    q_blocks = seq // tq
    grid_spec = pltpu.PrefetchScalarGridSpec(
        num_scalar_prefetch=1,
        grid=(batch, GQA_KV_HEADS, q_blocks),
        in_specs=[
            pl.BlockSpec((tq, gw), lambda b, g, i, flag: (b * q_blocks + i, g)),
            pl.BlockSpec((seq, HEAD_DIM), lambda b, g, i, flag: (b, k_col0 + g)),
            pl.BlockSpec((seq, HEAD_DIM), lambda b, g, i, flag: (b, v_col0 + g)),
        ],
        out_specs=pl.BlockSpec((tq, gw), lambda b, g, i, flag: (b * q_blocks + i, g)),
        scratch_shapes=[
            pltpu.VMEM((GQA_GROUP * tq, HEAD_DIM), BF16),
            pltpu.VMEM((seq // tk, HEAD_DIM + GQA_ONES_ROWS, tk), BF16),
            pltpu.VMEM((HEAD_DIM + GQA_ONES_ROWS, GQA_GROUP * tq), F32),
            pltpu.VMEM((2, tk, GQA_GROUP * tq), F32),
            pltpu.VMEM((1, GQA_GROUP * tq), F32),
        ],
    )
    return pl.pallas_call(
        functools.partial(_gqa_kernel, tk=tk),
        out_shape=jax.ShapeDtypeStruct((t, GQA_Q_WIDTH), BF16),
        grid_spec=grid_spec,
        compiler_params=_params(("parallel", "parallel", "arbitrary")),
        name="gqa_flash",
    )(plain_exp, qk, qk, proj)


NA_PAIR_ROWS = 2 * NA_WIN_ROWS
NA_MASK_BOTH, NA_MASK_LEFT, NA_MASK_RIGHT = 0, 1, 2


def _na_block_plan(rows):
    gr, br = NA_GROUP_ROWS, NA_BAND_ROWS
    plan = []
    for r0 in (0, min(gr, rows - gr), rows - gr):
        band_start = int(np.clip(r0 - NA_WIN_ROWS // 2, 0, rows - br))
        per_row = []
        for i in range(gr):
            r = r0 + i
            row_start = int(np.clip(r - NA_WIN_ROWS // 2, 0, rows - NA_WIN_ROWS))
            blocks = []
            for jp in range(br // 2):
                key_row = band_start + 2 * jp
                ok_l = row_start <= key_row < row_start + NA_WIN_ROWS
                ok_r = row_start <= key_row + 1 < row_start + NA_WIN_ROWS
                if not (ok_l or ok_r):
                    blocks.append(None)
                    continue
                kind = NA_MASK_BOTH if (ok_l and ok_r) else (NA_MASK_LEFT if ok_l else NA_MASK_RIGHT)
                blocks.append((key_row - r + NA_WIN_ROWS, kind))
            per_row.append(blocks)
        plan.append(per_row)
    return plan


def _na_pair_table(rpb):
    h, nr, nc = rpb.shape
    rpb = jnp.pad(rpb.astype(F32), ((0, 0), (0, 0), (0, GRID_W - nc)))
    zero = jnp.zeros((h, 1, GRID_W), F32)
    left = jnp.concatenate([zero, rpb], axis=1)
    right = jnp.concatenate([rpb, zero], axis=1)
    return jnp.concatenate([left, right], axis=2)


def _na_col_masks():
    w = np.arange(GRID_W)[:, None]
    c = np.arange(2 * GRID_W)[None, :] % GRID_W
    col_start = np.clip(w - NA_WIN_COLS // 2, 0, GRID_W - NA_WIN_COLS)
    base = np.where((c >= col_start) & (c < col_start + NA_WIN_COLS), 0.0, MASK_VALUE)
    left_half = np.arange(2 * GRID_W)[None, :] < GRID_W
    masks = np.stack([base, np.where(left_half, base, MASK_VALUE), np.where(left_half, MASK_VALUE, base)])
    return jnp.asarray(masks, F32)


def _na_kernel(q_ref, k_ref, v_ref, pair_ref, mask_ref, o_ref, tile_ref, table_ref, *, rows):
    g = pl.program_id(2)
    n_groups = rows // NA_GROUP_ROWS
    blk_w = 2 * GRID_W

    @pl.when(g == 0)
    def _():
        for p in range(NA_PAIR_ROWS):
            row = jnp.broadcast_to(pair_ref[p:p + 1, :], (GRID_W, blk_w))
            tile_ref[p] = pltpu.roll(row, blk_w - (NA_WIN_COLS - 1), 1, stride=1, stride_axis=0)
        for v, per_row in enumerate(_na_block_plan(rows)):
            for i, blocks in enumerate(per_row):
                for jp, blk in enumerate(blocks):
                    dst = (v, slice(i * GRID_W, (i + 1) * GRID_W), slice(jp * blk_w, (jp + 1) * blk_w))
                    if blk is None:
                        table_ref[dst] = jnp.full((GRID_W, blk_w), MASK_VALUE, F32)
                    else:
                        table_ref[dst] = (tile_ref[blk[0]] + mask_ref[blk[1]]) * LOG2E

    gq = NA_GROUP_ROWS * GRID_W
    band_rows = NA_BAND_ROWS * GRID_W
    for u in range(q_ref.shape[0] // gq):
        gi = g * (q_ref.shape[0] // gq) + u
        start_row = jnp.clip(gi * NA_GROUP_ROWS - NA_WIN_ROWS // 2, 0, rows - NA_BAND_ROWS)
        band = pl.ds(pl.multiple_of(start_row * GRID_W, GRID_W), band_rows)
        variant = jnp.where(gi == 0, 0, jnp.where(gi == n_groups - 1, 2, 1))
        s = lax.dot_general(q_ref[u * gq:(u + 1) * gq, :], k_ref[band, :], (((1,), (1,)), ((), ())),
                            preferred_element_type=F32)
        s = s * (HEAD_DIM ** -0.5 * LOG2E) + table_ref[variant]
        p = jnp.exp2(s - jnp.max(s, axis=-1, keepdims=True))
        l = jnp.sum(p, axis=-1, keepdims=True)
        o = jnp.dot(p.astype(BF16), v_ref[band, :], preferred_element_type=F32)
        o_ref[u * gq:(u + 1) * gq, :] = (o / l).astype(o_ref.dtype)


def _na_attention(proj, rpb, batch, seq):
    t = proj.shape[0]
    rows = seq // GRID_W
    assert NA_GROUP_ROWS % 4 == 0 and rows % NA_GROUP_ROWS == 0 and rows >= NA_BAND_ROWS
    assert 2 * GRID_W == HEAD_DIM
    n_groups = rows // NA_GROUP_ROWS
    gq = NA_GROUP_ROWS * GRID_W
    gk = NA_BAND_ROWS * GRID_W
    per_step = _pick(n_groups, (NA_GROUPS_PER_STEP, 2, 1))
    n_steps = n_groups // per_step
    k_col0 = OFF_NA_K // HEAD_DIM
    v_col0 = OFF_NA_V // HEAD_DIM
    return pl.pallas_call(
        functools.partial(_na_kernel, rows=rows),
        out_shape=jax.ShapeDtypeStruct((t, NA_WIDTH), BF16),
        grid=(batch, NA_HEADS, n_steps),
        in_specs=[
            pl.BlockSpec((per_step * gq, HEAD_DIM), lambda b, h, g: (b * n_steps + g, h)),
            pl.BlockSpec((seq, HEAD_DIM), lambda b, h, g: (b, k_col0 + h)),
            pl.BlockSpec((seq, HEAD_DIM), lambda b, h, g: (b, v_col0 + h)),
            pl.BlockSpec((None, NA_PAIR_ROWS, 2 * GRID_W), lambda b, h, g: (h, 0, 0)),
            pl.BlockSpec((3, GRID_W, 2 * GRID_W), lambda b, h, g: (0, 0, 0)),
        ],
        out_specs=pl.BlockSpec((per_step * gq, HEAD_DIM), lambda b, h, g: (b * n_steps + g, h)),
        scratch_shapes=[
            pltpu.VMEM((NA_PAIR_ROWS, GRID_W, 2 * GRID_W), F32),
            pltpu.VMEM((3, gq, gk), F32),
        ],
        compiler_params=_params(("parallel", "parallel", "arbitrary")),
        name="na_attention",
    )(proj, proj, proj, _na_pair_table(rpb), _na_col_masks())


CONV_HALO = 16


def _branch_merge_kernel(a_na_ref, a_gqa_ref, h_ref, b_ref, c_ref, hp_ref, cp_ref, hn_ref, cn_ref,
                         g_na_ref, g_gqa_ref, g_conv_ref, cw_ref, cb_ref, w_na_ref, w_gqa_ref,
                         w_conv_ref, o_ref, a_conv_ref, *, blocks_per_seq, col_chunk):
    tm = h_ref.shape[0]
    pos = pl.program_id(0) % blocks_per_seq

    u = c_ref[...].astype(F32) * h_ref[...].astype(F32)
    u_prev_row = cp_ref[CONV_HALO - 1:CONV_HALO, :].astype(F32) * hp_ref[CONV_HALO - 1:CONV_HALO, :].astype(F32)
    u_prev_row = jnp.where(pos == 0, 0.0, u_prev_row)
    u_next_row = cn_ref[0:1, :].astype(F32) * hn_ref[0:1, :].astype(F32)
    u_next_row = jnp.where(pos == blocks_per_seq - 1, 0.0, u_next_row)
    row = lax.broadcasted_iota(jnp.int32, u.shape, 0)
    u_prev = jnp.where(row == 0, u_prev_row, pltpu.roll(u, 1, 0))
    u_next = jnp.where(row == tm - 1, u_next_row, pltpu.roll(u, tm - 1, 0))
    y = cb_ref[...] + u_prev * cw_ref[0:1, :] + u * cw_ref[1:2, :] + u_next * cw_ref[2:3, :]
    a_conv_ref[...] = (b_ref[...].astype(F32) * y).astype(BF16)

    for c in range(o_ref.shape[1] // col_chunk):
        cols = slice(c * col_chunk, (c + 1) * col_chunk)

        def branch(a_ref, w_ref, gate_ref):
            y = jnp.dot(a_ref[...], w_ref[:, cols], preferred_element_type=F32)
            return jax.nn.sigmoid(gate_ref[:, cols].astype(F32)) * y

        merged = (branch(a_na_ref, w_na_ref, g_na_ref) + branch(a_gqa_ref, w_gqa_ref, g_gqa_ref)
                  + branch(a_conv_ref, w_conv_ref, g_conv_ref))
        o_ref[:, cols] = merged.astype(o_ref.dtype)


def _branch_merge(a_na, a_gqa, p_conv, p_gate, conv_w, conv_b, w_na, w_gqa, w_conv, layer, seq):
    t = a_na.shape[0]
    ch = conv_w.shape[2]
    d = w_na.shape[2]
    assert CONV_WIDTH == 3
    tm = _pick(seq, (512, 256))
    per_halo = tm // CONV_HALO
    last_halo = t // CONV_HALO - 1
    prev_map = lambda col: (lambda i: (jnp.maximum(i * per_halo - 1, 0), col))
    next_map = lambda col: (lambda i: (jnp.minimum((i + 1) * per_halo, last_halo), col))
    resident = lambda shape: pl.BlockSpec((None,) + shape, lambda i: (layer, 0, 0),
                                          pipeline_mode=pl.Buffered(1))
    return pl.pallas_call(
        functools.partial(_branch_merge_kernel, blocks_per_seq=seq // tm,
                          col_chunk=_pick(d, (512, 256, 128))),
        out_shape=jax.ShapeDtypeStruct((t, d), BF16),
        grid=(t // tm,),
        in_specs=[
            pl.BlockSpec((tm, NA_WIDTH), lambda i: (i, 0)),
            pl.BlockSpec((tm, GQA_Q_WIDTH), lambda i: (i, 0)),
            pl.BlockSpec((tm, ch), lambda i: (i, 0)),
            pl.BlockSpec((tm, ch), lambda i: (i, 1)),
            pl.BlockSpec((tm, ch), lambda i: (i, 2)),
            pl.BlockSpec((CONV_HALO, ch), prev_map(0)),
            pl.BlockSpec((CONV_HALO, ch), prev_map(2)),
            pl.BlockSpec((CONV_HALO, ch), next_map(0)),
            pl.BlockSpec((CONV_HALO, ch), next_map(2)),
            pl.BlockSpec((tm, d), lambda i: (i, 0)),
            pl.BlockSpec((tm, d), lambda i: (i, 1)),
            pl.BlockSpec((tm, d), lambda i: (i, 2)),
            pl.BlockSpec((None, CONV_WIDTH, ch), lambda i: (layer, 0, 0)),
            pl.BlockSpec((None, 1, ch), lambda i: (layer, 0, 0)),
            resident((NA_WIDTH, d)),
            resident((GQA_Q_WIDTH, d)),
            resident((ch, d)),
        ],
        out_specs=pl.BlockSpec((tm, d), lambda i: (i, 0)),
        scratch_shapes=[pltpu.VMEM((tm, ch), BF16)],
        compiler_params=_params(("parallel",)),
        name="branch_merge",
    )(a_na, a_gqa, p_conv, p_conv, p_conv, p_conv, p_conv, p_conv, p_conv, p_gate, p_gate, p_gate,
      conv_w, conv_b, w_na, w_gqa, w_conv)


def _out_proj_kernel(x_ref, m_ref, w_ref, g_post_ref, g_next_ref, o_ref, h_ref):
    half = x_ref.shape[0] // 2
    for r in range(2):
        rows = slice(r * half, (r + 1) * half)
        y = jnp.dot(m_ref[rows, :], w_ref[...], preferred_element_type=F32)
        x_new = x_ref[rows, :] + _rms_rows(y, g_post_ref[...])
        o_ref[rows, :] = x_new
        h_ref[rows, :] = _rms_rows(x_new, g_next_ref[...]).astype(h_ref.dtype)


def _out_proj(x, merged, w_out, post_gains, next_gains, layer):
    t, d = x.shape
    tm = _pick(t, (512, 256))
    gain = lambda: pl.BlockSpec((None, 1, d), lambda i: (layer, 0, 0))
    return pl.pallas_call(
        _out_proj_kernel,
        out_shape=(jax.ShapeDtypeStruct((t, d), F32), jax.ShapeDtypeStruct((t, d), BF16)),
        grid=(t // tm,),
        in_specs=[
            pl.BlockSpec((tm, d), lambda i: (i, 0)),
            pl.BlockSpec((tm, d), lambda i: (i, 0)),
            pl.BlockSpec((None, d, d), lambda i: (layer, 0, 0), pipeline_mode=pl.Buffered(1)),
            gain(),
            gain(),
        ],
        out_specs=(pl.BlockSpec((tm, d), lambda i: (i, 0)), pl.BlockSpec((tm, d), lambda i: (i, 0))),
        compiler_params=_params(("parallel",)),
        name="out_proj_norm",
    )(x, merged, w_out, post_gains, next_gains)


def _ffn_kernel(x_ref, h_ref, wg_ref, wu_ref, wd_ref, g_post_ref, g_next_ref, o_ref, hn_ref, acc_ref):
    f = pl.program_id(1)

    @pl.when(f == 0)
    def _():
        acc_ref[...] = jnp.zeros_like(acc_ref)

    h = h_ref[...]
    a = jnp.dot(h, wg_ref[...], preferred_element_type=F32)
    b = jnp.dot(h, wu_ref[...], preferred_element_type=F32)
    act = (a * jax.nn.sigmoid(a) * b).astype(BF16)
    acc_ref[...] += jnp.dot(act, wd_ref[...], preferred_element_type=F32)

    @pl.when(f == pl.num_programs(1) - 1)
    def _():
        x_new = x_ref[...] + _rms_rows(acc_ref[...], g_post_ref[...])
        o_ref[...] = x_new
        hn_ref[...] = _rms_rows(x_new, g_next_ref[...]).astype(hn_ref.dtype)


def _ffn(x, h, wg, wu, wd, post_gains, next_gains, layer, next_layer):
    t, d = x.shape
    hidden = wg.shape[2]
    tm = _pick(t, (512, 256))
    tf = _pick(hidden, (512, 256, 128))
    return pl.pallas_call(
        _ffn_kernel,
        out_shape=(jax.ShapeDtypeStruct((t, d), F32), jax.ShapeDtypeStruct((t, d), BF16)),
        grid=(t // tm, hidden // tf),
        in_specs=[
            pl.BlockSpec((tm, d), lambda i, f: (i, 0)),
            pl.BlockSpec((tm, d), lambda i, f: (i, 0)),
            pl.BlockSpec((None, d, tf), lambda i, f: (layer, 0, f)),
            pl.BlockSpec((None, d, tf), lambda i, f: (layer, 0, f)),
            pl.BlockSpec((None, tf, d), lambda i, f: (layer, f, 0)),
            pl.BlockSpec((None, 1, d), lambda i, f: (layer, 0, 0)),
            pl.BlockSpec((None, 1, d), lambda i, f: (next_layer, 0, 0)),
        ],
        out_specs=(pl.BlockSpec((tm, d), lambda i, f: (i, 0)), pl.BlockSpec((tm, d), lambda i, f: (i, 0))),
        scratch_shapes=[pltpu.VMEM((tm, d), F32)],
        compiler_params=_params(("parallel", "arbitrary")),
        name="ffn_swiglu",
    )(x, h, wg, wu, wd, post_gains, next_gains)


def kernel(x, w_in, na_rpb, q_norm_g, k_norm_g, conv_w, conv_b, w_br_na, w_br_gqa, w_br_conv, w_out,
           pre_mix_g, post_mix_g, pre_ffn_g, post_ffn_g, w_ffn_gate, w_ffn_up, w_ffn_down):
    batch, seq, d = x.shape
    depth = w_in.shape[0]
    ch = conv_w.shape[2]
    cos, sin = _rope_tables(seq)

    gain3 = lambda g: g.astype(F32).reshape(depth, 1, -1)
    pre_mix_g, post_mix_g, pre_ffn_g, post_ffn_g = map(gain3, (pre_mix_g, post_mix_g, pre_ffn_g, post_ffn_g))
    conv_w = conv_w.astype(F32)
    conv_b = conv_b.astype(F32).reshape(depth, 1, ch)
    w_br_na, w_br_gqa, w_br_conv, w_out = (w.astype(BF16) for w in (w_br_na, w_br_gqa, w_br_conv, w_out))
    w_ffn_gate, w_ffn_up, w_ffn_down = (w.astype(BF16) for w in (w_ffn_gate, w_ffn_up, w_ffn_down))
    q_gain = q_norm_g.astype(F32) * (HEAD_DIM ** -0.5 * LOG2E)
    qk_gains = jnp.concatenate([jnp.tile(q_gain, (1, GQA_Q_HEADS)),
                                jnp.tile(k_norm_g.astype(F32), (1, GQA_KV_HEADS))], axis=1)
    qk_gains = qk_gains.reshape(depth, -1, 1, 2 * HEAD_DIM)

    y = x.reshape(batch * seq, d)
    h = _pre_norm(y, pre_mix_g, 0)
    for l in range(depth):
        p_attn = _in_proj(h, w_in, l, 0, OFF_CONV, "in_proj_attn")
        p_conv = _in_proj(h, w_in, l, OFF_CONV, 3 * ch, "in_proj_conv")
        p_gate = _in_proj(h, w_in, l, OFF_CONV + 3 * ch, 3 * d, "in_proj_gate")

        qk = _qk_prep(p_attn, qk_gains[l], cos, sin, seq)
        plain_exp = (_gqa_logit_bound(q_gain[l], k_norm_g[l].astype(F32)) <= GQA_PLAIN_EXP_BOUND)
        a_gqa = _gqa_attention(qk, p_attn, plain_exp.astype(jnp.int32).reshape(1), batch, seq)
        a_na = _na_attention(p_attn, na_rpb[l], batch, seq)

        merged = _branch_merge(a_na, a_gqa, p_conv, p_gate, conv_w, conv_b, w_br_na, w_br_gqa,
                               w_br_conv, l, seq)
        y, h = _out_proj(y, merged, w_out, post_mix_g, pre_ffn_g, l)
        y, h = _ffn(y, h, w_ffn_gate, w_ffn_up, w_ffn_down, post_ffn_g, pre_mix_g, l, (l + 1) % depth)
    return y.reshape(batch, seq, d)
```

```python
import functools
import math

import numpy as np
import jax
import jax.numpy as jnp
from jax import lax
from jax.experimental import pallas as pl
from jax.experimental.pallas import tpu as pltpu

GRID_W = 64
HEAD_DIM = 128
NA_HEADS = 8
NA_WIN_ROWS = 8
NA_WIN_COLS = 16
GQA_Q_HEADS = 8
GQA_KV_HEADS = 2
GQA_GROUP = GQA_Q_HEADS // GQA_KV_HEADS
ROPE_THETA = 10000.0
CONV_WIDTH = 3
RMS_EPS = 1e-6

NA_WIDTH = NA_HEADS * HEAD_DIM
GQA_Q_WIDTH = GQA_Q_HEADS * HEAD_DIM
GQA_KV_WIDTH = GQA_KV_HEADS * HEAD_DIM
OFF_NA_Q = 0
OFF_NA_K = NA_WIDTH
OFF_NA_V = 2 * NA_WIDTH
OFF_GQA_Q = 3 * NA_WIDTH
OFF_GQA_K = OFF_GQA_Q + GQA_Q_WIDTH
OFF_GQA_V = OFF_GQA_K + GQA_KV_WIDTH
OFF_CONV = OFF_GQA_V + GQA_KV_WIDTH

NA_GROUP_ROWS = 4
NA_BAND_ROWS = NA_GROUP_ROWS + NA_WIN_ROWS
NA_GROUPS_PER_STEP = 4
IN_PROJ_COLS = 1536
MASK_VALUE = -1e30
LOG2E = math.log2(math.e)

V7X_VMEM_BYTES = 64 * 1024 * 1024
VMEM_LIMIT = 56 * 1024 * 1024

BF16 = jnp.bfloat16
F32 = jnp.float32


def _params(semantics):
    return pltpu.CompilerParams(dimension_semantics=semantics, vmem_limit_bytes=VMEM_LIMIT)


def _pick(n, candidates):
    for c in candidates:
        if n % c == 0:
            return c
    raise ValueError(f"no tile in {candidates} divides {n}")


def _sigmoid(x):
    return 0.5 * jnp.tanh(0.5 * x) + 0.5


def _rms_rows(x, g):
    ms = jnp.mean(x * x, axis=-1, keepdims=True)
    return x * lax.rsqrt(ms + RMS_EPS) * g


def _norm_kernel(x_ref, g_ref, o_ref):
    o_ref[...] = _rms_rows(x_ref[...], g_ref[...]).astype(o_ref.dtype)


def _pre_norm(x, gains, layer):
    t, d = x.shape
    tm = _pick(t, (512, 256))
    return pl.pallas_call(
        _norm_kernel,
        out_shape=jax.ShapeDtypeStruct((t, d), BF16),
        grid=(t // tm,),
        in_specs=[
            pl.BlockSpec((tm, d), lambda i: (i, 0)),
            pl.BlockSpec((None, 1, d), lambda i: (layer, 0, 0)),
        ],
        out_specs=pl.BlockSpec((tm, d), lambda i: (i, 0)),
        compiler_params=_params(("parallel",)),
        name="pre_norm",
    )(x, gains)


def _proj_kernel(h_ref, w_ref, o_ref, wb_ref, *, row_chunk):
    @pl.when(pl.program_id(1) == 0)
    def _():
        def body(c, carry):
            rows = pl.ds(pl.multiple_of(c * row_chunk, row_chunk), row_chunk)
            wb_ref[rows, :] = w_ref[rows, :].astype(BF16)
            return carry
        lax.fori_loop(0, w_ref.shape[0] // row_chunk, body, 0)

    o_ref[...] = jnp.dot(h_ref[...], wb_ref[...], preferred_element_type=F32).astype(o_ref.dtype)


def _in_proj(h, w_in, layer, col_off, n_cols, name):
    t, d = h.shape
    tm = _pick(t, (1024, 512, 256))
    tn = IN_PROJ_COLS
    assert col_off % tn == 0 and n_cols % tn == 0
    col0 = col_off // tn
    return pl.pallas_call(
        functools.partial(_proj_kernel, row_chunk=min(d, 256)),
        out_shape=jax.ShapeDtypeStruct((t, n_cols), BF16),
        grid=(n_cols // tn, t // tm),
        in_specs=[
            pl.BlockSpec((tm, d), lambda j, i: (i, 0)),
            pl.BlockSpec((None, d, tn), lambda j, i: (layer, 0, col0 + j)),
        ],
        out_specs=pl.BlockSpec((tm, tn), lambda j, i: (i, j)),
        scratch_shapes=[pltpu.VMEM((d, tn), BF16)],
        compiler_params=_params(("parallel", "arbitrary")),
        name=name,
    )(h, w_in)


def _rope_tables(seq):
    half = HEAD_DIM // 2
    quarter = half // 2
    inv_freq = 1.0 / (ROPE_THETA ** (np.arange(quarter, dtype=np.float64) / quarter))
    t = np.arange(seq)
    ang_r = (t // GRID_W)[:, None] * inv_freq[None, :]
    ang_c = (t % GRID_W)[:, None] * inv_freq[None, :]
    cos = np.concatenate([np.cos(ang_r), np.cos(ang_r), np.cos(ang_c), np.cos(ang_c)], axis=1)
    sin = np.concatenate([-np.sin(ang_r), np.sin(ang_r), -np.sin(ang_c), np.sin(ang_c)], axis=1)
    return jnp.asarray(cos, F32), jnp.asarray(sin, F32)


def _qk_prep_kernel(x_ref, g_ref, cos_ref, sin_ref, o_ref):
    heads = x_ref.shape[1] // HEAD_DIM
    quarter = HEAD_DIM // 4
    cos = cos_ref[...]
    sin = sin_ref[...]
    lane = lax.broadcasted_iota(jnp.int32, cos.shape, 1)
    first = (lane % (2 * quarter)) < quarter
    for h in range(heads):
        cols = slice(h * HEAD_DIM, (h + 1) * HEAD_DIM)
        y = _rms_rows(x_ref[:, cols].astype(F32), g_ref[0, :, cols])
        partner = jnp.where(first, pltpu.roll(y, HEAD_DIM - quarter, 1), pltpu.roll(y, quarter, 1))
        o_ref[:, cols] = (y * cos + partner * sin).astype(o_ref.dtype)


def _qk_prep(proj, gains, cos, sin, seq):
    t = proj.shape[0]
    n_heads = GQA_Q_HEADS + GQA_KV_HEADS
    width = 2 * HEAD_DIM
    assert OFF_GQA_Q % width == 0 and n_heads % 2 == 0
    tm = _pick(seq, (2048, 1024, 512, 256))
    blocks_per_seq = seq // tm
    col0 = OFF_GQA_Q // width
    return pl.pallas_call(
        _qk_prep_kernel,
        out_shape=jax.ShapeDtypeStruct((t, n_heads * HEAD_DIM), BF16),
        grid=(t // tm, n_heads // 2),
        in_specs=[
            pl.BlockSpec((tm, width), lambda i, j: (i, col0 + j)),
            pl.BlockSpec((1, 1, width), lambda i, j: (j, 0, 0)),
            pl.BlockSpec((tm, HEAD_DIM), lambda i, j: (i % blocks_per_seq, 0)),
            pl.BlockSpec((tm, HEAD_DIM), lambda i, j: (i % blocks_per_seq, 0)),
        ],
        out_specs=pl.BlockSpec((tm, width), lambda i, j: (i, j)),
        compiler_params=_params(("parallel", "parallel")),
        name="gqa_qk_prep",
    )(proj, gains, cos, sin)


GQA_ONES_ROWS = 16
GQA_PLAIN_EXP_BOUND = 64.0


def _gqa_logit_bound(q_gain, k_gain):
    return 1.02 * HEAD_DIM * jnp.max(jnp.abs(q_gain)) * jnp.max(jnp.abs(k_gain))


def _gqa_kernel(plain_ref, q_ref, k_ref, v_ref, o_ref, qs_ref, vt_ref, acc_ref, s_ref, m_ref, *, tk):
    tq = q_ref.shape[0]
    seq = k_ref.shape[0]
    n_chunks = seq // tk
    assert n_chunks % 2 == 0

    def chunk(c):
        return pl.ds(pl.multiple_of(c * tk, tk), tk)

    @pl.when(pl.program_id(2) == 0)
    def _():
        def body(c, carry):
            vt_ref[c, :HEAD_DIM, :] = v_ref[chunk(c), :].astype(F32).T.astype(BF16)
            vt_ref[c, HEAD_DIM:, :] = jnp.ones((GQA_ONES_ROWS, tk), BF16)
            return carry
        lax.fori_loop(0, n_chunks, body, 0)

    for h in range(GQA_GROUP):
        qs_ref[h * tq:(h + 1) * tq, :] = q_ref[:, h * HEAD_DIM:(h + 1) * HEAD_DIM]
    acc_ref[...] = jnp.zeros_like(acc_ref)

    def scores(c):
        return lax.dot_general(k_ref[chunk(c), :], qs_ref[...], (((1,), (1,)), ((), ())),
                               preferred_element_type=F32)

    @pl.when(plain_ref[0] != 0)
    def _():
        per_trip = 4 if n_chunks % 4 == 0 else 2

        def body(t, carry):
            pv = None
            for u in range(per_trip):
                c = per_trip * t + u
                p = jnp.exp2(scores(c)).astype(BF16)
                part = jnp.dot(vt_ref[c], p, preferred_element_type=F32)
                pv = part if pv is None else pv + part
            acc_ref[...] += pv
            return carry
        lax.fori_loop(0, n_chunks // per_trip, body, 0)

    @pl.when(plain_ref[0] == 0)
    def _():
        m_ref[...] = jnp.full_like(m_ref, -jnp.inf)

        def accumulate(c, slot):
            s = s_ref[slot]
            m_old = m_ref[...]
            m_new = jnp.maximum(m_old, jnp.max(s, axis=0, keepdims=True))
            m_ref[...] = m_new
            p = jnp.exp2(s - m_new).astype(BF16)
            pv = jnp.dot(vt_ref[c], p, preferred_element_type=F32)
            acc_ref[...] = acc_ref[...] * jnp.exp2(m_old - m_new) + pv

        s_ref[0] = scores(0)

        def body(t, carry):
            s_ref[1] = scores(2 * t + 1)
            accumulate(2 * t, 0)
            s_ref[0] = scores(2 * t + 2)
            accumulate(2 * t + 1, 1)
            return carry

        lax.fori_loop(0, n_chunks // 2 - 1, body, 0)
        s_ref[1] = scores(n_chunks - 1)
        accumulate(n_chunks - 2, 0)
        accumulate(n_chunks - 1, 1)

    out_t = acc_ref[:HEAD_DIM, :] / acc_ref[HEAD_DIM:HEAD_DIM + 1, :]
    for h in range(GQA_GROUP):
        o_ref[:, h * HEAD_DIM:(h + 1) * HEAD_DIM] = out_t[:, h * tq:(h + 1) * tq].T.astype(o_ref.dtype)


def _gqa_attention(qk, proj, plain_exp, batch, seq):
    t = qk.shape[0]
    tq = _pick(seq, (512, 256, 128))
    tk = _pick(seq, (512, 256, 128))
    gw = GQA_GROUP * HEAD_DIM
    k_col0 = GQA_Q_WIDTH // HEAD_DIM
    v_col0 = OFF_GQA_V // HEAD_DIM
    q_blocks = seq // tq
    grid_spec = pltpu.PrefetchScalarGridSpec(
        num_scalar_prefetch=1,
        grid=(batch, GQA_KV_HEADS, q_blocks),
        in_specs=[
            pl.BlockSpec((tq, gw), lambda b, g, i, flag: (b * q_blocks + i, g)),
            pl.BlockSpec((seq, HEAD_DIM), lambda b, g, i, flag: (b, k_col0 + g)),
            pl.BlockSpec((seq, HEAD_DIM), lambda b, g, i, flag: (b, v_col0 + g)),
        ],
        out_specs=pl.BlockSpec((tq, gw), lambda b, g, i, flag: (b * q_blocks + i, g)),
        scratch_shapes=[
            pltpu.VMEM((GQA_GROUP * tq, HEAD_DIM), BF16),
            pltpu.VMEM((seq // tk, HEAD_DIM + GQA_ONES_ROWS, tk), BF16),
            pltpu.VMEM((HEAD_DIM + GQA_ONES_ROWS, GQA_GROUP * tq), F32),
            pltpu.VMEM((2, tk, GQA_GROUP * tq), F32),
            pltpu.VMEM((1, GQA_GROUP * tq), F32),
        ],
    )
    return pl.pallas_call(
        functools.partial(_gqa_kernel, tk=tk),
        out_shape=jax.ShapeDtypeStruct((t, GQA_Q_WIDTH), BF16),
        grid_spec=grid_spec,
        compiler_params=_params(("parallel", "parallel", "arbitrary")),
        name="gqa_flash",
    )(plain_exp, qk, qk, proj)


NA_PAIR_ROWS = 2 * NA_WIN_ROWS
NA_MASK_BOTH, NA_MASK_LEFT, NA_MASK_RIGHT = 0, 1, 2


def _na_block_plan(rows):
    gr, br = NA_GROUP_ROWS, NA_BAND_ROWS
    plan = []
    for r0 in (0, min(gr, rows - gr), rows - gr):
        band_start = int(np.clip(r0 - NA_WIN_ROWS // 2, 0, rows - br))
        per_row = []
        for i in range(gr):
            r = r0 + i
            row_start = int(np.clip(r - NA_WIN_ROWS // 2, 0, rows - NA_WIN_ROWS))
            blocks = []
            for jp in range(br // 2):
                key_row = band_start + 2 * jp
                ok_l = row_start <= key_row < row_start + NA_WIN_ROWS
                ok_r = row_start <= key_row + 1 < row_start + NA_WIN_ROWS
                if not (ok_l or ok_r):
                    blocks.append(None)
                    continue
                kind = NA_MASK_BOTH if (ok_l and ok_r) else (NA_MASK_LEFT if ok_l else NA_MASK_RIGHT)
                blocks.append((key_row - r + NA_WIN_ROWS, kind))
            per_row.append(blocks)
        plan.append(per_row)
    return plan


def _na_pair_table(rpb):
    h, nr, nc = rpb.shape
    rpb = jnp.pad(rpb.astype(F32), ((0, 0), (0, 0), (0, GRID_W - nc)))
    zero = jnp.zeros((h, 1, GRID_W), F32)
    left = jnp.concatenate([zero, rpb], axis=1)
    right = jnp.concatenate([rpb, zero], axis=1)
    return jnp.concatenate([left, right], axis=2)


def _na_col_masks():
    w = np.arange(GRID_W)[:, None]
    c = np.arange(2 * GRID_W)[None, :] % GRID_W
    col_start = np.clip(w - NA_WIN_COLS // 2, 0, GRID_W - NA_WIN_COLS)
    base = np.where((c >= col_start) & (c < col_start + NA_WIN_COLS), 0.0, MASK_VALUE)
    left_half = np.arange(2 * GRID_W)[None, :] < GRID_W
    masks = np.stack([base, np.where(left_half, base, MASK_VALUE), np.where(left_half, MASK_VALUE, base)])
    return jnp.asarray(masks, F32)


def _na_kernel(q_ref, k_ref, v_ref, pair_ref, mask_ref, o_ref, tile_ref, table_ref, *, rows):
    g = pl.program_id(2)
    n_groups = rows // NA_GROUP_ROWS
    blk_w = 2 * GRID_W

    @pl.when(g == 0)
    def _():
        for p in range(NA_PAIR_ROWS):
            row = jnp.broadcast_to(pair_ref[p:p + 1, :], (GRID_W, blk_w))
            tile_ref[p] = pltpu.roll(row, blk_w - (NA_WIN_COLS - 1), 1, stride=1, stride_axis=0)
        for v, per_row in enumerate(_na_block_plan(rows)):
            for i, blocks in enumerate(per_row):
                for jp, blk in enumerate(blocks):
                    dst = (v, slice(i * GRID_W, (i + 1) * GRID_W), slice(jp * blk_w, (jp + 1) * blk_w))
                    if blk is None:
                        table_ref[dst] = jnp.full((GRID_W, blk_w), MASK_VALUE, F32)
                    else:
                        table_ref[dst] = (tile_ref[blk[0]] + mask_ref[blk[1]]) * LOG2E

    gq = NA_GROUP_ROWS * GRID_W
    band_rows = NA_BAND_ROWS * GRID_W
    for u in range(q_ref.shape[0] // gq):
        gi = g * (q_ref.shape[0] // gq) + u
        start_row = jnp.clip(gi * NA_GROUP_ROWS - NA_WIN_ROWS // 2, 0, rows - NA_BAND_ROWS)
        band = pl.ds(pl.multiple_of(start_row * GRID_W, GRID_W), band_rows)
        variant = jnp.where(gi == 0, 0, jnp.where(gi == n_groups - 1, 2, 1))
        s = lax.dot_general(q_ref[u * gq:(u + 1) * gq, :], k_ref[band, :], (((1,), (1,)), ((), ())),
                            preferred_element_type=F32)
        s = s * (HEAD_DIM ** -0.5 * LOG2E) + table_ref[variant]
        p = jnp.exp2(s - jnp.max(s, axis=-1, keepdims=True))
        l = jnp.sum(p, axis=-1, keepdims=True)
        o = jnp.dot(p.astype(BF16), v_ref[band, :], preferred_element_type=F32)
        o_ref[u * gq:(u + 1) * gq, :] = (o / l).astype(o_ref.dtype)


def _na_attention(proj, rpb, batch, seq):
    t = proj.shape[0]
    rows = seq // GRID_W
    assert NA_GROUP_ROWS % 4 == 0 and rows % NA_GROUP_ROWS == 0 and rows >= NA_BAND_ROWS
    assert 2 * GRID_W == HEAD_DIM
    n_groups = rows // NA_GROUP_ROWS
    gq = NA_GROUP_ROWS * GRID_W
    gk = NA_BAND_ROWS * GRID_W
    per_step = _pick(n_groups, (NA_GROUPS_PER_STEP, 2, 1))
    n_steps = n_groups // per_step
    k_col0 = OFF_NA_K // HEAD_DIM
    v_col0 = OFF_NA_V // HEAD_DIM
    return pl.pallas_call(
        functools.partial(_na_kernel, rows=rows),
        out_shape=jax.ShapeDtypeStruct((t, NA_WIDTH), BF16),
        grid=(batch, NA_HEADS, n_steps),
        in_specs=[
            pl.BlockSpec((per_step * gq, HEAD_DIM), lambda b, h, g: (b * n_steps + g, h)),
            pl.BlockSpec((seq, HEAD_DIM), lambda b, h, g: (b, k_col0 + h)),
            pl.BlockSpec((seq, HEAD_DIM), lambda b, h, g: (b, v_col0 + h)),
            pl.BlockSpec((None, NA_PAIR_ROWS, 2 * GRID_W), lambda b, h, g: (h, 0, 0)),
            pl.BlockSpec((3, GRID_W, 2 * GRID_W), lambda b, h, g: (0, 0, 0)),
        ],
        out_specs=pl.BlockSpec((per_step * gq, HEAD_DIM), lambda b, h, g: (b * n_steps + g, h)),
        scratch_shapes=[
            pltpu.VMEM((NA_PAIR_ROWS, GRID_W, 2 * GRID_W), F32),
            pltpu.VMEM((3, gq, gk), F32),
        ],
        compiler_params=_params(("parallel", "parallel", "arbitrary")),
        name="na_attention",
    )(proj, proj, proj, _na_pair_table(rpb), _na_col_masks())


CONV_HALO = 16


def _branch_merge_kernel(a_na_ref, a_gqa_ref, h_ref, b_ref, c_ref, hp_ref, cp_ref, hn_ref, cn_ref,
                         g_na_ref, g_gqa_ref, g_conv_ref, cw_ref, cb_ref, w_na_ref, w_gqa_ref,
                         w_conv_ref, o_ref, a_conv_ref, part_ref, *, blocks_per_seq, col_chunk):
    tm = h_ref.shape[0]
    pos = pl.program_id(0) % blocks_per_seq
    n_chunks = o_ref.shape[1] // col_chunk

    def branch(a_ref, w_ref, gate_ref, cols):
        y = jnp.dot(a_ref[...], w_ref[:, cols], preferred_element_type=F32)
        return _sigmoid(gate_ref[:, cols].astype(F32)) * y

    for c in range(n_chunks):
        cols = slice(c * col_chunk, (c + 1) * col_chunk)
        part_ref[:, cols] = (branch(a_na_ref, w_na_ref, g_na_ref, cols)
                             + branch(a_gqa_ref, w_gqa_ref, g_gqa_ref, cols))

    u = c_ref[...].astype(F32) * h_ref[...].astype(F32)
    u_prev_row = cp_ref[CONV_HALO - 1:CONV_HALO, :].astype(F32) * hp_ref[CONV_HALO - 1:CONV_HALO, :].astype(F32)
    u_prev_row = jnp.where(pos == 0, 0.0, u_prev_row)
    u_next_row = cn_ref[0:1, :].astype(F32) * hn_ref[0:1, :].astype(F32)
    u_next_row = jnp.where(pos == blocks_per_seq - 1, 0.0, u_next_row)
    row = lax.broadcasted_iota(jnp.int32, u.shape, 0)
    u_prev = jnp.where(row == 0, u_prev_row, pltpu.roll(u, 1, 0))
    u_next = jnp.where(row == tm - 1, u_next_row, pltpu.roll(u, tm - 1, 0))
    y = cb_ref[...] + u_prev * cw_ref[0:1, :] + u * cw_ref[1:2, :] + u_next * cw_ref[2:3, :]
    a_conv_ref[...] = (b_ref[...].astype(F32) * y).astype(BF16)

    for c in range(n_chunks):
        cols = slice(c * col_chunk, (c + 1) * col_chunk)
        merged = part_ref[:, cols] + branch(a_conv_ref, w_conv_ref, g_conv_ref, cols)
        o_ref[:, cols] = merged.astype(o_ref.dtype)


def _branch_merge(a_na, a_gqa, p_conv, p_gate, conv_w, conv_b, w_na, w_gqa, w_conv, layer, seq):
    t = a_na.shape[0]
    ch = conv_w.shape[2]
    d = w_na.shape[2]
    assert CONV_WIDTH == 3
    tm = _pick(seq, (512, 256))
    per_halo = tm // CONV_HALO
    last_halo = t // CONV_HALO - 1
    prev_map = lambda col: (lambda i: (jnp.maximum(i * per_halo - 1, 0), col))
    next_map = lambda col: (lambda i: (jnp.minimum((i + 1) * per_halo, last_halo), col))
    resident = lambda shape: pl.BlockSpec((None,) + shape, lambda i: (layer, 0, 0),
                                          pipeline_mode=pl.Buffered(1))
    return pl.pallas_call(
        functools.partial(_branch_merge_kernel, blocks_per_seq=seq // tm,
                          col_chunk=_pick(d, (512, 256, 128))),
        out_shape=jax.ShapeDtypeStruct((t, d), BF16),
        grid=(t // tm,),
        in_specs=[
            pl.BlockSpec((tm, NA_WIDTH), lambda i: (i, 0)),
            pl.BlockSpec((tm, GQA_Q_WIDTH), lambda i: (i, 0)),
            pl.BlockSpec((tm, ch), lambda i: (i, 0)),
            pl.BlockSpec((tm, ch), lambda i: (i, 1)),
            pl.BlockSpec((tm, ch), lambda i: (i, 2)),
            pl.BlockSpec((CONV_HALO, ch), prev_map(0)),
            pl.BlockSpec((CONV_HALO, ch), prev_map(2)),
            pl.BlockSpec((CONV_HALO, ch), next_map(0)),
            pl.BlockSpec((CONV_HALO, ch), next_map(2)),
            pl.BlockSpec((tm, d), lambda i: (i, 0)),
            pl.BlockSpec((tm, d), lambda i: (i, 1)),
            pl.BlockSpec((tm, d), lambda i: (i, 2)),
            pl.BlockSpec((None, CONV_WIDTH, ch), lambda i: (layer, 0, 0)),
            pl.BlockSpec((None, 1, ch), lambda i: (layer, 0, 0)),
            resident((NA_WIDTH, d)),
            resident((GQA_Q_WIDTH, d)),
            resident((ch, d)),
        ],
        out_specs=pl.BlockSpec((tm, d), lambda i: (i, 0)),
        scratch_shapes=[pltpu.VMEM((tm, ch), BF16), pltpu.VMEM((tm, d), F32)],
        compiler_params=_params(("parallel",)),
        name="branch_merge",
    )(a_na, a_gqa, p_conv, p_conv, p_conv, p_conv, p_conv, p_conv, p_conv, p_gate, p_gate, p_gate,
      conv_w, conv_b, w_na, w_gqa, w_conv)


def _out_proj_kernel(x_ref, m_ref, w_ref, g_post_ref, g_next_ref, o_ref, h_ref):
    half = x_ref.shape[0] // 2
    for r in range(2):
        rows = slice(r * half, (r + 1) * half)
        y = jnp.dot(m_ref[rows, :], w_ref[...], preferred_element_type=F32)
        x_new = x_ref[rows, :] + _rms_rows(y, g_post_ref[...])
        o_ref[rows, :] = x_new
        h_ref[rows, :] = _rms_rows(x_new, g_next_ref[...]).astype(h_ref.dtype)


def _out_proj(x, merged, w_out, post_gains, next_gains, layer):
    t, d = x.shape
    tm = _pick(t, (512, 256))
    gain = lambda: pl.BlockSpec((None, 1, d), lambda i: (layer, 0, 0))
    return pl.pallas_call(
        _out_proj_kernel,
        out_shape=(jax.ShapeDtypeStruct((t, d), F32), jax.ShapeDtypeStruct((t, d), BF16)),
        grid=(t // tm,),
        in_specs=[
            pl.BlockSpec((tm, d), lambda i: (i, 0)),
            pl.BlockSpec((tm, d), lambda i: (i, 0)),
            pl.BlockSpec((None, d, d), lambda i: (layer, 0, 0), pipeline_mode=pl.Buffered(1)),
            gain(),
            gain(),
        ],
        out_specs=(pl.BlockSpec((tm, d), lambda i: (i, 0)), pl.BlockSpec((tm, d), lambda i: (i, 0))),
        compiler_params=_params(("parallel",)),
        name="out_proj_norm",
    )(x, merged, w_out, post_gains, next_gains)


def _ffn_kernel(x_ref, h_ref, wg_ref, wu_ref, wd_ref, g_post_ref, g_next_ref, o_ref, hn_ref, acc_ref):
    f = pl.program_id(1)

    @pl.when(f == 0)
    def _():
        acc_ref[...] = jnp.zeros_like(acc_ref)

    h = h_ref[...]
    a = jnp.dot(h, wg_ref[...], preferred_element_type=F32)
    b = jnp.dot(h, wu_ref[...], preferred_element_type=F32)
    act = (a * _sigmoid(a) * b).astype(BF16)
    acc_ref[...] += jnp.dot(act, wd_ref[...], preferred_element_type=F32)

    @pl.when(f == pl.num_programs(1) - 1)
    def _():
        x_new = x_ref[...] + _rms_rows(acc_ref[...], g_post_ref[...])
        o_ref[...] = x_new
        hn_ref[...] = _rms_rows(x_new, g_next_ref[...]).astype(hn_ref.dtype)


def _ffn(x, h, wg, wu, wd, post_gains, next_gains, layer, next_layer):
    t, d = x.shape
    hidden = wg.shape[2]
    tm = _pick(t, (512, 256))
    tf = _pick(hidden, (512, 256, 128))
    return pl.pallas_call(
        _ffn_kernel,
        out_shape=(jax.ShapeDtypeStruct((t, d), F32), jax.ShapeDtypeStruct((t, d), BF16)),
        grid=(t // tm, hidden // tf),
        in_specs=[
            pl.BlockSpec((tm, d), lambda i, f: (i, 0)),
            pl.BlockSpec((tm, d), lambda i, f: (i, 0)),
            pl.BlockSpec((None, d, tf), lambda i, f: (layer, 0, f)),
            pl.BlockSpec((None, d, tf), lambda i, f: (layer, 0, f)),
            pl.BlockSpec((None, tf, d), lambda i, f: (layer, f, 0)),
            pl.BlockSpec((None, 1, d), lambda i, f: (layer, 0, 0)),
            pl.BlockSpec((None, 1, d), lambda i, f: (next_layer, 0, 0)),
        ],
        out_specs=(pl.BlockSpec((tm, d), lambda i, f: (i, 0)), pl.BlockSpec((tm, d), lambda i, f: (i, 0))),
        scratch_shapes=[pltpu.VMEM((tm, d), F32)],
        compiler_params=_params(("parallel", "arbitrary")),
        name="ffn_swiglu",
    )(x, h, wg, wu, wd, post_gains, next_gains)


def kernel(x, w_in, na_rpb, q_norm_g, k_norm_g, conv_w, conv_b, w_br_na, w_br_gqa, w_br_conv, w_out,
           pre_mix_g, post_mix_g, pre_ffn_g, post_ffn_g, w_ffn_gate, w_ffn_up, w_ffn_down):
    batch, seq, d = x.shape
    depth = w_in.shape[0]
    ch = conv_w.shape[2]
    cos, sin = _rope_tables(seq)

    gain3 = lambda g: g.astype(F32).reshape(depth, 1, -1)
    pre_mix_g, post_mix_g, pre_ffn_g, post_ffn_g = map(gain3, (pre_mix_g, post_mix_g, pre_ffn_g, post_ffn_g))
    conv_w = conv_w.astype(F32)
    conv_b = conv_b.astype(F32).reshape(depth, 1, ch)
    w_br_na, w_br_gqa, w_br_conv, w_out = (w.astype(BF16) for w in (w_br_na, w_br_gqa, w_br_conv, w_out))
    w_ffn_gate, w_ffn_up, w_ffn_down = (w.astype(BF16) for w in (w_ffn_gate, w_ffn_up, w_ffn_down))
    q_gain = q_norm_g.astype(F32) * (HEAD_DIM ** -0.5 * LOG2E)
    qk_gains = jnp.concatenate([jnp.tile(q_gain, (1, GQA_Q_HEADS)),
                                jnp.tile(k_norm_g.astype(F32), (1, GQA_KV_HEADS))], axis=1)
    qk_gains = qk_gains.reshape(depth, -1, 1, 2 * HEAD_DIM)

    y = x.reshape(batch * seq, d)
    h = _pre_norm(y, pre_mix_g, 0)
    for l in range(depth):
        p_attn = _in_proj(h, w_in, l, 0, OFF_CONV, "in_proj_attn")
        p_conv = _in_proj(h, w_in, l, OFF_CONV, 3 * ch, "in_proj_conv")
        p_gate = _in_proj(h, w_in, l, OFF_CONV + 3 * ch, 3 * d, "in_proj_gate")

        qk = _qk_prep(p_attn, qk_gains[l], cos, sin, seq)
        plain_exp = (_gqa_logit_bound(q_gain[l], k_norm_g[l].astype(F32)) <= GQA_PLAIN_EXP_BOUND)
        a_gqa = _gqa_attention(qk, p_attn, plain_exp.astype(jnp.int32).reshape(1), batch, seq)
        a_na = _na_attention(p_attn, na_rpb[l], batch, seq)

        merged = _branch_merge(a_na, a_gqa, p_conv, p_gate, conv_w, conv_b, w_br_na, w_br_gqa,
                               w_br_conv, l, seq)
        y, h = _out_proj(y, merged, w_out, post_mix_g, pre_ffn_g, l)
        y, h = _ffn(y, h, w_ffn_gate, w_ffn_up, w_ffn_down, post_ffn_g, pre_mix_g, l, (l + 1) % depth)
    return y.reshape(batch, seq, d)
```

```python
import functools
import math

import numpy as np
import jax
import jax.numpy as jnp
from jax import lax
from jax.experimental import pallas as pl
from jax.experimental.pallas import tpu as pltpu

GRID_W = 64
HEAD_DIM = 128
NA_HEADS = 8
NA_WIN_ROWS = 8
NA_WIN_COLS = 16
GQA_Q_HEADS = 8
GQA_KV_HEADS = 2
GQA_GROUP = GQA_Q_HEADS // GQA_KV_HEADS
ROPE_THETA = 10000.0
CONV_WIDTH = 3
RMS_EPS = 1e-6

NA_WIDTH = NA_HEADS * HEAD_DIM
GQA_Q_WIDTH = GQA_Q_HEADS * HEAD_DIM
GQA_KV_WIDTH = GQA_KV_HEADS * HEAD_DIM
OFF_NA_Q = 0
OFF_NA_K = NA_WIDTH
OFF_NA_V = 2 * NA_WIDTH
OFF_GQA_Q = 3 * NA_WIDTH
OFF_GQA_K = OFF_GQA_Q + GQA_Q_WIDTH
OFF_GQA_V = OFF_GQA_K + GQA_KV_WIDTH
OFF_CONV = OFF_GQA_V + GQA_KV_WIDTH

NA_GROUP_ROWS = 4
NA_BAND_ROWS = NA_GROUP_ROWS + NA_WIN_ROWS
NA_GROUPS_PER_STEP = 8
OUT_PROJ_ROW_PARTS = 4
IN_PROJ_COLS = 1536
MASK_VALUE = -1e30
LOG2E = math.log2(math.e)

V7X_VMEM_BYTES = 64 * 1024 * 1024
VMEM_LIMIT = 56 * 1024 * 1024

BF16 = jnp.bfloat16
F32 = jnp.float32


def _params(semantics):
    return pltpu.CompilerParams(dimension_semantics=semantics, vmem_limit_bytes=VMEM_LIMIT)


def _pick(n, candidates):
    for c in candidates:
        if n % c == 0:
            return c
    raise ValueError(f"no tile in {candidates} divides {n}")


def _sigmoid(x):
    return 0.5 * jnp.tanh(0.5 * x) + 0.5


def _rms_rows(x, g):
    ms = jnp.mean(x * x, axis=-1, keepdims=True)
    return x * lax.rsqrt(ms + RMS_EPS) * g


def _norm_kernel(x_ref, g_ref, o_ref):
    o_ref[...] = _rms_rows(x_ref[...], g_ref[...]).astype(o_ref.dtype)


def _pre_norm(x, gains, layer):
    t, d = x.shape
    tm = _pick(t, (512, 256))
    return pl.pallas_call(
        _norm_kernel,
        out_shape=jax.ShapeDtypeStruct((t, d), BF16),
        grid=(t // tm,),
        in_specs=[
            pl.BlockSpec((tm, d), lambda i: (i, 0)),
            pl.BlockSpec((None, 1, d), lambda i: (layer, 0, 0)),
        ],
        out_specs=pl.BlockSpec((tm, d), lambda i: (i, 0)),
        compiler_params=_params(("parallel",)),
        name="pre_norm",
    )(x, gains)


def _proj_kernel(h_ref, w_ref, *refs, row_chunk, n_side):
    side_in, o_ref, side_out, wb_ref = refs[:n_side], refs[n_side], refs[n_side + 1:-1], refs[-1]

    @pl.when(pl.program_id(1) == 0)
    def _():
        def body(c, carry):
            rows = pl.ds(pl.multiple_of(c * row_chunk, row_chunk), row_chunk)
            wb_ref[rows, :] = w_ref[rows, :].astype(BF16)
            return carry
        lax.fori_loop(0, w_ref.shape[0] // row_chunk, body, 0)

    o_ref[...] = jnp.dot(h_ref[...], wb_ref[...], preferred_element_type=F32).astype(o_ref.dtype)
    for src, dst in zip(side_in, side_out):
        dst[...] = src[...].astype(dst.dtype)


def _in_proj(h, w_in, layer, col_off, n_cols, name, side=()):
    t, d = h.shape
    tm = _pick(t, (1024, 512, 256))
    tn = IN_PROJ_COLS
    assert col_off % tn == 0 and n_cols % tn == 0
    col0 = col_off // tn
    n_i = t // tm
    n_steps = (n_cols // tn) * n_i
    side_specs, side_out_specs, side_shapes = [], [], []
    for w in side:
        rows, cols = w.shape[1:]
        slab = rows // n_steps
        assert slab * n_steps == rows and slab % 16 == 0, (w.shape, n_steps)
        side_specs.append(pl.BlockSpec((None, slab, cols), lambda j, i: (layer, j * n_i + i, 0)))
        side_out_specs.append(pl.BlockSpec((slab, cols), lambda j, i: (j * n_i + i, 0)))
        side_shapes.append(jax.ShapeDtypeStruct((rows, cols), BF16))
    outs = pl.pallas_call(
        functools.partial(_proj_kernel, row_chunk=min(d, 256), n_side=len(side)),
        out_shape=[jax.ShapeDtypeStruct((t, n_cols), BF16)] + side_shapes,
        grid=(n_cols // tn, n_i),
        in_specs=[
            pl.BlockSpec((tm, d), lambda j, i: (i, 0)),
            pl.BlockSpec((None, d, tn), lambda j, i: (layer, 0, col0 + j)),
        ] + side_specs,
        out_specs=[pl.BlockSpec((tm, tn), lambda j, i: (i, j))] + side_out_specs,
        scratch_shapes=[pltpu.VMEM((d, tn), BF16)],
        compiler_params=_params(("parallel", "arbitrary")),
        name=name,
    )(h, w_in, *side)
    return outs[0] if not side else outs


def _rope_tables(seq):
    half = HEAD_DIM // 2
    quarter = half // 2
    inv_freq = 1.0 / (ROPE_THETA ** (np.arange(quarter, dtype=np.float64) / quarter))
    t = np.arange(seq)
    ang_r = (t // GRID_W)[:, None] * inv_freq[None, :]
    ang_c = (t % GRID_W)[:, None] * inv_freq[None, :]
    cos = np.concatenate([np.cos(ang_r), np.cos(ang_r), np.cos(ang_c), np.cos(ang_c)], axis=1)
    sin = np.concatenate([-np.sin(ang_r), np.sin(ang_r), -np.sin(ang_c), np.sin(ang_c)], axis=1)
    return jnp.asarray(cos, F32), jnp.asarray(sin, F32)


def _qk_prep_kernel(x_ref, g_ref, cos_ref, sin_ref, o_ref):
    heads = x_ref.shape[1] // HEAD_DIM
    quarter = HEAD_DIM // 4
    cos = cos_ref[...]
    sin = sin_ref[...]
    lane = lax.broadcasted_iota(jnp.int32, cos.shape, 1)
    first = (lane % (2 * quarter)) < quarter
    for h in range(heads):
        cols = slice(h * HEAD_DIM, (h + 1) * HEAD_DIM)
        y = _rms_rows(x_ref[:, cols].astype(F32), g_ref[0, :, cols])
        partner = jnp.where(first, pltpu.roll(y, HEAD_DIM - quarter, 1), pltpu.roll(y, quarter, 1))
        o_ref[:, cols] = (y * cos + partner * sin).astype(o_ref.dtype)


def _qk_prep(proj, gains, cos, sin, seq):
    t = proj.shape[0]
    n_heads = GQA_Q_HEADS + GQA_KV_HEADS
    width = 2 * HEAD_DIM
    assert OFF_GQA_Q % width == 0 and n_heads % 2 == 0
    tm = _pick(seq, (2048, 1024, 512, 256))
    blocks_per_seq = seq // tm
    col0 = OFF_GQA_Q // width
    return pl.pallas_call(
        _qk_prep_kernel,
        out_shape=jax.ShapeDtypeStruct((t, n_heads * HEAD_DIM), BF16),
        grid=(t // tm, n_heads // 2),
        in_specs=[
            pl.BlockSpec((tm, width), lambda i, j: (i, col0 + j)),
            pl.BlockSpec((1, 1, width), lambda i, j: (j, 0, 0)),
            pl.BlockSpec((tm, HEAD_DIM), lambda i, j: (i % blocks_per_seq, 0)),
            pl.BlockSpec((tm, HEAD_DIM), lambda i, j: (i % blocks_per_seq, 0)),
        ],
        out_specs=pl.BlockSpec((tm, width), lambda i, j: (i, j)),
        compiler_params=_params(("parallel", "parallel")),
        name="gqa_qk_prep",
    )(proj, gains, cos, sin)


GQA_ONES_ROWS = 16
GQA_PLAIN_EXP_BOUND = 64.0


def _gqa_logit_bound(q_gain, k_gain):
    return 1.02 * HEAD_DIM * jnp.max(jnp.abs(q_gain)) * jnp.max(jnp.abs(k_gain))


def _gqa_kernel(plain_ref, q_ref, k_ref, v_ref, o_ref, qs_ref, vt_ref, acc_ref, s_ref, m_ref, *, tk):
    tq = q_ref.shape[0]
    seq = k_ref.shape[0]
    n_chunks = seq // tk
    assert n_chunks % 2 == 0

    def chunk(c):
        return pl.ds(pl.multiple_of(c * tk, tk), tk)

    @pl.when(pl.program_id(2) == 0)
    def _():
        def body(c, carry):
            vt_ref[c, :HEAD_DIM, :] = v_ref[chunk(c), :].astype(F32).T.astype(BF16)
            vt_ref[c, HEAD_DIM:, :] = jnp.ones((GQA_ONES_ROWS, tk), BF16)
            return carry
        lax.fori_loop(0, n_chunks, body, 0)

    for h in range(GQA_GROUP):
        qs_ref[h * tq:(h + 1) * tq, :] = q_ref[:, h * HEAD_DIM:(h + 1) * HEAD_DIM]
    acc_ref[...] = jnp.zeros_like(acc_ref)

    def scores(c):
        return lax.dot_general(k_ref[chunk(c), :], qs_ref[...], (((1,), (1,)), ((), ())),
                               preferred_element_type=F32)

    @pl.when(plain_ref[0] != 0)
    def _():
        per_trip = 4 if n_chunks % 4 == 0 else 2

        def body(t, carry):
            pv = None
            for u in range(per_trip):
                c = per_trip * t + u
                p = jnp.exp2(scores(c)).astype(BF16)
                part = jnp.dot(vt_ref[c], p, preferred_element_type=F32)
                pv = part if pv is None else pv + part
            acc_ref[...] += pv
            return carry
        lax.fori_loop(0, n_chunks // per_trip, body, 0)

    @pl.when(plain_ref[0] == 0)
    def _():
        m_ref[...] = jnp.full_like(m_ref, -jnp.inf)

        def accumulate(c, slot):
            s = s_ref[slot]
            m_old = m_ref[...]
            m_new = jnp.maximum(m_old, jnp.max(s, axis=0, keepdims=True))
            m_ref[...] = m_new
            p = jnp.exp2(s - m_new).astype(BF16)
            pv = jnp.dot(vt_ref[c], p, preferred_element_type=F32)
            acc_ref[...] = acc_ref[...] * jnp.exp2(m_old - m_new) + pv

        s_ref[0] = scores(0)

        def body(t, carry):
            s_ref[1] = scores(2 * t + 1)
            accumulate(2 * t, 0)
            s_ref[0] = scores(2 * t + 2)
            accumulate(2 * t + 1, 1)
            return carry

        lax.fori_loop(0, n_chunks // 2 - 1, body, 0)
        s_ref[1] = scores(n_chunks - 1)
        accumulate(n_chunks - 2, 0)
        accumulate(n_chunks - 1, 1)

    out_t = acc_ref[:HEAD_DIM, :] / acc_ref[HEAD_DIM:HEAD_DIM + 1, :]
    for h in range(GQA_GROUP):
        o_ref[:, h * HEAD_DIM:(h + 1) * HEAD_DIM] = out_t[:, h * tq:(h + 1) * tq].T.astype(o_ref.dtype)


def _gqa_attention(qk, proj, plain_exp, batch, seq):
    t = qk.shape[0]
    tq = _pick(seq, (512, 256, 128))
    tk = _pick(seq, (512, 256, 128))
    gw = GQA_GROUP * HEAD_DIM
    k_col0 = GQA_Q_WIDTH // HEAD_DIM
    v_col0 = OFF_GQA_V // HEAD_DIM
    q_blocks = seq // tq
    grid_spec = pltpu.PrefetchScalarGridSpec(
        num_scalar_prefetch=1,
        grid=(batch, GQA_KV_HEADS, q_blocks),
        in_specs=[
            pl.BlockSpec((tq, gw), lambda b, g, i, flag: (b * q_blocks + i, g)),
            pl.BlockSpec((seq, HEAD_DIM), lambda b, g, i, flag: (b, k_col0 + g)),
            pl.BlockSpec((seq, HEAD_DIM), lambda b, g, i, flag: (b, v_col0 + g)),
        ],
        out_specs=pl.BlockSpec((tq, gw), lambda b, g, i, flag: (b * q_blocks + i, g)),
        scratch_shapes=[
            pltpu.VMEM((GQA_GROUP * tq, HEAD_DIM), BF16),
            pltpu.VMEM((seq // tk, HEAD_DIM + GQA_ONES_ROWS, tk), BF16),
            pltpu.VMEM((HEAD_DIM + GQA_ONES_ROWS, GQA_GROUP * tq), F32),
            pltpu.VMEM((2, tk, GQA_GROUP * tq), F32),
            pltpu.VMEM((1, GQA_GROUP * tq), F32),
        ],
    )
    return pl.pallas_call(
        functools.partial(_gqa_kernel, tk=tk),
        out_shape=jax.ShapeDtypeStruct((t, GQA_Q_WIDTH), BF16),
        grid_spec=grid_spec,
        compiler_params=_params(("parallel", "parallel", "arbitrary")),
        name="gqa_flash",
    )(plain_exp, qk, qk, proj)


NA_PAIR_ROWS = 2 * NA_WIN_ROWS
NA_MASK_BOTH, NA_MASK_LEFT, NA_MASK_RIGHT = 0, 1, 2


def _na_block_plan(rows):
    gr, br = NA_GROUP_ROWS, NA_BAND_ROWS
    plan = []
    for r0 in (0, min(gr, rows - gr), rows - gr):
        band_start = int(np.clip(r0 - NA_WIN_ROWS // 2, 0, rows - br))
        per_row = []
        for i in range(gr):
            r = r0 + i
            row_start = int(np.clip(r - NA_WIN_ROWS // 2, 0, rows - NA_WIN_ROWS))
            blocks = []
            for jp in range(br // 2):
                key_row = band_start + 2 * jp
                ok_l = row_start <= key_row < row_start + NA_WIN_ROWS
                ok_r = row_start <= key_row + 1 < row_start + NA_WIN_ROWS
                if not (ok_l or ok_r):
                    blocks.append(None)
                    continue
                kind = NA_MASK_BOTH if (ok_l and ok_r) else (NA_MASK_LEFT if ok_l else NA_MASK_RIGHT)
                blocks.append((key_row - r + NA_WIN_ROWS, kind))
            per_row.append(blocks)
        plan.append(per_row)
    return plan


def _na_pair_table(rpb):
    h, nr, nc = rpb.shape
    rpb = jnp.pad(rpb.astype(F32), ((0, 0), (0, 0), (0, GRID_W - nc)))
    zero = jnp.zeros((h, 1, GRID_W), F32)
    left = jnp.concatenate([zero, rpb], axis=1)
    right = jnp.concatenate([rpb, zero], axis=1)
    return jnp.concatenate([left, right], axis=2)


def _na_col_masks():
    w = np.arange(GRID_W)[:, None]
    c = np.arange(2 * GRID_W)[None, :] % GRID_W
    col_start = np.clip(w - NA_WIN_COLS // 2, 0, GRID_W - NA_WIN_COLS)
    base = np.where((c >= col_start) & (c < col_start + NA_WIN_COLS), 0.0, MASK_VALUE)
    left_half = np.arange(2 * GRID_W)[None, :] < GRID_W
    masks = np.stack([base, np.where(left_half, base, MASK_VALUE), np.where(left_half, MASK_VALUE, base)])
    return jnp.asarray(masks, F32)


def _na_kernel(q_ref, k_ref, v_ref, pair_ref, mask_ref, o_ref, tile_ref, table_ref, *, rows):
    g = pl.program_id(2)
    n_groups = rows // NA_GROUP_ROWS
    blk_w = 2 * GRID_W

    @pl.when(g == 0)
    def _():
        for p in range(NA_PAIR_ROWS):
            row = jnp.broadcast_to(pair_ref[p:p + 1, :], (GRID_W, blk_w))
            tile_ref[p] = pltpu.roll(row, blk_w - (NA_WIN_COLS - 1), 1, stride=1, stride_axis=0)
        for v, per_row in enumerate(_na_block_plan(rows)):
            for i, blocks in enumerate(per_row):
                for jp, blk in enumerate(blocks):
                    dst = (v, slice(i * GRID_W, (i + 1) * GRID_W), slice(jp * blk_w, (jp + 1) * blk_w))
                    if blk is None:
                        table_ref[dst] = jnp.full((GRID_W, blk_w), MASK_VALUE, F32)
                    else:
                        table_ref[dst] = (tile_ref[blk[0]] + mask_ref[blk[1]]) * LOG2E

    gq = NA_GROUP_ROWS * GRID_W
    band_rows = NA_BAND_ROWS * GRID_W
    for u in range(q_ref.shape[0] // gq):
        gi = g * (q_ref.shape[0] // gq) + u
        start_row = jnp.clip(gi * NA_GROUP_ROWS - NA_WIN_ROWS // 2, 0, rows - NA_BAND_ROWS)
        band = pl.ds(pl.multiple_of(start_row * GRID_W, GRID_W), band_rows)
        variant = jnp.where(gi == 0, 0, jnp.where(gi == n_groups - 1, 2, 1))
        s = lax.dot_general(q_ref[u * gq:(u + 1) * gq, :], k_ref[band, :], (((1,), (1,)), ((), ())),
                            preferred_element_type=F32)
        s = s * (HEAD_DIM ** -0.5 * LOG2E) + table_ref[variant]
        p = jnp.exp2(s - jnp.max(s, axis=-1, keepdims=True))
        l = jnp.sum(p, axis=-1, keepdims=True)
        o = jnp.dot(p.astype(BF16), v_ref[band, :], preferred_element_type=F32)
        o_ref[u * gq:(u + 1) * gq, :] = (o / l).astype(o_ref.dtype)


def _na_attention(proj, rpb, batch, seq):
    t = proj.shape[0]
    rows = seq // GRID_W
    assert NA_GROUP_ROWS % 4 == 0 and rows % NA_GROUP_ROWS == 0 and rows >= NA_BAND_ROWS
    assert 2 * GRID_W == HEAD_DIM
    n_groups = rows // NA_GROUP_ROWS
    gq = NA_GROUP_ROWS * GRID_W
    gk = NA_BAND_ROWS * GRID_W
    per_step = _pick(n_groups, (NA_GROUPS_PER_STEP, 2, 1))
    n_steps = n_groups // per_step
    k_col0 = OFF_NA_K // HEAD_DIM
    v_col0 = OFF_NA_V // HEAD_DIM
    return pl.pallas_call(
        functools.partial(_na_kernel, rows=rows),
        out_shape=jax.ShapeDtypeStruct((t, NA_WIDTH), BF16),
        grid=(batch, NA_HEADS, n_steps),
        in_specs=[
            pl.BlockSpec((per_step * gq, HEAD_DIM), lambda b, h, g: (b * n_steps + g, h)),
            pl.BlockSpec((seq, HEAD_DIM), lambda b, h, g: (b, k_col0 + h)),
            pl.BlockSpec((seq, HEAD_DIM), lambda b, h, g: (b, v_col0 + h)),
            pl.BlockSpec((None, NA_PAIR_ROWS, 2 * GRID_W), lambda b, h, g: (h, 0, 0)),
            pl.BlockSpec((3, GRID_W, 2 * GRID_W), lambda b, h, g: (0, 0, 0)),
        ],
        out_specs=pl.BlockSpec((per_step * gq, HEAD_DIM), lambda b, h, g: (b * n_steps + g, h)),
        scratch_shapes=[
            pltpu.VMEM((NA_PAIR_ROWS, GRID_W, 2 * GRID_W), F32),
            pltpu.VMEM((3, gq, gk), F32),
        ],
        compiler_params=_params(("parallel", "parallel", "arbitrary")),
        name="na_attention",
    )(proj, proj, proj, _na_pair_table(rpb), _na_col_masks())


CONV_HALO = 16


def _branch_merge_kernel(a_na_ref, a_gqa_ref, h_ref, b_ref, c_ref, hp_ref, cp_ref, hn_ref, cn_ref,
                         g_na_ref, g_gqa_ref, g_conv_ref, cw_ref, cb_ref, w_na_ref, w_gqa_ref,
                         w_conv_ref, o_ref, a_conv_ref, part_ref, *, blocks_per_seq, col_chunk):
    tm = h_ref.shape[0]
    pos = pl.program_id(0) % blocks_per_seq
    n_chunks = o_ref.shape[1] // col_chunk

    def branch(a_ref, w_ref, gate_ref, cols):
        y = jnp.dot(a_ref[...], w_ref[:, cols], preferred_element_type=F32)
        return _sigmoid(gate_ref[:, cols].astype(F32)) * y

    for c in range(n_chunks):
        cols = slice(c * col_chunk, (c + 1) * col_chunk)
        part_ref[:, cols] = (branch(a_na_ref, w_na_ref, g_na_ref, cols)
                             + branch(a_gqa_ref, w_gqa_ref, g_gqa_ref, cols))

    u = c_ref[...].astype(F32) * h_ref[...].astype(F32)
    u_prev_row = cp_ref[CONV_HALO - 1:CONV_HALO, :].astype(F32) * hp_ref[CONV_HALO - 1:CONV_HALO, :].astype(F32)
    u_prev_row = jnp.where(pos == 0, 0.0, u_prev_row)
    u_next_row = cn_ref[0:1, :].astype(F32) * hn_ref[0:1, :].astype(F32)
    u_next_row = jnp.where(pos == blocks_per_seq - 1, 0.0, u_next_row)
    row = lax.broadcasted_iota(jnp.int32, u.shape, 0)
    u_prev = jnp.where(row == 0, u_prev_row, pltpu.roll(u, 1, 0))
    u_next = jnp.where(row == tm - 1, u_next_row, pltpu.roll(u, tm - 1, 0))
    y = cb_ref[...] + u_prev * cw_ref[0:1, :] + u * cw_ref[1:2, :] + u_next * cw_ref[2:3, :]
    a_conv_ref[...] = (b_ref[...].astype(F32) * y).astype(BF16)

    for c in range(n_chunks):
        cols = slice(c * col_chunk, (c + 1) * col_chunk)
        merged = part_ref[:, cols] + branch(a_conv_ref, w_conv_ref, g_conv_ref, cols)
        o_ref[:, cols] = merged.astype(o_ref.dtype)


def _branch_merge(a_na, a_gqa, p_conv, p_gate, conv_w, conv_b, w_na, w_gqa, w_conv, layer, seq):
    t = a_na.shape[0]
    ch = conv_w.shape[2]
    d = w_na.shape[2]
    assert CONV_WIDTH == 3
    tm = _pick(seq, (512, 256))
    per_halo = tm // CONV_HALO
    last_halo = t // CONV_HALO - 1
    prev_map = lambda col: (lambda i: (jnp.maximum(i * per_halo - 1, 0), col))
    next_map = lambda col: (lambda i: (jnp.minimum((i + 1) * per_halo, last_halo), col))
    resident = lambda shape: pl.BlockSpec((None,) + shape, lambda i: (layer, 0, 0),
                                          pipeline_mode=pl.Buffered(1))
    return pl.pallas_call(
        functools.partial(_branch_merge_kernel, blocks_per_seq=seq // tm,
                          col_chunk=_pick(d, (512, 256, 128))),
        out_shape=jax.ShapeDtypeStruct((t, d), BF16),
        grid=(t // tm,),
        in_specs=[
            pl.BlockSpec((tm, NA_WIDTH), lambda i: (i, 0)),
            pl.BlockSpec((tm, GQA_Q_WIDTH), lambda i: (i, 0)),
            pl.BlockSpec((tm, ch), lambda i: (i, 0)),
            pl.BlockSpec((tm, ch), lambda i: (i, 1)),
            pl.BlockSpec((tm, ch), lambda i: (i, 2)),
            pl.BlockSpec((CONV_HALO, ch), prev_map(0)),
            pl.BlockSpec((CONV_HALO, ch), prev_map(2)),
            pl.BlockSpec((CONV_HALO, ch), next_map(0)),
            pl.BlockSpec((CONV_HALO, ch), next_map(2)),
            pl.BlockSpec((tm, d), lambda i: (i, 0)),
            pl.BlockSpec((tm, d), lambda i: (i, 1)),
            pl.BlockSpec((tm, d), lambda i: (i, 2)),
            pl.BlockSpec((None, CONV_WIDTH, ch), lambda i: (layer, 0, 0)),
            pl.BlockSpec((None, 1, ch), lambda i: (layer, 0, 0)),
            resident((NA_WIDTH, d)),
            resident((GQA_Q_WIDTH, d)),
            resident((ch, d)),
        ],
        out_specs=pl.BlockSpec((tm, d), lambda i: (i, 0)),
        scratch_shapes=[pltpu.VMEM((tm, ch), BF16), pltpu.VMEM((tm, d), F32)],
        compiler_params=_params(("parallel",)),
        name="branch_merge",
    )(a_na, a_gqa, p_conv, p_conv, p_conv, p_conv, p_conv, p_conv, p_conv, p_gate, p_gate, p_gate,
      conv_w, conv_b, w_na, w_gqa, w_conv)


def _out_proj_kernel(x_ref, m_ref, w_ref, g_post_ref, g_next_ref, o_ref, h_ref):
    part = x_ref.shape[0] // OUT_PROJ_ROW_PARTS
    for r in range(OUT_PROJ_ROW_PARTS):
        rows = slice(r * part, (r + 1) * part)
        y = jnp.dot(m_ref[rows, :], w_ref[...], preferred_element_type=F32)
        x_new = x_ref[rows, :] + _rms_rows(y, g_post_ref[...])
        o_ref[rows, :] = x_new
        h_ref[rows, :] = _rms_rows(x_new, g_next_ref[...]).astype(h_ref.dtype)


def _out_proj(x, merged, w_out, post_gains, next_gains, layer):
    t, d = x.shape
    tm = _pick(t, (512, 256))
    gain = lambda: pl.BlockSpec((None, 1, d), lambda i: (layer, 0, 0))
    return pl.pallas_call(
        _out_proj_kernel,
        out_shape=(jax.ShapeDtypeStruct((t, d), F32), jax.ShapeDtypeStruct((t, d), BF16)),
        grid=(t // tm,),
        in_specs=[
            pl.BlockSpec((tm, d), lambda i: (i, 0)),
            pl.BlockSpec((tm, d), lambda i: (i, 0)),
            pl.BlockSpec((None, d, d), lambda i: (layer, 0, 0), pipeline_mode=pl.Buffered(1)),
            gain(),
            gain(),
        ],
        out_specs=(pl.BlockSpec((tm, d), lambda i: (i, 0)), pl.BlockSpec((tm, d), lambda i: (i, 0))),
        compiler_params=_params(("parallel",)),
        name="out_proj_norm",
    )(x, merged, w_out, post_gains, next_gains)


def _ffn_kernel(x_ref, h_ref, wg_ref, wu_ref, wd_ref, g_post_ref, g_next_ref, o_ref, hn_ref, acc_ref):
    f = pl.program_id(1)

    @pl.when(f == 0)
    def _():
        acc_ref[...] = jnp.zeros_like(acc_ref)

    h = h_ref[...]
    a = jnp.dot(h, wg_ref[...], preferred_element_type=F32)
    b = jnp.dot(h, wu_ref[...], preferred_element_type=F32)
    act = (a * _sigmoid(a) * b).astype(BF16)
    acc_ref[...] += jnp.dot(act, wd_ref[...], preferred_element_type=F32)

    @pl.when(f == pl.num_programs(1) - 1)
    def _():
        x_new = x_ref[...] + _rms_rows(acc_ref[...], g_post_ref[...])
        o_ref[...] = x_new
        hn_ref[...] = _rms_rows(x_new, g_next_ref[...]).astype(hn_ref.dtype)


def _ffn(x, h, wg, wu, wd, post_gains, next_gains, layer, next_layer):
    t, d = x.shape
    hidden = wg.shape[1]
    tm = _pick(t, (512, 256))
    tf = _pick(hidden, (512, 256, 128))
    return pl.pallas_call(
        _ffn_kernel,
        out_shape=(jax.ShapeDtypeStruct((t, d), F32), jax.ShapeDtypeStruct((t, d), BF16)),
        grid=(t // tm, hidden // tf),
        in_specs=[
            pl.BlockSpec((tm, d), lambda i, f: (i, 0)),
            pl.BlockSpec((tm, d), lambda i, f: (i, 0)),
            pl.BlockSpec((d, tf), lambda i, f: (0, f)),
            pl.BlockSpec((d, tf), lambda i, f: (0, f)),
            pl.BlockSpec((tf, d), lambda i, f: (f, 0)),
            pl.BlockSpec((None, 1, d), lambda i, f: (layer, 0, 0)),
            pl.BlockSpec((None, 1, d), lambda i, f: (next_layer, 0, 0)),
        ],
        out_specs=(pl.BlockSpec((tm, d), lambda i, f: (i, 0)), pl.BlockSpec((tm, d), lambda i, f: (i, 0))),
        scratch_shapes=[pltpu.VMEM((tm, d), F32)],
        compiler_params=_params(("parallel", "arbitrary")),
        name="ffn_swiglu",
    )(x, h, wg, wu, wd, post_gains, next_gains)


def kernel(x, w_in, na_rpb, q_norm_g, k_norm_g, conv_w, conv_b, w_br_na, w_br_gqa, w_br_conv, w_out,
           pre_mix_g, post_mix_g, pre_ffn_g, post_ffn_g, w_ffn_gate, w_ffn_up, w_ffn_down):
    batch, seq, d = x.shape
    depth = w_in.shape[0]
    ch = conv_w.shape[2]
    cos, sin = _rope_tables(seq)

    gain3 = lambda g: g.astype(F32).reshape(depth, 1, -1)
    pre_mix_g, post_mix_g, pre_ffn_g, post_ffn_g = map(gain3, (pre_mix_g, post_mix_g, pre_ffn_g, post_ffn_g))
    conv_w = conv_w.astype(F32)
    conv_b = conv_b.astype(F32).reshape(depth, 1, ch)
    w_br_na, w_br_gqa, w_br_conv, w_out = (w.astype(BF16) for w in (w_br_na, w_br_gqa, w_br_conv, w_out))
    q_gain = q_norm_g.astype(F32) * (HEAD_DIM ** -0.5 * LOG2E)
    qk_gains = jnp.concatenate([jnp.tile(q_gain, (1, GQA_Q_HEADS)),
                                jnp.tile(k_norm_g.astype(F32), (1, GQA_KV_HEADS))], axis=1)
    qk_gains = qk_gains.reshape(depth, -1, 1, 2 * HEAD_DIM)

    y = x.reshape(batch * seq, d)
    h = _pre_norm(y, pre_mix_g, 0)
    for l in range(depth):
        p_attn = _in_proj(h, w_in, l, 0, OFF_CONV, "in_proj_attn")
        p_conv, wd = _in_proj(h, w_in, l, OFF_CONV, 3 * ch, "in_proj_conv", side=(w_ffn_down,))
        p_gate, wg, wu = _in_proj(h, w_in, l, OFF_CONV + 3 * ch, 3 * d, "in_proj_gate",
                                  side=(w_ffn_gate, w_ffn_up))

        qk = _qk_prep(p_attn, qk_gains[l], cos, sin, seq)
        plain_exp = (_gqa_logit_bound(q_gain[l], k_norm_g[l].astype(F32)) <= GQA_PLAIN_EXP_BOUND)
        a_gqa = _gqa_attention(qk, p_attn, plain_exp.astype(jnp.int32).reshape(1), batch, seq)
        a_na = _na_attention(p_attn, na_rpb[l], batch, seq)

        merged = _branch_merge(a_na, a_gqa, p_conv, p_gate, conv_w, conv_b, w_br_na, w_br_gqa,
                               w_br_conv, l, seq)
        y, h = _out_proj(y, merged, w_out, post_mix_g, pre_ffn_g, l)
        y, h = _ffn(y, h, wg, wu, wd, post_ffn_g, pre_mix_g, l, (l + 1) % depth)
    return y.reshape(batch, seq, d)
```

```python
import functools
import math

import numpy as np
import jax
import jax.numpy as jnp
from jax import lax
from jax.experimental import pallas as pl
from jax.experimental.pallas import tpu as pltpu

GRID_W = 64
HEAD_DIM = 128
NA_HEADS = 8
NA_WIN_ROWS = 8
NA_WIN_COLS = 16
GQA_Q_HEADS = 8
GQA_KV_HEADS = 2
GQA_GROUP = GQA_Q_HEADS // GQA_KV_HEADS
ROPE_THETA = 10000.0
CONV_WIDTH = 3
RMS_EPS = 1e-6

NA_WIDTH = NA_HEADS * HEAD_DIM
GQA_Q_WIDTH = GQA_Q_HEADS * HEAD_DIM
GQA_KV_WIDTH = GQA_KV_HEADS * HEAD_DIM
OFF_NA_Q = 0
OFF_NA_K = NA_WIDTH
OFF_NA_V = 2 * NA_WIDTH
OFF_GQA_Q = 3 * NA_WIDTH
OFF_GQA_K = OFF_GQA_Q + GQA_Q_WIDTH
OFF_GQA_V = OFF_GQA_K + GQA_KV_WIDTH
OFF_CONV = OFF_GQA_V + GQA_KV_WIDTH

MASK_VALUE = -1e30
LOG2E = math.log2(math.e)

V7X_VMEM_BYTES = 64 * 1024 * 1024
VMEM_LIMIT = V7X_VMEM_BYTES * 7 // 8

TOKENS_F32_BLOCK = (512, 256)
TOKENS_IN_PROJ = (1024, 512, 256)
TOKENS_QK_PREP = (2048, 1024, 512, 256)
IN_PROJ_COLS = 1536
WEIGHT_CAST_ROWS = 256
GQA_QUERY_BLOCK = (512, 256, 128)
GQA_KEY_CHUNK = (512, 256, 128)
GQA_CHUNKS_PER_TRIP = (4, 2)
NA_GROUP_ROWS = 4
NA_BAND_ROWS = NA_GROUP_ROWS + NA_WIN_ROWS
NA_GROUPS_PER_STEP = (8, 2, 1)
MERGE_COL_CHUNK = (512, 256, 128)
OUT_PROJ_ROW_PARTS = 4
FFN_HIDDEN_CHUNK = (512, 256, 128)

BF16 = jnp.bfloat16
F32 = jnp.float32
BF16_SUBLANES = 16


def _params(semantics):
    return pltpu.CompilerParams(dimension_semantics=semantics, vmem_limit_bytes=VMEM_LIMIT)


def _pick(n, candidates):
    for c in candidates:
        if n % c == 0:
            return c
    raise ValueError(f"no tile in {candidates} divides {n}")


def _sigmoid(x):
    return 0.5 * jnp.tanh(0.5 * x) + 0.5


def _rms_rows(x, g):
    ms = jnp.mean(x * x, axis=-1, keepdims=True)
    return x * lax.rsqrt(ms + RMS_EPS) * g


def _norm_kernel(x_ref, g_ref, o_ref):
    o_ref[...] = _rms_rows(x_ref[...], g_ref[...]).astype(o_ref.dtype)


def _pre_norm(x, gains, layer):
    t, d = x.shape
    tm = _pick(t, TOKENS_F32_BLOCK)
    return pl.pallas_call(
        _norm_kernel,
        out_shape=jax.ShapeDtypeStruct((t, d), BF16),
        grid=(t // tm,),
        in_specs=[
            pl.BlockSpec((tm, d), lambda i: (i, 0)),
            pl.BlockSpec((None, 1, d), lambda i: (layer, 0, 0)),
        ],
        out_specs=pl.BlockSpec((tm, d), lambda i: (i, 0)),
        compiler_params=_params(("parallel",)),
        name="pre_norm",
    )(x, gains)


def _proj_kernel(h_ref, w_ref, *refs, row_chunk, n_side):
    side_in, o_ref, side_out, wb_ref = refs[:n_side], refs[n_side], refs[n_side + 1:-1], refs[-1]

    @pl.when(pl.program_id(1) == 0)
    def _():
        def body(c, carry):
            rows = pl.ds(pl.multiple_of(c * row_chunk, row_chunk), row_chunk)
            wb_ref[rows, :] = w_ref[rows, :].astype(BF16)
            return carry
        lax.fori_loop(0, w_ref.shape[0] // row_chunk, body, 0)

    o_ref[...] = jnp.dot(h_ref[...], wb_ref[...], preferred_element_type=F32).astype(o_ref.dtype)
    for src, dst in zip(side_in, side_out):
        dst[...] = src[...].astype(dst.dtype)


def _in_proj(h, w_in, layer, col_off, n_cols, name, side=()):
    t, d = h.shape
    tm = _pick(t, TOKENS_IN_PROJ)
    tn = IN_PROJ_COLS
    assert col_off % tn == 0 and n_cols % tn == 0
    col0 = col_off // tn
    n_i = t // tm
    n_steps = (n_cols // tn) * n_i
    side_specs, side_out_specs, side_shapes = [], [], []
    for w in side:
        rows, cols = w.shape[1:]
        slab = next(s for s in range(BF16_SUBLANES, rows + 1, BF16_SUBLANES)
                    if rows % s == 0 and rows // s <= n_steps)
        last = rows // slab - 1
        side_specs.append(pl.BlockSpec(
            (None, slab, cols), lambda j, i, last=last: (layer, jnp.minimum(j * n_i + i, last), 0)))
        side_out_specs.append(pl.BlockSpec(
            (slab, cols), lambda j, i, last=last: (jnp.minimum(j * n_i + i, last), 0)))
        side_shapes.append(jax.ShapeDtypeStruct((rows, cols), BF16))
    outs = pl.pallas_call(
        functools.partial(_proj_kernel, row_chunk=min(d, WEIGHT_CAST_ROWS), n_side=len(side)),
        out_shape=[jax.ShapeDtypeStruct((t, n_cols), BF16)] + side_shapes,
        grid=(n_cols // tn, n_i),
        in_specs=[
            pl.BlockSpec((tm, d), lambda j, i: (i, 0)),
            pl.BlockSpec((None, d, tn), lambda j, i: (layer, 0, col0 + j)),
        ] + side_specs,
        out_specs=[pl.BlockSpec((tm, tn), lambda j, i: (i, j))] + side_out_specs,
        scratch_shapes=[pltpu.VMEM((d, tn), BF16)],
        compiler_params=_params(("arbitrary", "arbitrary")),
        name=name,
    )(h, w_in, *side)
    return outs[0] if not side else outs


def _rope_tables(seq):
    half = HEAD_DIM // 2
    quarter = half // 2
    inv_freq = 1.0 / (ROPE_THETA ** (np.arange(quarter, dtype=np.float64) / quarter))
    t = np.arange(seq)
    ang_r = (t // GRID_W)[:, None] * inv_freq[None, :]
    ang_c = (t % GRID_W)[:, None] * inv_freq[None, :]
    cos = np.concatenate([np.cos(ang_r), np.cos(ang_r), np.cos(ang_c), np.cos(ang_c)], axis=1)
    sin = np.concatenate([-np.sin(ang_r), np.sin(ang_r), -np.sin(ang_c), np.sin(ang_c)], axis=1)
    return jnp.asarray(cos, F32), jnp.asarray(sin, F32)


def _qk_prep_kernel(x_ref, g_ref, cos_ref, sin_ref, o_ref):
    heads = x_ref.shape[1] // HEAD_DIM
    quarter = HEAD_DIM // 4
    cos = cos_ref[...]
    sin = sin_ref[...]
    lane = lax.broadcasted_iota(jnp.int32, cos.shape, 1)
    first = (lane % (2 * quarter)) < quarter
    for h in range(heads):
        cols = slice(h * HEAD_DIM, (h + 1) * HEAD_DIM)
        y = _rms_rows(x_ref[:, cols].astype(F32), g_ref[0, :, cols])
        partner = jnp.where(first, pltpu.roll(y, HEAD_DIM - quarter, 1), pltpu.roll(y, quarter, 1))
        o_ref[:, cols] = (y * cos + partner * sin).astype(o_ref.dtype)


def _qk_prep(proj, gains, cos, sin, seq):
    t = proj.shape[0]
    n_heads = GQA_Q_HEADS + GQA_KV_HEADS
    width = 2 * HEAD_DIM
    assert OFF_GQA_Q % width == 0 and n_heads % 2 == 0
    tm = _pick(seq, TOKENS_QK_PREP)
    blocks_per_seq = seq // tm
    col0 = OFF_GQA_Q // width
    return pl.pallas_call(
        _qk_prep_kernel,
        out_shape=jax.ShapeDtypeStruct((t, n_heads * HEAD_DIM), BF16),
        grid=(t // tm, n_heads // 2),
        in_specs=[
            pl.BlockSpec((tm, width), lambda i, j: (i, col0 + j)),
            pl.BlockSpec((1, 1, width), lambda i, j: (j, 0, 0)),
            pl.BlockSpec((tm, HEAD_DIM), lambda i, j: (i % blocks_per_seq, 0)),
            pl.BlockSpec((tm, HEAD_DIM), lambda i, j: (i % blocks_per_seq, 0)),
        ],
        out_specs=pl.BlockSpec((tm, width), lambda i, j: (i, j)),
        compiler_params=_params(("parallel", "parallel")),
        name="gqa_qk_prep",
    )(proj, gains, cos, sin)


GQA_ONES_ROWS = 16
GQA_PLAIN_EXP_BOUND = 64.0


def _gqa_logit_bound(q_gain, k_gain):
    return 1.02 * HEAD_DIM * jnp.max(jnp.abs(q_gain)) * jnp.max(jnp.abs(k_gain))


def _gqa_kernel(plain_ref, q_ref, k_ref, v_ref, o_ref, qs_ref, vt_ref, acc_ref, s_ref, m_ref, *, tk):
    tq = q_ref.shape[0]
    seq = k_ref.shape[0]
    n_chunks = seq // tk
    assert n_chunks % 2 == 0

    def chunk(c):
        return pl.ds(pl.multiple_of(c * tk, tk), tk)

    @pl.when(pl.program_id(2) == 0)
    def _():
        def body(c, carry):
            vt_ref[c, :HEAD_DIM, :] = v_ref[chunk(c), :].astype(F32).T.astype(BF16)
            vt_ref[c, HEAD_DIM:, :] = jnp.ones((GQA_ONES_ROWS, tk), BF16)
            return carry
        lax.fori_loop(0, n_chunks, body, 0)

    for h in range(GQA_GROUP):
        qs_ref[h * tq:(h + 1) * tq, :] = q_ref[:, h * HEAD_DIM:(h + 1) * HEAD_DIM]
    acc_ref[...] = jnp.zeros_like(acc_ref)

    def scores(c):
        return lax.dot_general(k_ref[chunk(c), :], qs_ref[...], (((1,), (1,)), ((), ())),
                               preferred_element_type=F32)

    @pl.when(plain_ref[0] != 0)
    def _():
        per_trip = _pick(n_chunks, GQA_CHUNKS_PER_TRIP)

        def body(t, carry):
            pv = None
            for u in range(per_trip):
                c = per_trip * t + u
                p = jnp.exp2(scores(c)).astype(BF16)
                part = jnp.dot(vt_ref[c], p, preferred_element_type=F32)
                pv = part if pv is None else pv + part
            acc_ref[...] += pv
            return carry
        lax.fori_loop(0, n_chunks // per_trip, body, 0)

    @pl.when(plain_ref[0] == 0)
    def _():
        m_ref[...] = jnp.full_like(m_ref, -jnp.inf)

        def accumulate(c, slot):
            s = s_ref[slot]
            m_old = m_ref[...]
            m_new = jnp.maximum(m_old, jnp.max(s, axis=0, keepdims=True))
            m_ref[...] = m_new
            p = jnp.exp2(s - m_new).astype(BF16)
            pv = jnp.dot(vt_ref[c], p, preferred_element_type=F32)
            acc_ref[...] = acc_ref[...] * jnp.exp2(m_old - m_new) + pv

        s_ref[0] = scores(0)

        def body(t, carry):
            s_ref[1] = scores(2 * t + 1)
            accumulate(2 * t, 0)
            s_ref[0] = scores(2 * t + 2)
            accumulate(2 * t + 1, 1)
            return carry

        lax.fori_loop(0, n_chunks // 2 - 1, body, 0)
        s_ref[1] = scores(n_chunks - 1)
        accumulate(n_chunks - 2, 0)
        accumulate(n_chunks - 1, 1)

    out_t = acc_ref[:HEAD_DIM, :] / acc_ref[HEAD_DIM:HEAD_DIM + 1, :]
    for h in range(GQA_GROUP):
        o_ref[:, h * HEAD_DIM:(h + 1) * HEAD_DIM] = out_t[:, h * tq:(h + 1) * tq].T.astype(o_ref.dtype)


def _gqa_attention(qk, proj, plain_exp, batch, seq):
    t = qk.shape[0]
    tq = _pick(seq, GQA_QUERY_BLOCK)
    tk = _pick(seq, GQA_KEY_CHUNK)
    gw = GQA_GROUP * HEAD_DIM
    k_col0 = GQA_Q_WIDTH // HEAD_DIM
    v_col0 = OFF_GQA_V // HEAD_DIM
    q_blocks = seq // tq
    grid_spec = pltpu.PrefetchScalarGridSpec(
        num_scalar_prefetch=1,
        grid=(batch, GQA_KV_HEADS, q_blocks),
        in_specs=[
            pl.BlockSpec((tq, gw), lambda b, g, i, flag: (b * q_blocks + i, g)),
            pl.BlockSpec((seq, HEAD_DIM), lambda b, g, i, flag: (b, k_col0 + g)),
            pl.BlockSpec((seq, HEAD_DIM), lambda b, g, i, flag: (b, v_col0 + g)),
        ],
        out_specs=pl.BlockSpec((tq, gw), lambda b, g, i, flag: (b * q_blocks + i, g)),
        scratch_shapes=[
            pltpu.VMEM((GQA_GROUP * tq, HEAD_DIM), BF16),
            pltpu.VMEM((seq // tk, HEAD_DIM + GQA_ONES_ROWS, tk), BF16),
            pltpu.VMEM((HEAD_DIM + GQA_ONES_ROWS, GQA_GROUP * tq), F32),
            pltpu.VMEM((2, tk, GQA_GROUP * tq), F32),
            pltpu.VMEM((1, GQA_GROUP * tq), F32),
        ],
    )
    return pl.pallas_call(
        functools.partial(_gqa_kernel, tk=tk),
        out_shape=jax.ShapeDtypeStruct((t, GQA_Q_WIDTH), BF16),
        grid_spec=grid_spec,
        compiler_params=_params(("parallel", "parallel", "arbitrary")),
        name="gqa_flash",
    )(plain_exp, qk, qk, proj)


NA_PAIR_ROWS = 2 * NA_WIN_ROWS
NA_MASK_BOTH, NA_MASK_LEFT, NA_MASK_RIGHT = 0, 1, 2


def _na_block_plan(rows):
    gr, br = NA_GROUP_ROWS, NA_BAND_ROWS
    plan = []
    for r0 in (0, min(gr, rows - gr), rows - gr):
        band_start = int(np.clip(r0 - NA_WIN_ROWS // 2, 0, rows - br))
        per_row = []
        for i in range(gr):
            r = r0 + i
            row_start = int(np.clip(r - NA_WIN_ROWS // 2, 0, rows - NA_WIN_ROWS))
            blocks = []
            for jp in range(br // 2):
                key_row = band_start + 2 * jp
                ok_l = row_start <= key_row < row_start + NA_WIN_ROWS
                ok_r = row_start <= key_row + 1 < row_start + NA_WIN_ROWS
                if not (ok_l or ok_r):
                    blocks.append(None)
                    continue
                kind = NA_MASK_BOTH if (ok_l and ok_r) else (NA_MASK_LEFT if ok_l else NA_MASK_RIGHT)
                blocks.append((key_row - r + NA_WIN_ROWS, kind))
            per_row.append(blocks)
        plan.append(per_row)
    return plan


def _na_pair_table(rpb):
    h, nr, nc = rpb.shape
    rpb = jnp.pad(rpb.astype(F32), ((0, 0), (0, 0), (0, GRID_W - nc)))
    zero = jnp.zeros((h, 1, GRID_W), F32)
    left = jnp.concatenate([zero, rpb], axis=1)
    right = jnp.concatenate([rpb, zero], axis=1)
    return jnp.concatenate([left, right], axis=2)


def _na_col_masks():
    w = np.arange(GRID_W)[:, None]
    c = np.arange(2 * GRID_W)[None, :] % GRID_W
    col_start = np.clip(w - NA_WIN_COLS // 2, 0, GRID_W - NA_WIN_COLS)
    base = np.where((c >= col_start) & (c < col_start + NA_WIN_COLS), 0.0, MASK_VALUE)
    left_half = np.arange(2 * GRID_W)[None, :] < GRID_W
    masks = np.stack([base, np.where(left_half, base, MASK_VALUE), np.where(left_half, MASK_VALUE, base)])
    return jnp.asarray(masks, F32)


def _na_kernel(q_ref, k_ref, v_ref, pair_ref, mask_ref, o_ref, tile_ref, table_ref, *, rows):
    g = pl.program_id(2)
    n_groups = rows // NA_GROUP_ROWS
    blk_w = 2 * GRID_W

    @pl.when(g == 0)
    def _():
        for p in range(NA_PAIR_ROWS):
            row = jnp.broadcast_to(pair_ref[p:p + 1, :], (GRID_W, blk_w))
            tile_ref[p] = pltpu.roll(row, blk_w - (NA_WIN_COLS - 1), 1, stride=1, stride_axis=0)
        for v, per_row in enumerate(_na_block_plan(rows)):
            for i, blocks in enumerate(per_row):
                for jp, blk in enumerate(blocks):
                    dst = (v, slice(i * GRID_W, (i + 1) * GRID_W), slice(jp * blk_w, (jp + 1) * blk_w))
                    if blk is None:
                        table_ref[dst] = jnp.full((GRID_W, blk_w), MASK_VALUE, F32)
                    else:
                        table_ref[dst] = (tile_ref[blk[0]] + mask_ref[blk[1]]) * LOG2E

    gq = NA_GROUP_ROWS * GRID_W
    band_rows = NA_BAND_ROWS * GRID_W
    for u in range(q_ref.shape[0] // gq):
        gi = g * (q_ref.shape[0] // gq) + u
        start_row = jnp.clip(gi * NA_GROUP_ROWS - NA_WIN_ROWS // 2, 0, rows - NA_BAND_ROWS)
        band = pl.ds(pl.multiple_of(start_row * GRID_W, GRID_W), band_rows)
        variant = jnp.where(gi == 0, 0, jnp.where(gi == n_groups - 1, 2, 1))
        s = lax.dot_general(q_ref[u * gq:(u + 1) * gq, :], k_ref[band, :], (((1,), (1,)), ((), ())),
                            preferred_element_type=F32)
        s = s * (HEAD_DIM ** -0.5 * LOG2E) + table_ref[variant]
        p = jnp.exp2(s - jnp.max(s, axis=-1, keepdims=True))
        l = jnp.sum(p, axis=-1, keepdims=True)
        o = jnp.dot(p.astype(BF16), v_ref[band, :], preferred_element_type=F32)
        o_ref[u * gq:(u + 1) * gq, :] = (o / l).astype(o_ref.dtype)


def _na_attention(proj, rpb, batch, seq):
    t = proj.shape[0]
    rows = seq // GRID_W
    assert NA_GROUP_ROWS % 4 == 0 and rows % NA_GROUP_ROWS == 0 and rows >= NA_BAND_ROWS
    assert 2 * GRID_W == HEAD_DIM
    n_groups = rows // NA_GROUP_ROWS
    gq = NA_GROUP_ROWS * GRID_W
    gk = NA_BAND_ROWS * GRID_W
    per_step = _pick(n_groups, NA_GROUPS_PER_STEP)
    n_steps = n_groups // per_step
    k_col0 = OFF_NA_K // HEAD_DIM
    v_col0 = OFF_NA_V // HEAD_DIM
    return pl.pallas_call(
        functools.partial(_na_kernel, rows=rows),
        out_shape=jax.ShapeDtypeStruct((t, NA_WIDTH), BF16),
        grid=(batch, NA_HEADS, n_steps),
        in_specs=[
            pl.BlockSpec((per_step * gq, HEAD_DIM), lambda b, h, g: (b * n_steps + g, h)),
            pl.BlockSpec((seq, HEAD_DIM), lambda b, h, g: (b, k_col0 + h)),
            pl.BlockSpec((seq, HEAD_DIM), lambda b, h, g: (b, v_col0 + h)),
            pl.BlockSpec((None, NA_PAIR_ROWS, 2 * GRID_W), lambda b, h, g: (h, 0, 0)),
            pl.BlockSpec((3, GRID_W, 2 * GRID_W), lambda b, h, g: (0, 0, 0)),
        ],
        out_specs=pl.BlockSpec((per_step * gq, HEAD_DIM), lambda b, h, g: (b * n_steps + g, h)),
        scratch_shapes=[
            pltpu.VMEM((NA_PAIR_ROWS, GRID_W, 2 * GRID_W), F32),
            pltpu.VMEM((3, gq, gk), F32),
        ],
        compiler_params=_params(("parallel", "parallel", "arbitrary")),
        name="na_attention",
    )(proj, proj, proj, _na_pair_table(rpb), _na_col_masks())


CONV_HALO = 16


def _branch_merge_kernel(a_na_ref, a_gqa_ref, h_ref, b_ref, c_ref, hp_ref, cp_ref, hn_ref, cn_ref,
                         g_na_ref, g_gqa_ref, g_conv_ref, cw_ref, cb_ref, w_na_ref, w_gqa_ref,
                         w_conv_ref, o_ref, a_conv_ref, part_ref, *, blocks_per_seq, col_chunk):
    tm = h_ref.shape[0]
    pos = pl.program_id(0) % blocks_per_seq
    n_chunks = o_ref.shape[1] // col_chunk

    def branch(a_ref, w_ref, gate_ref, cols):
        y = jnp.dot(a_ref[...], w_ref[:, cols], preferred_element_type=F32)
        return _sigmoid(gate_ref[:, cols].astype(F32)) * y

    for c in range(n_chunks):
        cols = slice(c * col_chunk, (c + 1) * col_chunk)
        part_ref[:, cols] = (branch(a_na_ref, w_na_ref, g_na_ref, cols)
                             + branch(a_gqa_ref, w_gqa_ref, g_gqa_ref, cols))

    u = c_ref[...].astype(F32) * h_ref[...].astype(F32)
    u_prev_row = cp_ref[CONV_HALO - 1:CONV_HALO, :].astype(F32) * hp_ref[CONV_HALO - 1:CONV_HALO, :].astype(F32)
    u_prev_row = jnp.where(pos == 0, 0.0, u_prev_row)
    u_next_row = cn_ref[0:1, :].astype(F32) * hn_ref[0:1, :].astype(F32)
    u_next_row = jnp.where(pos == blocks_per_seq - 1, 0.0, u_next_row)
    row = lax.broadcasted_iota(jnp.int32, u.shape, 0)
    u_prev = jnp.where(row == 0, u_prev_row, pltpu.roll(u, 1, 0))
    u_next = jnp.where(row == tm - 1, u_next_row, pltpu.roll(u, tm - 1, 0))
    y = cb_ref[...] + u_prev * cw_ref[0:1, :] + u * cw_ref[1:2, :] + u_next * cw_ref[2:3, :]
    a_conv_ref[...] = (b_ref[...].astype(F32) * y).astype(BF16)

    for c in range(n_chunks):
        cols = slice(c * col_chunk, (c + 1) * col_chunk)
        merged = part_ref[:, cols] + branch(a_conv_ref, w_conv_ref, g_conv_ref, cols)
        o_ref[:, cols] = merged.astype(o_ref.dtype)


def _branch_merge(a_na, a_gqa, p_conv, p_gate, conv_w, conv_b, w_na, w_gqa, w_conv, layer, seq):
    t = a_na.shape[0]
    ch = conv_w.shape[2]
    d = w_na.shape[1]
    assert CONV_WIDTH == 3
    tm = _pick(seq, TOKENS_F32_BLOCK)
    per_halo = tm // CONV_HALO
    last_halo = t // CONV_HALO - 1
    prev_map = lambda col: (lambda i: (jnp.maximum(i * per_halo - 1, 0), col))
    next_map = lambda col: (lambda i: (jnp.minimum((i + 1) * per_halo, last_halo), col))
    resident = lambda shape: pl.BlockSpec(shape, lambda i: (0, 0), pipeline_mode=pl.Buffered(1))
    return pl.pallas_call(
        functools.partial(_branch_merge_kernel, blocks_per_seq=seq // tm,
                          col_chunk=_pick(d, MERGE_COL_CHUNK)),
        out_shape=jax.ShapeDtypeStruct((t, d), BF16),
        grid=(t // tm,),
        in_specs=[
            pl.BlockSpec((tm, NA_WIDTH), lambda i: (i, 0)),
            pl.BlockSpec((tm, GQA_Q_WIDTH), lambda i: (i, 0)),
            pl.BlockSpec((tm, ch), lambda i: (i, 0)),
            pl.BlockSpec((tm, ch), lambda i: (i, 1)),
            pl.BlockSpec((tm, ch), lambda i: (i, 2)),
            pl.BlockSpec((CONV_HALO, ch), prev_map(0)),
            pl.BlockSpec((CONV_HALO, ch), prev_map(2)),
            pl.BlockSpec((CONV_HALO, ch), next_map(0)),
            pl.BlockSpec((CONV_HALO, ch), next_map(2)),
            pl.BlockSpec((tm, d), lambda i: (i, 0)),
            pl.BlockSpec((tm, d), lambda i: (i, 1)),
            pl.BlockSpec((tm, d), lambda i: (i, 2)),
            pl.BlockSpec((None, CONV_WIDTH, ch), lambda i: (layer, 0, 0)),
            pl.BlockSpec((None, 1, ch), lambda i: (layer, 0, 0)),
            resident((NA_WIDTH, d)),
            resident((GQA_Q_WIDTH, d)),
            resident((ch, d)),
        ],
        out_specs=pl.BlockSpec((tm, d), lambda i: (i, 0)),
        scratch_shapes=[pltpu.VMEM((tm, ch), BF16), pltpu.VMEM((tm, d), F32)],
        compiler_params=_params(("parallel",)),
        name="branch_merge",
    )(a_na, a_gqa, p_conv, p_conv, p_conv, p_conv, p_conv, p_conv, p_conv, p_gate, p_gate, p_gate,
      conv_w, conv_b, w_na, w_gqa, w_conv)


def _out_proj_kernel(x_ref, m_ref, w_ref, g_post_ref, g_next_ref, o_ref, h_ref):
    part = x_ref.shape[0] // OUT_PROJ_ROW_PARTS
    for r in range(OUT_PROJ_ROW_PARTS):
        rows = slice(r * part, (r + 1) * part)
        y = jnp.dot(m_ref[rows, :], w_ref[...], preferred_element_type=F32)
        x_new = x_ref[rows, :] + _rms_rows(y, g_post_ref[...])
        o_ref[rows, :] = x_new
        h_ref[rows, :] = _rms_rows(x_new, g_next_ref[...]).astype(h_ref.dtype)


def _out_proj(x, merged, w_out, post_gains, next_gains, layer):
    t, d = x.shape
    tm = _pick(t, TOKENS_F32_BLOCK)
    gain = lambda: pl.BlockSpec((None, 1, d), lambda i: (layer, 0, 0))
    return pl.pallas_call(
        _out_proj_kernel,
        out_shape=(jax.ShapeDtypeStruct((t, d), F32), jax.ShapeDtypeStruct((t, d), BF16)),
        grid=(t // tm,),
        in_specs=[
            pl.BlockSpec((tm, d), lambda i: (i, 0)),
            pl.BlockSpec((tm, d), lambda i: (i, 0)),
            pl.BlockSpec((d, d), lambda i: (0, 0), pipeline_mode=pl.Buffered(1)),
            gain(),
            gain(),
        ],
        out_specs=(pl.BlockSpec((tm, d), lambda i: (i, 0)), pl.BlockSpec((tm, d), lambda i: (i, 0))),
        compiler_params=_params(("parallel",)),
        name="out_proj_norm",
    )(x, merged, w_out, post_gains, next_gains)


def _ffn_kernel(x_ref, h_ref, wg_ref, wu_ref, wd_ref, g_post_ref, *refs, emit_next):
    (g_next_ref, o_ref, hn_ref, acc_ref) = refs if emit_next else (None, refs[0], None, refs[1])
    f = pl.program_id(1)

    @pl.when(f == 0)
    def _():
        acc_ref[...] = jnp.zeros_like(acc_ref)

    h = h_ref[...]
    a = jnp.dot(h, wg_ref[...], preferred_element_type=F32)
    b = jnp.dot(h, wu_ref[...], preferred_element_type=F32)
    act = (a * _sigmoid(a) * b).astype(BF16)
    acc_ref[...] += jnp.dot(act, wd_ref[...], preferred_element_type=F32)

    @pl.when(f == pl.num_programs(1) - 1)
    def _():
        x_new = x_ref[...] + _rms_rows(acc_ref[...], g_post_ref[...])
        o_ref[...] = x_new
        if emit_next:
            hn_ref[...] = _rms_rows(x_new, g_next_ref[...]).astype(hn_ref.dtype)


def _ffn(x, h, wg, wu, wd, post_gains, next_gains, layer, next_layer):
    t, d = x.shape
    hidden = wg.shape[1]
    tm = _pick(t, TOKENS_F32_BLOCK)
    tf = _pick(hidden, FFN_HIDDEN_CHUNK)
    emit_next = next_layer is not None
    token_block = lambda: pl.BlockSpec((tm, d), lambda i, f: (i, 0))
    gain = lambda l: pl.BlockSpec((None, 1, d), lambda i, f: (l, 0, 0))
    outs = pl.pallas_call(
        functools.partial(_ffn_kernel, emit_next=emit_next),
        out_shape=[jax.ShapeDtypeStruct((t, d), F32)] + [jax.ShapeDtypeStruct((t, d), BF16)] * emit_next,
        grid=(t // tm, hidden // tf),
        in_specs=[
            token_block(),
            token_block(),
            pl.BlockSpec((d, tf), lambda i, f: (0, f)),
            pl.BlockSpec((d, tf), lambda i, f: (0, f)),
            pl.BlockSpec((tf, d), lambda i, f: (f, 0)),
            gain(layer),
        ] + [gain(next_layer)] * emit_next,
        out_specs=[token_block()] + [token_block()] * emit_next,
        scratch_shapes=[pltpu.VMEM((tm, d), F32)],
        compiler_params=_params(("parallel", "arbitrary")),
        name="ffn_swiglu",
    )(x, h, wg, wu, wd, post_gains, *([next_gains] * emit_next))
    return (outs[0], outs[1]) if emit_next else (outs[0], None)


def kernel(x, w_in, na_rpb, q_norm_g, k_norm_g, conv_w, conv_b, w_br_na, w_br_gqa, w_br_conv, w_out,
           pre_mix_g, post_mix_g, pre_ffn_g, post_ffn_g, w_ffn_gate, w_ffn_up, w_ffn_down):
    batch, seq, d = x.shape
    depth = w_in.shape[0]
    ch = conv_w.shape[2]
    cos, sin = _rope_tables(seq)

    gain3 = lambda g: g.astype(F32).reshape(depth, 1, -1)
    pre_mix_g, post_mix_g, pre_ffn_g, post_ffn_g = map(gain3, (pre_mix_g, post_mix_g, pre_ffn_g, post_ffn_g))
    conv_w = conv_w.astype(F32)
    conv_b = conv_b.astype(F32).reshape(depth, 1, ch)
    q_gain = q_norm_g.astype(F32) * (HEAD_DIM ** -0.5 * LOG2E)
    qk_gains = jnp.concatenate([jnp.tile(q_gain, (1, GQA_Q_HEADS)),
                                jnp.tile(k_norm_g.astype(F32), (1, GQA_KV_HEADS))], axis=1)
    qk_gains = qk_gains.reshape(depth, -1, 1, 2 * HEAD_DIM)

    y = x.reshape(batch * seq, d)
    h = _pre_norm(y, pre_mix_g, 0)
    for l in range(depth):
        p_attn, w_na, w_gqa, w_conv, w_o = _in_proj(h, w_in, l, 0, OFF_CONV, "in_proj_attn",
                                                    side=(w_br_na, w_br_gqa, w_br_conv, w_out))
        p_conv, wd = _in_proj(h, w_in, l, OFF_CONV, 3 * ch, "in_proj_conv", side=(w_ffn_down,))
        p_gate, wg, wu = _in_proj(h, w_in, l, OFF_CONV + 3 * ch, 3 * d, "in_proj_gate",
                                  side=(w_ffn_gate, w_ffn_up))

        qk = _qk_prep(p_attn, qk_gains[l], cos, sin, seq)
        plain_exp = (_gqa_logit_bound(q_gain[l], k_norm_g[l].astype(F32)) <= GQA_PLAIN_EXP_BOUND)
        a_gqa = _gqa_attention(qk, p_attn, plain_exp.astype(jnp.int32).reshape(1), batch, seq)
        a_na = _na_attention(p_attn, na_rpb[l], batch, seq)

        merged = _branch_merge(a_na, a_gqa, p_conv, p_gate, conv_w, conv_b, w_na, w_gqa, w_conv, l, seq)
        y, h = _out_proj(y, merged, w_o, post_mix_g, pre_ffn_g, l)
        y, h = _ffn(y, h, wg, wu, wd, post_ffn_g, pre_mix_g, l, l + 1 if l + 1 < depth else None)
    return y.reshape(batch, seq, d)
```

```python
import functools
import math

import numpy as np
import jax
import jax.numpy as jnp
from jax import lax
from jax.experimental import pallas as pl
from jax.experimental.pallas import tpu as pltpu

GRID_W = 64
HEAD_DIM = 128
NA_HEADS = 8
NA_WIN_ROWS = 8
NA_WIN_COLS = 16
GQA_Q_HEADS = 8
GQA_KV_HEADS = 2
GQA_GROUP = GQA_Q_HEADS // GQA_KV_HEADS
ROPE_THETA = 10000.0
CONV_WIDTH = 3
RMS_EPS = 1e-6

NA_WIDTH = NA_HEADS * HEAD_DIM
GQA_Q_WIDTH = GQA_Q_HEADS * HEAD_DIM
GQA_KV_WIDTH = GQA_KV_HEADS * HEAD_DIM
OFF_NA_Q = 0
OFF_NA_K = NA_WIDTH
OFF_NA_V = 2 * NA_WIDTH
OFF_GQA_Q = 3 * NA_WIDTH
OFF_GQA_K = OFF_GQA_Q + GQA_Q_WIDTH
OFF_GQA_V = OFF_GQA_K + GQA_KV_WIDTH
OFF_CONV = OFF_GQA_V + GQA_KV_WIDTH

MASK_VALUE = -1e30
LOG2E = math.log2(math.e)

V7X_VMEM_BYTES = 64 * 1024 * 1024
VMEM_LIMIT = V7X_VMEM_BYTES * 7 // 8

TOKENS_F32_BLOCK = (512, 256)
TOKENS_IN_PROJ = (1024, 512, 256)
TOKENS_QK_PREP = (2048, 1024, 512, 256)
IN_PROJ_COLS = 1536
WEIGHT_CAST_ROWS = 256
GQA_QUERY_BLOCK = (512, 256, 128)
GQA_KEY_CHUNK = (512, 256, 128)
GQA_CHUNKS_PER_TRIP = (4, 2)
NA_GROUP_ROWS = 4
NA_BAND_ROWS = NA_GROUP_ROWS + NA_WIN_ROWS
NA_GROUPS_PER_STEP = (8, 2, 1)
MERGE_COL_CHUNK = (512, 256, 128)
OUT_PROJ_ROW_PARTS = 4
FFN_HIDDEN_CHUNK = (512, 256, 128)

BF16 = jnp.bfloat16
F32 = jnp.float32
BF16_SUBLANES = 16


def _params(semantics):
    return pltpu.CompilerParams(dimension_semantics=semantics, vmem_limit_bytes=VMEM_LIMIT)


def _pick(n, candidates):
    for c in candidates:
        if n % c == 0:
            return c
    raise ValueError(f"no tile in {candidates} divides {n}")


def _sigmoid(x):
    return 0.5 * jnp.tanh(0.5 * x) + 0.5


def _rms_rows(x, g):
    ms = jnp.mean(x * x, axis=-1, keepdims=True)
    return x * lax.rsqrt(ms + RMS_EPS) * g


def _norm_kernel(x_ref, g_ref, o_ref):
    o_ref[...] = _rms_rows(x_ref[...], g_ref[...]).astype(o_ref.dtype)


def _pre_norm(x, gains, layer):
    t, d = x.shape
    tm = _pick(t, TOKENS_F32_BLOCK)
    return pl.pallas_call(
        _norm_kernel,
        out_shape=jax.ShapeDtypeStruct((t, d), BF16),
        grid=(t // tm,),
        in_specs=[
            pl.BlockSpec((tm, d), lambda i: (i, 0)),
            pl.BlockSpec((None, 1, d), lambda i: (layer, 0, 0)),
        ],
        out_specs=pl.BlockSpec((tm, d), lambda i: (i, 0)),
        compiler_params=_params(("parallel",)),
        name="pre_norm",
    )(x, gains)


def _proj_kernel(h_ref, w_ref, *refs, row_chunk, n_side):
    side_in, o_ref, side_out, wb_ref = refs[:n_side], refs[n_side], refs[n_side + 1:-1], refs[-1]

    @pl.when(pl.program_id(1) == 0)
    def _():
        def body(c, carry):
            rows = pl.ds(pl.multiple_of(c * row_chunk, row_chunk), row_chunk)
            wb_ref[rows, :] = w_ref[rows, :].astype(BF16)
            return carry
        lax.fori_loop(0, w_ref.shape[0] // row_chunk, body, 0)

    o_ref[...] = jnp.dot(h_ref[...], wb_ref[...], preferred_element_type=F32).astype(o_ref.dtype)
    for src, dst in zip(side_in, side_out):
        dst[...] = src[...].astype(dst.dtype)


def _in_proj(h, w_in, layer, col_off, n_cols, name, side=()):
    t, d = h.shape
    tm = _pick(t, TOKENS_IN_PROJ)
    tn = IN_PROJ_COLS
    assert col_off % tn == 0 and n_cols % tn == 0
    col0 = col_off // tn
    n_i = t // tm
    n_steps = (n_cols // tn) * n_i
    side_specs, side_out_specs, side_shapes = [], [], []
    for w in side:
        rows, cols = w.shape[1:]
        slab = next(s for s in range(BF16_SUBLANES, rows + 1, BF16_SUBLANES)
                    if rows % s == 0 and rows // s <= n_steps)
        last = rows // slab - 1
        side_specs.append(pl.BlockSpec(
            (None, slab, cols), lambda j, i, last=last: (layer, jnp.minimum(j * n_i + i, last), 0)))
        side_out_specs.append(pl.BlockSpec(
            (slab, cols), lambda j, i, last=last: (jnp.minimum(j * n_i + i, last), 0)))
        side_shapes.append(jax.ShapeDtypeStruct((rows, cols), BF16))
    outs = pl.pallas_call(
        functools.partial(_proj_kernel, row_chunk=min(d, WEIGHT_CAST_ROWS), n_side=len(side)),
        out_shape=[jax.ShapeDtypeStruct((t, n_cols), BF16)] + side_shapes,
        grid=(n_cols // tn, n_i),
        in_specs=[
            pl.BlockSpec((tm, d), lambda j, i: (i, 0)),
            pl.BlockSpec((None, d, tn), lambda j, i: (layer, 0, col0 + j)),
        ] + side_specs,
        out_specs=[pl.BlockSpec((tm, tn), lambda j, i: (i, j))] + side_out_specs,
        scratch_shapes=[pltpu.VMEM((d, tn), BF16)],
        compiler_params=_params(("arbitrary", "arbitrary")),
        name=name,
    )(h, w_in, *side)
    return outs[0] if not side else outs


def _rope_tables(seq):
    half = HEAD_DIM // 2
    quarter = half // 2
    inv_freq = 1.0 / (ROPE_THETA ** (np.arange(quarter, dtype=np.float64) / quarter))
    t = np.arange(seq)
    ang_r = (t // GRID_W)[:, None] * inv_freq[None, :]
    ang_c = (t % GRID_W)[:, None] * inv_freq[None, :]
    cos = np.concatenate([np.cos(ang_r), np.cos(ang_r), np.cos(ang_c), np.cos(ang_c)], axis=1)
    sin = np.concatenate([-np.sin(ang_r), np.sin(ang_r), -np.sin(ang_c), np.sin(ang_c)], axis=1)
    return jnp.asarray(cos, F32), jnp.asarray(sin, F32)


def _qk_prep_kernel(x_ref, g_ref, cos_ref, sin_ref, o_ref):
    heads = x_ref.shape[1] // HEAD_DIM
    quarter = HEAD_DIM // 4
    cos = cos_ref[...]
    sin = sin_ref[...]
    lane = lax.broadcasted_iota(jnp.int32, cos.shape, 1)
    first = (lane % (2 * quarter)) < quarter
    for h in range(heads):
        cols = slice(h * HEAD_DIM, (h + 1) * HEAD_DIM)
        y = _rms_rows(x_ref[:, cols].astype(F32), g_ref[0, :, cols])
        partner = jnp.where(first, pltpu.roll(y, HEAD_DIM - quarter, 1), pltpu.roll(y, quarter, 1))
        o_ref[:, cols] = (y * cos + partner * sin).astype(o_ref.dtype)


def _qk_prep(proj, gains, cos, sin, seq):
    t = proj.shape[0]
    n_heads = GQA_Q_HEADS + GQA_KV_HEADS
    width = 2 * HEAD_DIM
    assert OFF_GQA_Q % width == 0 and n_heads % 2 == 0
    tm = _pick(seq, TOKENS_QK_PREP)
    blocks_per_seq = seq // tm
    col0 = OFF_GQA_Q // width
    return pl.pallas_call(
        _qk_prep_kernel,
        out_shape=jax.ShapeDtypeStruct((t, n_heads * HEAD_DIM), BF16),
        grid=(t // tm, n_heads // 2),
        in_specs=[
            pl.BlockSpec((tm, width), lambda i, j: (i, col0 + j)),
            pl.BlockSpec((1, 1, width), lambda i, j: (j, 0, 0)),
            pl.BlockSpec((tm, HEAD_DIM), lambda i, j: (i % blocks_per_seq, 0)),
            pl.BlockSpec((tm, HEAD_DIM), lambda i, j: (i % blocks_per_seq, 0)),
        ],
        out_specs=pl.BlockSpec((tm, width), lambda i, j: (i, j)),
        compiler_params=_params(("parallel", "parallel")),
        name="gqa_qk_prep",
    )(proj, gains, cos, sin)


GQA_ONES_ROWS = 16
GQA_PLAIN_EXP_BOUND = 64.0


def _gqa_logit_bound(q_gain, k_gain):
    return 1.02 * HEAD_DIM * jnp.max(jnp.abs(q_gain)) * jnp.max(jnp.abs(k_gain))


def _gqa_kernel(plain_ref, q_ref, k_ref, v_ref, o_ref, qs_ref, vt_ref, acc_ref, s_ref, m_ref, *, tk):
    tq = q_ref.shape[0]
    seq = k_ref.shape[0]
    n_chunks = seq // tk
    assert n_chunks % 2 == 0

    def chunk(c):
        return pl.ds(pl.multiple_of(c * tk, tk), tk)

    @pl.when(pl.program_id(2) == 0)
    def _():
        def body(c, carry):
            vt_ref[c, :HEAD_DIM, :] = v_ref[chunk(c), :].astype(F32).T.astype(BF16)
            vt_ref[c, HEAD_DIM:, :] = jnp.ones((GQA_ONES_ROWS, tk), BF16)
            return carry
        lax.fori_loop(0, n_chunks, body, 0)

    for h in range(GQA_GROUP):
        qs_ref[h * tq:(h + 1) * tq, :] = q_ref[:, h * HEAD_DIM:(h + 1) * HEAD_DIM]
    acc_ref[...] = jnp.zeros_like(acc_ref)

    def scores(c):
        return lax.dot_general(k_ref[chunk(c), :], qs_ref[...], (((1,), (1,)), ((), ())),
                               preferred_element_type=F32)

    @pl.when(plain_ref[0] != 0)
    def _():
        per_trip = _pick(n_chunks, GQA_CHUNKS_PER_TRIP)

        def body(t, carry):
            pv = None
            for u in range(per_trip):
                c = per_trip * t + u
                p = jnp.exp2(scores(c)).astype(BF16)
                part = jnp.dot(vt_ref[c], p, preferred_element_type=F32)
                pv = part if pv is None else pv + part
            acc_ref[...] += pv
            return carry
        lax.fori_loop(0, n_chunks // per_trip, body, 0)

    @pl.when(plain_ref[0] == 0)
    def _():
        m_ref[...] = jnp.full_like(m_ref, -jnp.inf)

        def accumulate(c, slot):
            s = s_ref[slot]
            m_old = m_ref[...]
            m_new = jnp.maximum(m_old, jnp.max(s, axis=0, keepdims=True))
            m_ref[...] = m_new
            p = jnp.exp2(s - m_new).astype(BF16)
            pv = jnp.dot(vt_ref[c], p, preferred_element_type=F32)
            acc_ref[...] = acc_ref[...] * jnp.exp2(m_old - m_new) + pv

        s_ref[0] = scores(0)

        def body(t, carry):
            s_ref[1] = scores(2 * t + 1)
            accumulate(2 * t, 0)
            s_ref[0] = scores(2 * t + 2)
            accumulate(2 * t + 1, 1)
            return carry

        lax.fori_loop(0, n_chunks // 2 - 1, body, 0)
        s_ref[1] = scores(n_chunks - 1)
        accumulate(n_chunks - 2, 0)
        accumulate(n_chunks - 1, 1)

    out_t = acc_ref[:HEAD_DIM, :] / acc_ref[HEAD_DIM:HEAD_DIM + 1, :]
    for h in range(GQA_GROUP):
        o_ref[:, h * HEAD_DIM:(h + 1) * HEAD_DIM] = out_t[:, h * tq:(h + 1) * tq].T.astype(o_ref.dtype)


def _gqa_attention(qk, proj, plain_exp, batch, seq):
    t = qk.shape[0]
    tq = _pick(seq, GQA_QUERY_BLOCK)
    tk = _pick(seq, GQA_KEY_CHUNK)
    gw = GQA_GROUP * HEAD_DIM
    k_col0 = GQA_Q_WIDTH // HEAD_DIM
    v_col0 = OFF_GQA_V // HEAD_DIM
    q_blocks = seq // tq
    grid_spec = pltpu.PrefetchScalarGridSpec(
        num_scalar_prefetch=1,
        grid=(batch, GQA_KV_HEADS, q_blocks),
        in_specs=[
            pl.BlockSpec((tq, gw), lambda b, g, i, flag: (b * q_blocks + i, g)),
            pl.BlockSpec((seq, HEAD_DIM), lambda b, g, i, flag: (b, k_col0 + g)),
            pl.BlockSpec((seq, HEAD_DIM), lambda b, g, i, flag: (b, v_col0 + g)),
        ],
        out_specs=pl.BlockSpec((tq, gw), lambda b, g, i, flag: (b * q_blocks + i, g)),
        scratch_shapes=[
            pltpu.VMEM((GQA_GROUP * tq, HEAD_DIM), BF16),
            pltpu.VMEM((seq // tk, HEAD_DIM + GQA_ONES_ROWS, tk), BF16),
            pltpu.VMEM((HEAD_DIM + GQA_ONES_ROWS, GQA_GROUP * tq), F32),
            pltpu.VMEM((2, tk, GQA_GROUP * tq), F32),
            pltpu.VMEM((1, GQA_GROUP * tq), F32),
        ],
    )
    return pl.pallas_call(
        functools.partial(_gqa_kernel, tk=tk),
        out_shape=jax.ShapeDtypeStruct((t, GQA_Q_WIDTH), BF16),
        grid_spec=grid_spec,
        compiler_params=_params(("parallel", "parallel", "arbitrary")),
        name="gqa_flash",
    )(plain_exp, qk, qk, proj)


NA_PAIR_ROWS = 2 * NA_WIN_ROWS
NA_MASK_BOTH, NA_MASK_LEFT, NA_MASK_RIGHT = 0, 1, 2


def _na_block_plan(rows):
    gr, br = NA_GROUP_ROWS, NA_BAND_ROWS
    plan = []
    for r0 in (0, min(gr, rows - gr), rows - gr):
        band_start = int(np.clip(r0 - NA_WIN_ROWS // 2, 0, rows - br))
        per_row = []
        for i in range(gr):
            r = r0 + i
            row_start = int(np.clip(r - NA_WIN_ROWS // 2, 0, rows - NA_WIN_ROWS))
            blocks = []
            for jp in range(br // 2):
                key_row = band_start + 2 * jp
                ok_l = row_start <= key_row < row_start + NA_WIN_ROWS
                ok_r = row_start <= key_row + 1 < row_start + NA_WIN_ROWS
                if not (ok_l or ok_r):
                    blocks.append(None)
                    continue
                kind = NA_MASK_BOTH if (ok_l and ok_r) else (NA_MASK_LEFT if ok_l else NA_MASK_RIGHT)
                blocks.append((key_row - r + NA_WIN_ROWS, kind))
            per_row.append(blocks)
        plan.append(per_row)
    return plan


def _na_pair_table(rpb):
    h, nr, nc = rpb.shape
    rpb = jnp.pad(rpb.astype(F32), ((0, 0), (0, 0), (0, GRID_W - nc)))
    zero = jnp.zeros((h, 1, GRID_W), F32)
    left = jnp.concatenate([zero, rpb], axis=1)
    right = jnp.concatenate([rpb, zero], axis=1)
    return jnp.concatenate([left, right], axis=2)


def _na_col_masks():
    w = np.arange(GRID_W)[:, None]
    c = np.arange(2 * GRID_W)[None, :] % GRID_W
    col_start = np.clip(w - NA_WIN_COLS // 2, 0, GRID_W - NA_WIN_COLS)
    base = np.where((c >= col_start) & (c < col_start + NA_WIN_COLS), 0.0, MASK_VALUE)
    left_half = np.arange(2 * GRID_W)[None, :] < GRID_W
    masks = np.stack([base, np.where(left_half, base, MASK_VALUE), np.where(left_half, MASK_VALUE, base)])
    return jnp.asarray(masks, F32)


def _na_kernel(q_ref, k_ref, v_ref, pair_ref, mask_ref, o_ref, tile_ref, table_ref, vaug_ref, *, rows):
    g = pl.program_id(2)
    n_groups = rows // NA_GROUP_ROWS
    blk_w = 2 * GRID_W

    @pl.when(g == 0)
    def _():
        for p in range(NA_PAIR_ROWS):
            row = jnp.broadcast_to(pair_ref[p:p + 1, :], (GRID_W, blk_w))
            tile_ref[p] = pltpu.roll(row, blk_w - (NA_WIN_COLS - 1), 1, stride=1, stride_axis=0)
        for v, per_row in enumerate(_na_block_plan(rows)):
            for i, blocks in enumerate(per_row):
                for jp, blk in enumerate(blocks):
                    dst = (v, slice(i * GRID_W, (i + 1) * GRID_W), slice(jp * blk_w, (jp + 1) * blk_w))
                    if blk is None:
                        table_ref[dst] = jnp.full((GRID_W, blk_w), MASK_VALUE, F32)
                    else:
                        table_ref[dst] = (tile_ref[blk[0]] + mask_ref[blk[1]]) * LOG2E
        vaug_ref[:, :HEAD_DIM] = v_ref[...]
        vaug_ref[:, HEAD_DIM:] = jnp.ones(v_ref.shape, BF16)

    gq = NA_GROUP_ROWS * GRID_W
    band_rows = NA_BAND_ROWS * GRID_W
    for u in range(q_ref.shape[0] // gq):
        gi = g * (q_ref.shape[0] // gq) + u
        start_row = jnp.clip(gi * NA_GROUP_ROWS - NA_WIN_ROWS // 2, 0, rows - NA_BAND_ROWS)
        band = pl.ds(pl.multiple_of(start_row * GRID_W, GRID_W), band_rows)
        variant = jnp.where(gi == 0, 0, jnp.where(gi == n_groups - 1, 2, 1))
        s = lax.dot_general(q_ref[u * gq:(u + 1) * gq, :], k_ref[band, :], (((1,), (1,)), ((), ())),
                            preferred_element_type=F32)
        s = s * (HEAD_DIM ** -0.5 * LOG2E) + table_ref[variant]
        p = jnp.exp2(s - jnp.max(s, axis=-1, keepdims=True)).astype(BF16)
        o = jnp.dot(p, vaug_ref[band, :], preferred_element_type=F32)
        o_ref[u * gq:(u + 1) * gq, :] = (o[:, :HEAD_DIM] / o[:, HEAD_DIM:]).astype(o_ref.dtype)


def _na_attention(proj, rpb, batch, seq):
    t = proj.shape[0]
    rows = seq // GRID_W
    assert NA_GROUP_ROWS % 4 == 0 and rows % NA_GROUP_ROWS == 0 and rows >= NA_BAND_ROWS
    assert 2 * GRID_W == HEAD_DIM
    n_groups = rows // NA_GROUP_ROWS
    gq = NA_GROUP_ROWS * GRID_W
    gk = NA_BAND_ROWS * GRID_W
    per_step = _pick(n_groups, NA_GROUPS_PER_STEP)
    n_steps = n_groups // per_step
    k_col0 = OFF_NA_K // HEAD_DIM
    v_col0 = OFF_NA_V // HEAD_DIM
    return pl.pallas_call(
        functools.partial(_na_kernel, rows=rows),
        out_shape=jax.ShapeDtypeStruct((t, NA_WIDTH), BF16),
        grid=(batch, NA_HEADS, n_steps),
        in_specs=[
            pl.BlockSpec((per_step * gq, HEAD_DIM), lambda b, h, g: (b * n_steps + g, h)),
            pl.BlockSpec((seq, HEAD_DIM), lambda b, h, g: (b, k_col0 + h)),
            pl.BlockSpec((seq, HEAD_DIM), lambda b, h, g: (b, v_col0 + h)),
            pl.BlockSpec((None, NA_PAIR_ROWS, 2 * GRID_W), lambda b, h, g: (h, 0, 0)),
            pl.BlockSpec((3, GRID_W, 2 * GRID_W), lambda b, h, g: (0, 0, 0)),
        ],
        out_specs=pl.BlockSpec((per_step * gq, HEAD_DIM), lambda b, h, g: (b * n_steps + g, h)),
        scratch_shapes=[
            pltpu.VMEM((NA_PAIR_ROWS, GRID_W, 2 * GRID_W), F32),
            pltpu.VMEM((3, gq, gk), F32),
            pltpu.VMEM((seq, 2 * HEAD_DIM), BF16),
        ],
        compiler_params=_params(("parallel", "parallel", "arbitrary")),
        name="na_attention",
    )(proj, proj, proj, _na_pair_table(rpb), _na_col_masks())


CONV_HALO = 16


def _branch_merge_kernel(a_na_ref, a_gqa_ref, h_ref, b_ref, c_ref, hp_ref, cp_ref, hn_ref, cn_ref,
                         g_na_ref, g_gqa_ref, g_conv_ref, cw_ref, cb_ref, w_na_ref, w_gqa_ref,
                         w_conv_ref, o_ref, a_conv_ref, part_ref, *, blocks_per_seq, col_chunk):
    tm = h_ref.shape[0]
    pos = pl.program_id(0) % blocks_per_seq
    n_chunks = o_ref.shape[1] // col_chunk

    def branch(a_ref, w_ref, gate_ref, cols):
        y = jnp.dot(a_ref[...], w_ref[:, cols], preferred_element_type=F32)
        return _sigmoid(gate_ref[:, cols].astype(F32)) * y

    for c in range(n_chunks):
        cols = slice(c * col_chunk, (c + 1) * col_chunk)
        part_ref[:, cols] = (branch(a_na_ref, w_na_ref, g_na_ref, cols)
                             + branch(a_gqa_ref, w_gqa_ref, g_gqa_ref, cols))

    u = c_ref[...].astype(F32) * h_ref[...].astype(F32)
    u_prev_row = cp_ref[CONV_HALO - 1:CONV_HALO, :].astype(F32) * hp_ref[CONV_HALO - 1:CONV_HALO, :].astype(F32)
    u_prev_row = jnp.where(pos == 0, 0.0, u_prev_row)
    u_next_row = cn_ref[0:1, :].astype(F32) * hn_ref[0:1, :].astype(F32)
    u_next_row = jnp.where(pos == blocks_per_seq - 1, 0.0, u_next_row)
    row = lax.broadcasted_iota(jnp.int32, u.shape, 0)
    u_prev = jnp.where(row == 0, u_prev_row, pltpu.roll(u, 1, 0))
    u_next = jnp.where(row == tm - 1, u_next_row, pltpu.roll(u, tm - 1, 0))
    y = cb_ref[...] + u_prev * cw_ref[0:1, :] + u * cw_ref[1:2, :] + u_next * cw_ref[2:3, :]
    a_conv_ref[...] = (b_ref[...].astype(F32) * y).astype(BF16)

    for c in range(n_chunks):
        cols = slice(c * col_chunk, (c + 1) * col_chunk)
        merged = part_ref[:, cols] + branch(a_conv_ref, w_conv_ref, g_conv_ref, cols)
        o_ref[:, cols] = merged.astype(o_ref.dtype)


def _branch_merge(a_na, a_gqa, p_conv, p_gate, conv_w, conv_b, w_na, w_gqa, w_conv, layer, seq):
    t = a_na.shape[0]
    ch = conv_w.shape[2]
    d = w_na.shape[1]
    assert CONV_WIDTH == 3
    tm = _pick(seq, TOKENS_F32_BLOCK)
    per_halo = tm // CONV_HALO
    last_halo = t // CONV_HALO - 1
    prev_map = lambda col: (lambda i: (jnp.maximum(i * per_halo - 1, 0), col))
    next_map = lambda col: (lambda i: (jnp.minimum((i + 1) * per_halo, last_halo), col))
    resident = lambda shape: pl.BlockSpec(shape, lambda i: (0, 0), pipeline_mode=pl.Buffered(1))
    return pl.pallas_call(
        functools.partial(_branch_merge_kernel, blocks_per_seq=seq // tm,
                          col_chunk=_pick(d, MERGE_COL_CHUNK)),
        out_shape=jax.ShapeDtypeStruct((t, d), BF16),
        grid=(t // tm,),
        in_specs=[
            pl.BlockSpec((tm, NA_WIDTH), lambda i: (i, 0)),
            pl.BlockSpec((tm, GQA_Q_WIDTH), lambda i: (i, 0)),
            pl.BlockSpec((tm, ch), lambda i: (i, 0)),
            pl.BlockSpec((tm, ch), lambda i: (i, 1)),
            pl.BlockSpec((tm, ch), lambda i: (i, 2)),
            pl.BlockSpec((CONV_HALO, ch), prev_map(0)),
            pl.BlockSpec((CONV_HALO, ch), prev_map(2)),
            pl.BlockSpec((CONV_HALO, ch), next_map(0)),
            pl.BlockSpec((CONV_HALO, ch), next_map(2)),
            pl.BlockSpec((tm, d), lambda i: (i, 0)),
            pl.BlockSpec((tm, d), lambda i: (i, 1)),
            pl.BlockSpec((tm, d), lambda i: (i, 2)),
            pl.BlockSpec((None, CONV_WIDTH, ch), lambda i: (layer, 0, 0)),
            pl.BlockSpec((None, 1, ch), lambda i: (layer, 0, 0)),
            resident((NA_WIDTH, d)),
            resident((GQA_Q_WIDTH, d)),
            resident((ch, d)),
        ],
        out_specs=pl.BlockSpec((tm, d), lambda i: (i, 0)),
        scratch_shapes=[pltpu.VMEM((tm, ch), BF16), pltpu.VMEM((tm, d), F32)],
        compiler_params=_params(("parallel",)),
        name="branch_merge",
    )(a_na, a_gqa, p_conv, p_conv, p_conv, p_conv, p_conv, p_conv, p_conv, p_gate, p_gate, p_gate,
      conv_w, conv_b, w_na, w_gqa, w_conv)


def _out_proj_kernel(x_ref, m_ref, w_ref, g_post_ref, g_next_ref, o_ref, h_ref):
    part = x_ref.shape[0] // OUT_PROJ_ROW_PARTS
    for r in range(OUT_PROJ_ROW_PARTS):
        rows = slice(r * part, (r + 1) * part)
        y = jnp.dot(m_ref[rows, :], w_ref[...], preferred_element_type=F32)
        x_new = x_ref[rows, :] + _rms_rows(y, g_post_ref[...])
        o_ref[rows, :] = x_new
        h_ref[rows, :] = _rms_rows(x_new, g_next_ref[...]).astype(h_ref.dtype)


def _out_proj(x, merged, w_out, post_gains, next_gains, layer):
    t, d = x.shape
    tm = _pick(t, TOKENS_F32_BLOCK)
    gain = lambda: pl.BlockSpec((None, 1, d), lambda i: (layer, 0, 0))
    return pl.pallas_call(
        _out_proj_kernel,
        out_shape=(jax.ShapeDtypeStruct((t, d), F32), jax.ShapeDtypeStruct((t, d), BF16)),
        grid=(t // tm,),
        in_specs=[
            pl.BlockSpec((tm, d), lambda i: (i, 0)),
            pl.BlockSpec((tm, d), lambda i: (i, 0)),
            pl.BlockSpec((d, d), lambda i: (0, 0), pipeline_mode=pl.Buffered(1)),
            gain(),
            gain(),
        ],
        out_specs=(pl.BlockSpec((tm, d), lambda i: (i, 0)), pl.BlockSpec((tm, d), lambda i: (i, 0))),
        compiler_params=_params(("parallel",)),
        name="out_proj_norm",
    )(x, merged, w_out, post_gains, next_gains)


def _ffn_kernel(x_ref, h_ref, wg_ref, wu_ref, wd_ref, g_post_ref, *refs, emit_next):
    (g_next_ref, o_ref, hn_ref, acc_ref) = refs if emit_next else (None, refs[0], None, refs[1])
    f = pl.program_id(1)

    @pl.when(f == 0)
    def _():
        acc_ref[...] = jnp.zeros_like(acc_ref)

    h = h_ref[...]
    a = jnp.dot(h, wg_ref[...], preferred_element_type=F32)
    b = jnp.dot(h, wu_ref[...], preferred_element_type=F32)
    act = (a * _sigmoid(a) * b).astype(BF16)
    acc_ref[...] += jnp.dot(act, wd_ref[...], preferred_element_type=F32)

    @pl.when(f == pl.num_programs(1) - 1)
    def _():
        x_new = x_ref[...] + _rms_rows(acc_ref[...], g_post_ref[...])
        o_ref[...] = x_new
        if emit_next:
            hn_ref[...] = _rms_rows(x_new, g_next_ref[...]).astype(hn_ref.dtype)


def _ffn(x, h, wg, wu, wd, post_gains, next_gains, layer, next_layer):
    t, d = x.shape
    hidden = wg.shape[1]
    tm = _pick(t, TOKENS_F32_BLOCK)
    tf = _pick(hidden, FFN_HIDDEN_CHUNK)
    emit_next = next_layer is not None
    token_block = lambda: pl.BlockSpec((tm, d), lambda i, f: (i, 0))
    gain = lambda l: pl.BlockSpec((None, 1, d), lambda i, f: (l, 0, 0))
    outs = pl.pallas_call(
        functools.partial(_ffn_kernel, emit_next=emit_next),
        out_shape=[jax.ShapeDtypeStruct((t, d), F32)] + [jax.ShapeDtypeStruct((t, d), BF16)] * emit_next,
        grid=(t // tm, hidden // tf),
        in_specs=[
            token_block(),
            token_block(),
            pl.BlockSpec((d, tf), lambda i, f: (0, f)),
            pl.BlockSpec((d, tf), lambda i, f: (0, f)),
            pl.BlockSpec((tf, d), lambda i, f: (f, 0)),
            gain(layer),
        ] + [gain(next_layer)] * emit_next,
        out_specs=[token_block()] + [token_block()] * emit_next,
        scratch_shapes=[pltpu.VMEM((tm, d), F32)],
        compiler_params=_params(("parallel", "arbitrary")),
        name="ffn_swiglu",
    )(x, h, wg, wu, wd, post_gains, *([next_gains] * emit_next))
    return (outs[0], outs[1]) if emit_next else (outs[0], None)


def kernel(x, w_in, na_rpb, q_norm_g, k_norm_g, conv_w, conv_b, w_br_na, w_br_gqa, w_br_conv, w_out,
           pre_mix_g, post_mix_g, pre_ffn_g, post_ffn_g, w_ffn_gate, w_ffn_up, w_ffn_down):
    batch, seq, d = x.shape
    depth = w_in.shape[0]
    ch = conv_w.shape[2]
    cos, sin = _rope_tables(seq)

    gain3 = lambda g: g.astype(F32).reshape(depth, 1, -1)
    pre_mix_g, post_mix_g, pre_ffn_g, post_ffn_g = map(gain3, (pre_mix_g, post_mix_g, pre_ffn_g, post_ffn_g))
    conv_w = conv_w.astype(F32)
    conv_b = conv_b.astype(F32).reshape(depth, 1, ch)
    q_gain = q_norm_g.astype(F32) * (HEAD_DIM ** -0.5 * LOG2E)
    qk_gains = jnp.concatenate([jnp.tile(q_gain, (1, GQA_Q_HEADS)),
                                jnp.tile(k_norm_g.astype(F32), (1, GQA_KV_HEADS))], axis=1)
    qk_gains = qk_gains.reshape(depth, -1, 1, 2 * HEAD_DIM)

    y = x.reshape(batch * seq, d)
    h = _pre_norm(y, pre_mix_g, 0)
    for l in range(depth):
        p_attn, w_na, w_gqa, w_conv, w_o = _in_proj(h, w_in, l, 0, OFF_CONV, "in_proj_attn",
                                                    side=(w_br_na, w_br_gqa, w_br_conv, w_out))
        p_conv, wd = _in_proj(h, w_in, l, OFF_CONV, 3 * ch, "in_proj_conv", side=(w_ffn_down,))
        p_gate, wg, wu = _in_proj(h, w_in, l, OFF_CONV + 3 * ch, 3 * d, "in_proj_gate",
                                  side=(w_ffn_gate, w_ffn_up))

        qk = _qk_prep(p_attn, qk_gains[l], cos, sin, seq)
        plain_exp = (_gqa_logit_bound(q_gain[l], k_norm_g[l].astype(F32)) <= GQA_PLAIN_EXP_BOUND)
        a_gqa = _gqa_attention(qk, p_attn, plain_exp.astype(jnp.int32).reshape(1), batch, seq)
        a_na = _na_attention(p_attn, na_rpb[l], batch, seq)

        merged = _branch_merge(a_na, a_gqa, p_conv, p_gate, conv_w, conv_b, w_na, w_gqa, w_conv, l, seq)
        y, h = _out_proj(y, merged, w_o, post_mix_g, pre_ffn_g, l)
        y, h = _ffn(y, h, wg, wu, wd, post_ffn_g, pre_mix_g, l, l + 1 if l + 1 < depth else None)
    return y.reshape(batch, seq, d)
```

```python
import functools
import math

import numpy as np
import jax
import jax.numpy as jnp
from jax import lax
from jax.experimental import pallas as pl
from jax.experimental.pallas import tpu as pltpu

GRID_W = 64
HEAD_DIM = 128
NA_HEADS = 8
NA_WIN_ROWS = 8
NA_WIN_COLS = 16
GQA_Q_HEADS = 8
GQA_KV_HEADS = 2
GQA_GROUP = GQA_Q_HEADS // GQA_KV_HEADS
ROPE_THETA = 10000.0
CONV_WIDTH = 3
RMS_EPS = 1e-6

NA_WIDTH = NA_HEADS * HEAD_DIM
GQA_Q_WIDTH = GQA_Q_HEADS * HEAD_DIM
GQA_KV_WIDTH = GQA_KV_HEADS * HEAD_DIM
OFF_NA_Q = 0
OFF_NA_K = NA_WIDTH
OFF_NA_V = 2 * NA_WIDTH
OFF_GQA_Q = 3 * NA_WIDTH
OFF_GQA_K = OFF_GQA_Q + GQA_Q_WIDTH
OFF_GQA_V = OFF_GQA_K + GQA_KV_WIDTH
OFF_CONV = OFF_GQA_V + GQA_KV_WIDTH

MASK_VALUE = -1e30
LOG2E = math.log2(math.e)

V7X_VMEM_BYTES = 64 * 1024 * 1024
VMEM_LIMIT = V7X_VMEM_BYTES * 7 // 8

TOKENS_F32_BLOCK = (512, 256)
TOKENS_IN_PROJ = (1024, 512, 256)
TOKENS_QK_PREP = (2048, 1024, 512, 256)
IN_PROJ_COLS = 1536
WEIGHT_CAST_ROWS = 256
GQA_QUERY_BLOCK = (512, 256, 128)
GQA_KEY_CHUNK = (512, 256, 128)
GQA_CHUNKS_PER_TRIP = (4, 2)
NA_GROUP_ROWS = 4
NA_BAND_ROWS = NA_GROUP_ROWS + NA_WIN_ROWS
NA_GROUPS_PER_STEP = (8, 2, 1)
MERGE_COL_CHUNK = (512, 256, 128)
OUT_PROJ_ROW_PARTS = 4
FFN_HIDDEN_CHUNK = (512, 256, 128)

BF16 = jnp.bfloat16
F32 = jnp.float32
BF16_SUBLANES = 16


def _params(semantics):
    return pltpu.CompilerParams(dimension_semantics=semantics, vmem_limit_bytes=VMEM_LIMIT)


def _pick(n, candidates):
    for c in candidates:
        if n % c == 0:
            return c
    raise ValueError(f"no tile in {candidates} divides {n}")


def _sigmoid(x):
    return 0.5 * jnp.tanh(0.5 * x) + 0.5


def _rms_rows(x, g):
    ms = jnp.mean(x * x, axis=-1, keepdims=True)
    return x * lax.rsqrt(ms + RMS_EPS) * g


def _norm_kernel(x_ref, g_ref, o_ref):
    o_ref[...] = _rms_rows(x_ref[...], g_ref[...]).astype(o_ref.dtype)


def _pre_norm(x, gains, layer):
    t, d = x.shape
    tm = _pick(t, TOKENS_F32_BLOCK)
    return pl.pallas_call(
        _norm_kernel,
        out_shape=jax.ShapeDtypeStruct((t, d), BF16),
        grid=(t // tm,),
        in_specs=[
            pl.BlockSpec((tm, d), lambda i: (i, 0)),
            pl.BlockSpec((None, 1, d), lambda i: (layer, 0, 0)),
        ],
        out_specs=pl.BlockSpec((tm, d), lambda i: (i, 0)),
        compiler_params=_params(("parallel",)),
        name="pre_norm",
    )(x, gains)


def _proj_kernel(h_ref, w_ref, *refs, row_chunk, n_side):
    side_in, o_ref, side_out, wb_ref = refs[:n_side], refs[n_side], refs[n_side + 1:-1], refs[-1]

    @pl.when(pl.program_id(1) == 0)
    def _():
        def body(c, carry):
            rows = pl.ds(pl.multiple_of(c * row_chunk, row_chunk), row_chunk)
            wb_ref[rows, :] = w_ref[rows, :].astype(BF16)
            return carry
        lax.fori_loop(0, w_ref.shape[0] // row_chunk, body, 0)

    o_ref[...] = jnp.dot(h_ref[...], wb_ref[...], preferred_element_type=F32).astype(o_ref.dtype)
    for src, dst in zip(side_in, side_out):
        dst[...] = src[...].astype(dst.dtype)


def _in_proj(h, w_in, layer, col_off, n_cols, name, side=()):
    t, d = h.shape
    tm = _pick(t, TOKENS_IN_PROJ)
    tn = IN_PROJ_COLS
    assert col_off % tn == 0 and n_cols % tn == 0
    col0 = col_off // tn
    n_i = t // tm
    n_steps = (n_cols // tn) * n_i
    side_specs, side_out_specs, side_shapes = [], [], []
    for w in side:
        rows, cols = w.shape[1:]
        slab = next(s for s in range(BF16_SUBLANES, rows + 1, BF16_SUBLANES)
                    if rows % s == 0 and rows // s <= n_steps)
        last = rows // slab - 1
        side_specs.append(pl.BlockSpec(
            (None, slab, cols), lambda j, i, last=last: (layer, jnp.minimum(j * n_i + i, last), 0)))
        side_out_specs.append(pl.BlockSpec(
            (slab, cols), lambda j, i, last=last: (jnp.minimum(j * n_i + i, last), 0)))
        side_shapes.append(jax.ShapeDtypeStruct((rows, cols), BF16))
    outs = pl.pallas_call(
        functools.partial(_proj_kernel, row_chunk=min(d, WEIGHT_CAST_ROWS), n_side=len(side)),
        out_shape=[jax.ShapeDtypeStruct((t, n_cols), BF16)] + side_shapes,
        grid=(n_cols // tn, n_i),
        in_specs=[
            pl.BlockSpec((tm, d), lambda j, i: (i, 0)),
            pl.BlockSpec((None, d, tn), lambda j, i: (layer, 0, col0 + j)),
        ] + side_specs,
        out_specs=[pl.BlockSpec((tm, tn), lambda j, i: (i, j))] + side_out_specs,
        scratch_shapes=[pltpu.VMEM((d, tn), BF16)],
        compiler_params=_params(("arbitrary", "arbitrary")),
        name=name,
    )(h, w_in, *side)
    return outs[0] if not side else outs


def _rope_tables(seq):
    half = HEAD_DIM // 2
    quarter = half // 2
    inv_freq = 1.0 / (ROPE_THETA ** (np.arange(quarter, dtype=np.float64) / quarter))
    t = np.arange(seq)
    ang_r = (t // GRID_W)[:, None] * inv_freq[None, :]
    ang_c = (t % GRID_W)[:, None] * inv_freq[None, :]
    cos = np.concatenate([np.cos(ang_r), np.cos(ang_r), np.cos(ang_c), np.cos(ang_c)], axis=1)
    sin = np.concatenate([-np.sin(ang_r), np.sin(ang_r), -np.sin(ang_c), np.sin(ang_c)], axis=1)
    return jnp.asarray(cos, F32), jnp.asarray(sin, F32)


def _qk_prep_kernel(x_ref, g_ref, cos_ref, sin_ref, o_ref):
    heads = x_ref.shape[1] // HEAD_DIM
    quarter = HEAD_DIM // 4
    cos = cos_ref[...]
    sin = sin_ref[...]
    lane = lax.broadcasted_iota(jnp.int32, cos.shape, 1)
    first = (lane % (2 * quarter)) < quarter
    ones = jnp.ones((HEAD_DIM, HEAD_DIM), BF16)
    for h in range(heads):
        cols = slice(h * HEAD_DIM, (h + 1) * HEAD_DIM)
        x = x_ref[:, cols].astype(F32)
        ssq = jnp.dot((x * x).astype(BF16), ones, preferred_element_type=F32)
        y = x * lax.rsqrt(ssq * (1.0 / HEAD_DIM) + RMS_EPS) * g_ref[0, :, cols]
        partner = jnp.where(first, pltpu.roll(y, HEAD_DIM - quarter, 1), pltpu.roll(y, quarter, 1))
        o_ref[:, cols] = (y * cos + partner * sin).astype(o_ref.dtype)


def _qk_prep(proj, gains, cos, sin, seq):
    t = proj.shape[0]
    n_heads = GQA_Q_HEADS + GQA_KV_HEADS
    width = 2 * HEAD_DIM
    assert OFF_GQA_Q % width == 0 and n_heads % 2 == 0
    tm = _pick(seq, TOKENS_QK_PREP)
    blocks_per_seq = seq // tm
    col0 = OFF_GQA_Q // width
    return pl.pallas_call(
        _qk_prep_kernel,
        out_shape=jax.ShapeDtypeStruct((t, n_heads * HEAD_DIM), BF16),
        grid=(t // tm, n_heads // 2),
        in_specs=[
            pl.BlockSpec((tm, width), lambda i, j: (i, col0 + j)),
            pl.BlockSpec((1, 1, width), lambda i, j: (j, 0, 0)),
            pl.BlockSpec((tm, HEAD_DIM), lambda i, j: (i % blocks_per_seq, 0)),
            pl.BlockSpec((tm, HEAD_DIM), lambda i, j: (i % blocks_per_seq, 0)),
        ],
        out_specs=pl.BlockSpec((tm, width), lambda i, j: (i, j)),
        compiler_params=_params(("parallel", "parallel")),
        name="gqa_qk_prep",
    )(proj, gains, cos, sin)


GQA_ONES_ROWS = 16
GQA_PLAIN_EXP_BOUND = 64.0


def _gqa_logit_bound(q_gain, k_gain):
    return 1.02 * HEAD_DIM * jnp.max(jnp.abs(q_gain)) * jnp.max(jnp.abs(k_gain))


def _gqa_kernel(plain_ref, q_ref, k_ref, v_ref, o_ref, qs_ref, vt_ref, acc_ref, s_ref, m_ref, *, tk):
    tq = q_ref.shape[0]
    seq = k_ref.shape[0]
    n_chunks = seq // tk
    assert n_chunks % 2 == 0

    def chunk(c):
        return pl.ds(pl.multiple_of(c * tk, tk), tk)

    @pl.when(pl.program_id(2) == 0)
    def _():
        def body(c, carry):
            vt_ref[c, :HEAD_DIM, :] = v_ref[chunk(c), :].astype(F32).T.astype(BF16)
            vt_ref[c, HEAD_DIM:, :] = jnp.ones((GQA_ONES_ROWS, tk), BF16)
            return carry
        lax.fori_loop(0, n_chunks, body, 0)

    for h in range(GQA_GROUP):
        qs_ref[h * tq:(h + 1) * tq, :] = q_ref[:, h * HEAD_DIM:(h + 1) * HEAD_DIM]
    acc_ref[...] = jnp.zeros_like(acc_ref)

    def scores(c):
        return lax.dot_general(k_ref[chunk(c), :], qs_ref[...], (((1,), (1,)), ((), ())),
                               preferred_element_type=F32)

    @pl.when(plain_ref[0] != 0)
    def _():
        per_trip = _pick(n_chunks, GQA_CHUNKS_PER_TRIP)

        def body(t, carry):
            pv = None
            for u in range(per_trip):
                c = per_trip * t + u
                p = jnp.exp2(scores(c)).astype(BF16)
                part = jnp.dot(vt_ref[c], p, preferred_element_type=F32)
                pv = part if pv is None else pv + part
            acc_ref[...] += pv
            return carry
        lax.fori_loop(0, n_chunks // per_trip, body, 0)

    @pl.when(plain_ref[0] == 0)
    def _():
        m_ref[...] = jnp.full_like(m_ref, -jnp.inf)

        def accumulate(c, slot):
            s = s_ref[slot]
            m_old = m_ref[...]
            m_new = jnp.maximum(m_old, jnp.max(s, axis=0, keepdims=True))
            m_ref[...] = m_new
            p = jnp.exp2(s - m_new).astype(BF16)
            pv = jnp.dot(vt_ref[c], p, preferred_element_type=F32)
            acc_ref[...] = acc_ref[...] * jnp.exp2(m_old - m_new) + pv

        s_ref[0] = scores(0)

        def body(t, carry):
            s_ref[1] = scores(2 * t + 1)
            accumulate(2 * t, 0)
            s_ref[0] = scores(2 * t + 2)
            accumulate(2 * t + 1, 1)
            return carry

        lax.fori_loop(0, n_chunks // 2 - 1, body, 0)
        s_ref[1] = scores(n_chunks - 1)
        accumulate(n_chunks - 2, 0)
        accumulate(n_chunks - 1, 1)

    out_t = acc_ref[:HEAD_DIM, :] / acc_ref[HEAD_DIM:HEAD_DIM + 1, :]
    for h in range(GQA_GROUP):
        o_ref[:, h * HEAD_DIM:(h + 1) * HEAD_DIM] = out_t[:, h * tq:(h + 1) * tq].T.astype(o_ref.dtype)


def _gqa_attention(qk, proj, plain_exp, batch, seq):
    t = qk.shape[0]
    tq = _pick(seq, GQA_QUERY_BLOCK)
    tk = _pick(seq, GQA_KEY_CHUNK)
    gw = GQA_GROUP * HEAD_DIM
    k_col0 = GQA_Q_WIDTH // HEAD_DIM
    v_col0 = OFF_GQA_V // HEAD_DIM
    q_blocks = seq // tq
    grid_spec = pltpu.PrefetchScalarGridSpec(
        num_scalar_prefetch=1,
        grid=(batch, GQA_KV_HEADS, q_blocks),
        in_specs=[
            pl.BlockSpec((tq, gw), lambda b, g, i, flag: (b * q_blocks + i, g)),
            pl.BlockSpec((seq, HEAD_DIM), lambda b, g, i, flag: (b, k_col0 + g)),
            pl.BlockSpec((seq, HEAD_DIM), lambda b, g, i, flag: (b, v_col0 + g)),
        ],
        out_specs=pl.BlockSpec((tq, gw), lambda b, g, i, flag: (b * q_blocks + i, g)),
        scratch_shapes=[
            pltpu.VMEM((GQA_GROUP * tq, HEAD_DIM), BF16),
            pltpu.VMEM((seq // tk, HEAD_DIM + GQA_ONES_ROWS, tk), BF16),
            pltpu.VMEM((HEAD_DIM + GQA_ONES_ROWS, GQA_GROUP * tq), F32),
            pltpu.VMEM((2, tk, GQA_GROUP * tq), F32),
            pltpu.VMEM((1, GQA_GROUP * tq), F32),
        ],
    )
    return pl.pallas_call(
        functools.partial(_gqa_kernel, tk=tk),
        out_shape=jax.ShapeDtypeStruct((t, GQA_Q_WIDTH), BF16),
        grid_spec=grid_spec,
        compiler_params=_params(("parallel", "parallel", "arbitrary")),
        name="gqa_flash",
    )(plain_exp, qk, qk, proj)


NA_PAIR_ROWS = 2 * NA_WIN_ROWS
NA_MASK_BOTH, NA_MASK_LEFT, NA_MASK_RIGHT = 0, 1, 2


def _na_block_plan(rows):
    gr, br = NA_GROUP_ROWS, NA_BAND_ROWS
    plan = []
    for r0 in (0, min(gr, rows - gr), rows - gr):
        band_start = int(np.clip(r0 - NA_WIN_ROWS // 2, 0, rows - br))
        per_row = []
        for i in range(gr):
            r = r0 + i
            row_start = int(np.clip(r - NA_WIN_ROWS // 2, 0, rows - NA_WIN_ROWS))
            blocks = []
            for jp in range(br // 2):
                key_row = band_start + 2 * jp
                ok_l = row_start <= key_row < row_start + NA_WIN_ROWS
                ok_r = row_start <= key_row + 1 < row_start + NA_WIN_ROWS
                if not (ok_l or ok_r):
                    blocks.append(None)
                    continue
                kind = NA_MASK_BOTH if (ok_l and ok_r) else (NA_MASK_LEFT if ok_l else NA_MASK_RIGHT)
                blocks.append((key_row - r + NA_WIN_ROWS, kind))
            per_row.append(blocks)
        plan.append(per_row)
    return plan


def _na_pair_table(rpb):
    h, nr, nc = rpb.shape
    rpb = jnp.pad(rpb.astype(F32), ((0, 0), (0, 0), (0, GRID_W - nc)))
    zero = jnp.zeros((h, 1, GRID_W), F32)
    left = jnp.concatenate([zero, rpb], axis=1)
    right = jnp.concatenate([rpb, zero], axis=1)
    return jnp.concatenate([left, right], axis=2)


def _na_col_masks():
    w = np.arange(GRID_W)[:, None]
    c = np.arange(2 * GRID_W)[None, :] % GRID_W
    col_start = np.clip(w - NA_WIN_COLS // 2, 0, GRID_W - NA_WIN_COLS)
    base = np.where((c >= col_start) & (c < col_start + NA_WIN_COLS), 0.0, MASK_VALUE)
    left_half = np.arange(2 * GRID_W)[None, :] < GRID_W
    masks = np.stack([base, np.where(left_half, base, MASK_VALUE), np.where(left_half, MASK_VALUE, base)])
    return jnp.asarray(masks, F32)


def _na_kernel(q_ref, k_ref, v_ref, pair_ref, mask_ref, o_ref, tile_ref, table_ref, vaug_ref, *, rows):
    g = pl.program_id(2)
    n_groups = rows // NA_GROUP_ROWS
    blk_w = 2 * GRID_W

    @pl.when(g == 0)
    def _():
        for p in range(NA_PAIR_ROWS):
            row = jnp.broadcast_to(pair_ref[p:p + 1, :], (GRID_W, blk_w))
            tile_ref[p] = pltpu.roll(row, blk_w - (NA_WIN_COLS - 1), 1, stride=1, stride_axis=0)
        for v, per_row in enumerate(_na_block_plan(rows)):
            for i, blocks in enumerate(per_row):
                for jp, blk in enumerate(blocks):
                    dst = (v, slice(i * GRID_W, (i + 1) * GRID_W), slice(jp * blk_w, (jp + 1) * blk_w))
                    if blk is None:
                        table_ref[dst] = jnp.full((GRID_W, blk_w), MASK_VALUE, F32)
                    else:
                        table_ref[dst] = (tile_ref[blk[0]] + mask_ref[blk[1]]) * LOG2E
        vaug_ref[:, :HEAD_DIM] = v_ref[...]
        vaug_ref[:, HEAD_DIM:] = jnp.ones(v_ref.shape, BF16)

    gq = NA_GROUP_ROWS * GRID_W
    band_rows = NA_BAND_ROWS * GRID_W
    for u in range(q_ref.shape[0] // gq):
        gi = g * (q_ref.shape[0] // gq) + u
        start_row = jnp.clip(gi * NA_GROUP_ROWS - NA_WIN_ROWS // 2, 0, rows - NA_BAND_ROWS)
        band = pl.ds(pl.multiple_of(start_row * GRID_W, GRID_W), band_rows)
        variant = jnp.where(gi == 0, 0, jnp.where(gi == n_groups - 1, 2, 1))
        s = lax.dot_general(q_ref[u * gq:(u + 1) * gq, :], k_ref[band, :], (((1,), (1,)), ((), ())),
                            preferred_element_type=F32)
        s = s * (HEAD_DIM ** -0.5 * LOG2E) + table_ref[variant]
        p = jnp.exp2(s - jnp.max(s, axis=-1, keepdims=True)).astype(BF16)
        o = jnp.dot(p, vaug_ref[band, :], preferred_element_type=F32)
        o_ref[u * gq:(u + 1) * gq, :] = (o[:, :HEAD_DIM] / o[:, HEAD_DIM:]).astype(o_ref.dtype)


def _na_attention(proj, rpb, batch, seq):
    t = proj.shape[0]
    rows = seq // GRID_W
    assert NA_GROUP_ROWS % 4 == 0 and rows % NA_GROUP_ROWS == 0 and rows >= NA_BAND_ROWS
    assert 2 * GRID_W == HEAD_DIM
    n_groups = rows // NA_GROUP_ROWS
    gq = NA_GROUP_ROWS * GRID_W
    gk = NA_BAND_ROWS * GRID_W
    per_step = _pick(n_groups, NA_GROUPS_PER_STEP)
    n_steps = n_groups // per_step
    k_col0 = OFF_NA_K // HEAD_DIM
    v_col0 = OFF_NA_V // HEAD_DIM
    return pl.pallas_call(
        functools.partial(_na_kernel, rows=rows),
        out_shape=jax.ShapeDtypeStruct((t, NA_WIDTH), BF16),
        grid=(batch, NA_HEADS, n_steps),
        in_specs=[
            pl.BlockSpec((per_step * gq, HEAD_DIM), lambda b, h, g: (b * n_steps + g, h)),
            pl.BlockSpec((seq, HEAD_DIM), lambda b, h, g: (b, k_col0 + h)),
            pl.BlockSpec((seq, HEAD_DIM), lambda b, h, g: (b, v_col0 + h)),
            pl.BlockSpec((None, NA_PAIR_ROWS, 2 * GRID_W), lambda b, h, g: (h, 0, 0)),
            pl.BlockSpec((3, GRID_W, 2 * GRID_W), lambda b, h, g: (0, 0, 0)),
        ],
        out_specs=pl.BlockSpec((per_step * gq, HEAD_DIM), lambda b, h, g: (b * n_steps + g, h)),
        scratch_shapes=[
            pltpu.VMEM((NA_PAIR_ROWS, GRID_W, 2 * GRID_W), F32),
            pltpu.VMEM((3, gq, gk), F32),
            pltpu.VMEM((seq, 2 * HEAD_DIM), BF16),
        ],
        compiler_params=_params(("parallel", "parallel", "arbitrary")),
        name="na_attention",
    )(proj, proj, proj, _na_pair_table(rpb), _na_col_masks())


CONV_HALO = 16


def _branch_merge_kernel(a_na_ref, a_gqa_ref, h_ref, b_ref, c_ref, hp_ref, cp_ref, hn_ref, cn_ref,
                         g_na_ref, g_gqa_ref, g_conv_ref, cw_ref, cb_ref, w_na_ref, w_gqa_ref,
                         w_conv_ref, o_ref, a_conv_ref, part_ref, *, blocks_per_seq, col_chunk):
    tm = h_ref.shape[0]
    pos = pl.program_id(0) % blocks_per_seq
    n_chunks = o_ref.shape[1] // col_chunk

    def branch(a_ref, w_ref, gate_ref, cols):
        y = jnp.dot(a_ref[...], w_ref[:, cols], preferred_element_type=F32)
        return _sigmoid(gate_ref[:, cols].astype(F32)) * y

    for c in range(n_chunks):
        cols = slice(c * col_chunk, (c + 1) * col_chunk)
        part_ref[:, cols] = (branch(a_na_ref, w_na_ref, g_na_ref, cols)
                             + branch(a_gqa_ref, w_gqa_ref, g_gqa_ref, cols))

    u = c_ref[...].astype(F32) * h_ref[...].astype(F32)
    u_prev_row = cp_ref[CONV_HALO - 1:CONV_HALO, :].astype(F32) * hp_ref[CONV_HALO - 1:CONV_HALO, :].astype(F32)
    u_prev_row = jnp.where(pos == 0, 0.0, u_prev_row)
    u_next_row = cn_ref[0:1, :].astype(F32) * hn_ref[0:1, :].astype(F32)
    u_next_row = jnp.where(pos == blocks_per_seq - 1, 0.0, u_next_row)
    row = lax.broadcasted_iota(jnp.int32, u.shape, 0)
    u_prev = jnp.where(row == 0, u_prev_row, pltpu.roll(u, 1, 0))
    u_next = jnp.where(row == tm - 1, u_next_row, pltpu.roll(u, tm - 1, 0))
    y = cb_ref[...] + u_prev * cw_ref[0:1, :] + u * cw_ref[1:2, :] + u_next * cw_ref[2:3, :]
    a_conv_ref[...] = (b_ref[...].astype(F32) * y).astype(BF16)

    for c in range(n_chunks):
        cols = slice(c * col_chunk, (c + 1) * col_chunk)
        merged = part_ref[:, cols] + branch(a_conv_ref, w_conv_ref, g_conv_ref, cols)
        o_ref[:, cols] = merged.astype(o_ref.dtype)


def _branch_merge(a_na, a_gqa, p_conv, p_gate, conv_w, conv_b, w_na, w_gqa, w_conv, layer, seq):
    t = a_na.shape[0]
    ch = conv_w.shape[2]
    d = w_na.shape[1]
    assert CONV_WIDTH == 3
    tm = _pick(seq, TOKENS_F32_BLOCK)
    per_halo = tm // CONV_HALO
    last_halo = t // CONV_HALO - 1
    prev_map = lambda col: (lambda i: (jnp.maximum(i * per_halo - 1, 0), col))
    next_map = lambda col: (lambda i: (jnp.minimum((i + 1) * per_halo, last_halo), col))
    resident = lambda shape: pl.BlockSpec(shape, lambda i: (0, 0), pipeline_mode=pl.Buffered(1))
    return pl.pallas_call(
        functools.partial(_branch_merge_kernel, blocks_per_seq=seq // tm,
                          col_chunk=_pick(d, MERGE_COL_CHUNK)),
        out_shape=jax.ShapeDtypeStruct((t, d), BF16),
        grid=(t // tm,),
        in_specs=[
            pl.BlockSpec((tm, NA_WIDTH), lambda i: (i, 0)),
            pl.BlockSpec((tm, GQA_Q_WIDTH), lambda i: (i, 0)),
            pl.BlockSpec((tm, ch), lambda i: (i, 0)),
            pl.BlockSpec((tm, ch), lambda i: (i, 1)),
            pl.BlockSpec((tm, ch), lambda i: (i, 2)),
            pl.BlockSpec((CONV_HALO, ch), prev_map(0)),
            pl.BlockSpec((CONV_HALO, ch), prev_map(2)),
            pl.BlockSpec((CONV_HALO, ch), next_map(0)),
            pl.BlockSpec((CONV_HALO, ch), next_map(2)),
            pl.BlockSpec((tm, d), lambda i: (i, 0)),
            pl.BlockSpec((tm, d), lambda i: (i, 1)),
            pl.BlockSpec((tm, d), lambda i: (i, 2)),
            pl.BlockSpec((None, CONV_WIDTH, ch), lambda i: (layer, 0, 0)),
            pl.BlockSpec((None, 1, ch), lambda i: (layer, 0, 0)),
            resident((NA_WIDTH, d)),
            resident((GQA_Q_WIDTH, d)),
            resident((ch, d)),
        ],
        out_specs=pl.BlockSpec((tm, d), lambda i: (i, 0)),
        scratch_shapes=[pltpu.VMEM((tm, ch), BF16), pltpu.VMEM((tm, d), F32)],
        compiler_params=_params(("parallel",)),
        name="branch_merge",
    )(a_na, a_gqa, p_conv, p_conv, p_conv, p_conv, p_conv, p_conv, p_conv, p_gate, p_gate, p_gate,
      conv_w, conv_b, w_na, w_gqa, w_conv)


def _out_proj_kernel(x_ref, m_ref, w_ref, g_post_ref, g_next_ref, o_ref, h_ref):
    part = x_ref.shape[0] // OUT_PROJ_ROW_PARTS
    for r in range(OUT_PROJ_ROW_PARTS):
        rows = slice(r * part, (r + 1) * part)
        y = jnp.dot(m_ref[rows, :], w_ref[...], preferred_element_type=F32)
        x_new = x_ref[rows, :] + _rms_rows(y, g_post_ref[...])
        o_ref[rows, :] = x_new
        h_ref[rows, :] = _rms_rows(x_new, g_next_ref[...]).astype(h_ref.dtype)


def _out_proj(x, merged, w_out, post_gains, next_gains, layer):
    t, d = x.shape
    tm = _pick(t, TOKENS_F32_BLOCK)
    gain = lambda: pl.BlockSpec((None, 1, d), lambda i: (layer, 0, 0))
    return pl.pallas_call(
        _out_proj_kernel,
        out_shape=(jax.ShapeDtypeStruct((t, d), F32), jax.ShapeDtypeStruct((t, d), BF16)),
        grid=(t // tm,),
        in_specs=[
            pl.BlockSpec((tm, d), lambda i: (i, 0)),
            pl.BlockSpec((tm, d), lambda i: (i, 0)),
            pl.BlockSpec((d, d), lambda i: (0, 0), pipeline_mode=pl.Buffered(1)),
            gain(),
            gain(),
        ],
        out_specs=(pl.BlockSpec((tm, d), lambda i: (i, 0)), pl.BlockSpec((tm, d), lambda i: (i, 0))),
        compiler_params=_params(("parallel",)),
        name="out_proj_norm",
    )(x, merged, w_out, post_gains, next_gains)


def _ffn_kernel(x_ref, h_ref, wg_ref, wu_ref, wd_ref, g_post_ref, *refs, emit_next):
    (g_next_ref, o_ref, hn_ref, acc_ref) = refs if emit_next else (None, refs[0], None, refs[1])
    f = pl.program_id(1)

    @pl.when(f == 0)
    def _():
        acc_ref[...] = jnp.zeros_like(acc_ref)

    def down_proj(rows):
        h = h_ref[rows, :]
        a = jnp.dot(h, wg_ref[...], preferred_element_type=F32)
        b = jnp.dot(h, wu_ref[...], preferred_element_type=F32)
        act = (a * _sigmoid(a) * b).astype(BF16)
        return jnp.dot(act, wd_ref[...], preferred_element_type=F32)

    last = pl.num_programs(1) - 1

    @pl.when(f != last)
    def _():
        acc_ref[...] += down_proj(slice(None))

    @pl.when(f == last)
    def _():
        half = x_ref.shape[0] // 2
        for r in range(2):
            rows = slice(r * half, (r + 1) * half)
            x_new = x_ref[rows, :] + _rms_rows(acc_ref[rows, :] + down_proj(rows), g_post_ref[...])
            o_ref[rows, :] = x_new
            if emit_next:
                hn_ref[rows, :] = _rms_rows(x_new, g_next_ref[...]).astype(hn_ref.dtype)


def _ffn(x, h, wg, wu, wd, post_gains, next_gains, layer, next_layer):
    t, d = x.shape
    hidden = wg.shape[1]
    tm = _pick(t, TOKENS_F32_BLOCK)
    tf = _pick(hidden, FFN_HIDDEN_CHUNK)
    emit_next = next_layer is not None
    token_block = lambda: pl.BlockSpec((tm, d), lambda i, f: (i, 0))
    gain = lambda l: pl.BlockSpec((None, 1, d), lambda i, f: (l, 0, 0))
    outs = pl.pallas_call(
        functools.partial(_ffn_kernel, emit_next=emit_next),
        out_shape=[jax.ShapeDtypeStruct((t, d), F32)] + [jax.ShapeDtypeStruct((t, d), BF16)] * emit_next,
        grid=(t // tm, hidden // tf),
        in_specs=[
            token_block(),
            token_block(),
            pl.BlockSpec((d, tf), lambda i, f: (0, f)),
            pl.BlockSpec((d, tf), lambda i, f: (0, f)),
            pl.BlockSpec((tf, d), lambda i, f: (f, 0)),
            gain(layer),
        ] + [gain(next_layer)] * emit_next,
        out_specs=[token_block()] + [token_block()] * emit_next,
        scratch_shapes=[pltpu.VMEM((tm, d), F32)],
        compiler_params=_params(("parallel", "arbitrary")),
        name="ffn_swiglu",
    )(x, h, wg, wu, wd, post_gains, *([next_gains] * emit_next))
    return (outs[0], outs[1]) if emit_next else (outs[0], None)


def kernel(x, w_in, na_rpb, q_norm_g, k_norm_g, conv_w, conv_b, w_br_na, w_br_gqa, w_br_conv, w_out,
           pre_mix_g, post_mix_g, pre_ffn_g, post_ffn_g, w_ffn_gate, w_ffn_up, w_ffn_down):
    batch, seq, d = x.shape
    depth = w_in.shape[0]
    ch = conv_w.shape[2]
    cos, sin = _rope_tables(seq)

    gain3 = lambda g: g.astype(F32).reshape(depth, 1, -1)
    pre_mix_g, post_mix_g, pre_ffn_g, post_ffn_g = map(gain3, (pre_mix_g, post_mix_g, pre_ffn_g, post_ffn_g))
    conv_w = conv_w.astype(F32)
    conv_b = conv_b.astype(F32).reshape(depth, 1, ch)
    q_gain = q_norm_g.astype(F32) * (HEAD_DIM ** -0.5 * LOG2E)
    qk_gains = jnp.concatenate([jnp.tile(q_gain, (1, GQA_Q_HEADS)),
                                jnp.tile(k_norm_g.astype(F32), (1, GQA_KV_HEADS))], axis=1)
    qk_gains = qk_gains.reshape(depth, -1, 1, 2 * HEAD_DIM)

    y = x.reshape(batch * seq, d)
    h = _pre_norm(y, pre_mix_g, 0)
    for l in range(depth):
        p_attn, w_na, w_gqa, w_conv, w_o = _in_proj(h, w_in, l, 0, OFF_CONV, "in_proj_attn",
                                                    side=(w_br_na, w_br_gqa, w_br_conv, w_out))
        p_conv, wd = _in_proj(h, w_in, l, OFF_CONV, 3 * ch, "in_proj_conv", side=(w_ffn_down,))
        p_gate, wg, wu = _in_proj(h, w_in, l, OFF_CONV + 3 * ch, 3 * d, "in_proj_gate",
                                  side=(w_ffn_gate, w_ffn_up))

        qk = _qk_prep(p_attn, qk_gains[l], cos, sin, seq)
        plain_exp = (_gqa_logit_bound(q_gain[l], k_norm_g[l].astype(F32)) <= GQA_PLAIN_EXP_BOUND)
        a_gqa = _gqa_attention(qk, p_attn, plain_exp.astype(jnp.int32).reshape(1), batch, seq)
        a_na = _na_attention(p_attn, na_rpb[l], batch, seq)

        merged = _branch_merge(a_na, a_gqa, p_conv, p_gate, conv_w, conv_b, w_na, w_gqa, w_conv, l, seq)
        y, h = _out_proj(y, merged, w_o, post_mix_g, pre_ffn_g, l)
        y, h = _ffn(y, h, wg, wu, wd, post_ffn_g, pre_mix_g, l, l + 1 if l + 1 < depth else None)
    return y.reshape(batch, seq, d)
```

```python
import functools
import math

import numpy as np
import jax
import jax.numpy as jnp
from jax import lax
from jax.experimental import pallas as pl
from jax.experimental.pallas import tpu as pltpu

GRID_W = 64
HEAD_DIM = 128
NA_HEADS = 8
NA_WIN_ROWS = 8
NA_WIN_COLS = 16
GQA_Q_HEADS = 8
GQA_KV_HEADS = 2
GQA_GROUP = GQA_Q_HEADS // GQA_KV_HEADS
ROPE_THETA = 10000.0
CONV_WIDTH = 3
RMS_EPS = 1e-6

NA_WIDTH = NA_HEADS * HEAD_DIM
GQA_Q_WIDTH = GQA_Q_HEADS * HEAD_DIM
GQA_KV_WIDTH = GQA_KV_HEADS * HEAD_DIM
OFF_NA_Q = 0
OFF_NA_K = NA_WIDTH
OFF_NA_V = 2 * NA_WIDTH
OFF_GQA_Q = 3 * NA_WIDTH
OFF_GQA_K = OFF_GQA_Q + GQA_Q_WIDTH
OFF_GQA_V = OFF_GQA_K + GQA_KV_WIDTH
OFF_CONV = OFF_GQA_V + GQA_KV_WIDTH

MASK_VALUE = -1e30
LOG2E = math.log2(math.e)

V7X_VMEM_BYTES = 64 * 1024 * 1024
VMEM_LIMIT = V7X_VMEM_BYTES * 7 // 8

TOKENS_F32_BLOCK = (512, 256)
TOKENS_IN_PROJ = (1024, 512, 256)
TOKENS_QK_PREP = (2048, 1024, 512, 256)
IN_PROJ_COLS = 1536
WEIGHT_CAST_ROWS = 256
GQA_QUERY_BLOCK = (512, 256, 128)
GQA_KEY_CHUNK = (512, 256, 128)
GQA_CHUNKS_PER_TRIP = (4, 2)
NA_GROUP_ROWS = 4
NA_BAND_ROWS = NA_GROUP_ROWS + NA_WIN_ROWS
NA_GROUPS_PER_STEP = (8, 2, 1)
MERGE_COL_CHUNK = (512, 256, 128)
OUT_PROJ_ROW_PARTS = 4
FFN_HIDDEN_CHUNK = (512, 256, 128)

BF16 = jnp.bfloat16
F32 = jnp.float32
BF16_SUBLANES = 16


def _params(semantics):
    return pltpu.CompilerParams(dimension_semantics=semantics, vmem_limit_bytes=VMEM_LIMIT)


def _pick(n, candidates):
    for c in candidates:
        if n % c == 0:
            return c
    raise ValueError(f"no tile in {candidates} divides {n}")


def _sigmoid(x):
    return 0.5 * jnp.tanh(0.5 * x) + 0.5


def _rms_rows(x, g):
    ms = jnp.mean(x * x, axis=-1, keepdims=True)
    return x * lax.rsqrt(ms + RMS_EPS) * g


def _norm_kernel(x_ref, g_ref, o_ref):
    o_ref[...] = _rms_rows(x_ref[...], g_ref[...]).astype(o_ref.dtype)


def _pre_norm(x, gains, layer):
    t, d = x.shape
    tm = _pick(t, TOKENS_F32_BLOCK)
    return pl.pallas_call(
        _norm_kernel,
        out_shape=jax.ShapeDtypeStruct((t, d), BF16),
        grid=(t // tm,),
        in_specs=[
            pl.BlockSpec((tm, d), lambda i: (i, 0)),
            pl.BlockSpec((None, 1, d), lambda i: (layer, 0, 0)),
        ],
        out_specs=pl.BlockSpec((tm, d), lambda i: (i, 0)),
        compiler_params=_params(("parallel",)),
        name="pre_norm",
    )(x, gains)


def _proj_kernel(h_ref, w_ref, *refs, row_chunk, n_side):
    side_in, o_ref, side_out, wb_ref = refs[:n_side], refs[n_side], refs[n_side + 1:-1], refs[-1]

    @pl.when(pl.program_id(1) == 0)
    def _():
        def body(c, carry):
            rows = pl.ds(pl.multiple_of(c * row_chunk, row_chunk), row_chunk)
            wb_ref[rows, :] = w_ref[rows, :].astype(BF16)
            return carry
        lax.fori_loop(0, w_ref.shape[0] // row_chunk, body, 0)

    o_ref[...] = jnp.dot(h_ref[...], wb_ref[...], preferred_element_type=F32).astype(o_ref.dtype)
    for src, dst in zip(side_in, side_out):
        dst[...] = src[...].astype(dst.dtype)


def _in_proj(h, w_in, layer, col_off, n_cols, name, side=()):
    t, d = h.shape
    tm = _pick(t, TOKENS_IN_PROJ)
    tn = IN_PROJ_COLS
    assert col_off % tn == 0 and n_cols % tn == 0
    col0 = col_off // tn
    n_i = t // tm
    n_steps = (n_cols // tn) * n_i
    side_specs, side_out_specs, side_shapes = [], [], []
    for w in side:
        rows, cols = w.shape[1:]
        slab = next(s for s in range(BF16_SUBLANES, rows + 1, BF16_SUBLANES)
                    if rows % s == 0 and rows // s <= n_steps)
        last = rows // slab - 1
        side_specs.append(pl.BlockSpec(
            (None, slab, cols), lambda j, i, last=last: (layer, jnp.minimum(j * n_i + i, last), 0)))
        side_out_specs.append(pl.BlockSpec(
            (slab, cols), lambda j, i, last=last: (jnp.minimum(j * n_i + i, last), 0)))
        side_shapes.append(jax.ShapeDtypeStruct((rows, cols), BF16))
    outs = pl.pallas_call(
        functools.partial(_proj_kernel, row_chunk=min(d, WEIGHT_CAST_ROWS), n_side=len(side)),
        out_shape=[jax.ShapeDtypeStruct((t, n_cols), BF16)] + side_shapes,
        grid=(n_cols // tn, n_i),
        in_specs=[
            pl.BlockSpec((tm, d), lambda j, i: (i, 0)),
            pl.BlockSpec((None, d, tn), lambda j, i: (layer, 0, col0 + j)),
        ] + side_specs,
        out_specs=[pl.BlockSpec((tm, tn), lambda j, i: (i, j))] + side_out_specs,
        scratch_shapes=[pltpu.VMEM((d, tn), BF16)],
        compiler_params=_params(("arbitrary", "arbitrary")),
        name=name,
    )(h, w_in, *side)
    return outs[0] if not side else outs


def _rope_tables(seq):
    half = HEAD_DIM // 2
    quarter = half // 2
    inv_freq = 1.0 / (ROPE_THETA ** (np.arange(quarter, dtype=np.float64) / quarter))
    t = np.arange(seq)
    ang_r = (t // GRID_W)[:, None] * inv_freq[None, :]
    ang_c = (t % GRID_W)[:, None] * inv_freq[None, :]
    cos = np.concatenate([np.cos(ang_r), np.cos(ang_r), np.cos(ang_c), np.cos(ang_c)], axis=1)
    sin = np.concatenate([-np.sin(ang_r), np.sin(ang_r), -np.sin(ang_c), np.sin(ang_c)], axis=1)
    return jnp.asarray(cos, F32), jnp.asarray(sin, F32)


def _qk_prep_kernel(x_ref, g_ref, cos_ref, sin_ref, o_ref):
    heads = x_ref.shape[1] // HEAD_DIM
    quarter = HEAD_DIM // 4
    cos = cos_ref[...]
    sin = sin_ref[...]
    lane = lax.broadcasted_iota(jnp.int32, cos.shape, 1)
    first = (lane % (2 * quarter)) < quarter
    for h in range(heads):
        cols = slice(h * HEAD_DIM, (h + 1) * HEAD_DIM)
        y = _rms_rows(x_ref[:, cols].astype(F32), g_ref[0, :, cols])
        partner = jnp.where(first, pltpu.roll(y, HEAD_DIM - quarter, 1), pltpu.roll(y, quarter, 1))
        o_ref[:, cols] = (y * cos + partner * sin).astype(o_ref.dtype)


def _qk_prep(proj, gains, cos, sin, seq):
    t = proj.shape[0]
    n_heads = GQA_Q_HEADS + GQA_KV_HEADS
    width = 2 * HEAD_DIM
    assert OFF_GQA_Q % width == 0 and n_heads % 2 == 0
    tm = _pick(seq, TOKENS_QK_PREP)
    blocks_per_seq = seq // tm
    col0 = OFF_GQA_Q // width
    return pl.pallas_call(
        _qk_prep_kernel,
        out_shape=jax.ShapeDtypeStruct((t, n_heads * HEAD_DIM), BF16),
        grid=(t // tm, n_heads // 2),
        in_specs=[
            pl.BlockSpec((tm, width), lambda i, j: (i, col0 + j)),
            pl.BlockSpec((1, 1, width), lambda i, j: (j, 0, 0)),
            pl.BlockSpec((tm, HEAD_DIM), lambda i, j: (i % blocks_per_seq, 0)),
            pl.BlockSpec((tm, HEAD_DIM), lambda i, j: (i % blocks_per_seq, 0)),
        ],
        out_specs=pl.BlockSpec((tm, width), lambda i, j: (i, j)),
        compiler_params=_params(("parallel", "parallel")),
        name="gqa_qk_prep",
    )(proj, gains, cos, sin)


GQA_ONES_ROWS = 16
GQA_PLAIN_EXP_BOUND = 64.0


def _gqa_logit_bound(q_gain, k_gain):
    return 1.02 * HEAD_DIM * jnp.max(jnp.abs(q_gain)) * jnp.max(jnp.abs(k_gain))


def _gqa_kernel(plain_ref, q_ref, k_ref, v_ref, o_ref, qs_ref, vt_ref, acc_ref, s_ref, m_ref, *, tk):
    tq = q_ref.shape[0]
    seq = k_ref.shape[0]
    n_chunks = seq // tk
    assert n_chunks % 2 == 0

    def chunk(c):
        return pl.ds(pl.multiple_of(c * tk, tk), tk)

    @pl.when(pl.program_id(2) == 0)
    def _():
        def body(c, carry):
            vt_ref[c, :HEAD_DIM, :] = v_ref[chunk(c), :].astype(F32).T.astype(BF16)
            vt_ref[c, HEAD_DIM:, :] = jnp.ones((GQA_ONES_ROWS, tk), BF16)
            return carry
        lax.fori_loop(0, n_chunks, body, 0)

    for h in range(GQA_GROUP):
        qs_ref[h * tq:(h + 1) * tq, :] = q_ref[:, h * HEAD_DIM:(h + 1) * HEAD_DIM]
    acc_ref[...] = jnp.zeros_like(acc_ref)

    def scores(c):
        return lax.dot_general(k_ref[chunk(c), :], qs_ref[...], (((1,), (1,)), ((), ())),
                               preferred_element_type=F32)

    @pl.when(plain_ref[0] != 0)
    def _():
        per_trip = _pick(n_chunks, GQA_CHUNKS_PER_TRIP)

        def body(t, carry):
            pv = None
            for u in range(per_trip):
                c = per_trip * t + u
                p = jnp.exp2(scores(c)).astype(BF16)
                part = jnp.dot(vt_ref[c], p, preferred_element_type=F32)
                pv = part if pv is None else pv + part
            acc_ref[...] += pv
            return carry
        lax.fori_loop(0, n_chunks // per_trip, body, 0)

    @pl.when(plain_ref[0] == 0)
    def _():
        m_ref[...] = jnp.full_like(m_ref, -jnp.inf)

        def accumulate(c, slot):
            s = s_ref[slot]
            m_old = m_ref[...]
            m_new = jnp.maximum(m_old, jnp.max(s, axis=0, keepdims=True))
            m_ref[...] = m_new
            p = jnp.exp2(s - m_new).astype(BF16)
            pv = jnp.dot(vt_ref[c], p, preferred_element_type=F32)
            acc_ref[...] = acc_ref[...] * jnp.exp2(m_old - m_new) + pv

        s_ref[0] = scores(0)

        def body(t, carry):
            s_ref[1] = scores(2 * t + 1)
            accumulate(2 * t, 0)
            s_ref[0] = scores(2 * t + 2)
            accumulate(2 * t + 1, 1)
            return carry

        lax.fori_loop(0, n_chunks // 2 - 1, body, 0)
        s_ref[1] = scores(n_chunks - 1)
        accumulate(n_chunks - 2, 0)
        accumulate(n_chunks - 1, 1)

    out_t = acc_ref[:HEAD_DIM, :] / acc_ref[HEAD_DIM:HEAD_DIM + 1, :]
    for h in range(GQA_GROUP):
        o_ref[:, h * HEAD_DIM:(h + 1) * HEAD_DIM] = out_t[:, h * tq:(h + 1) * tq].T.astype(o_ref.dtype)


def _gqa_attention(qk, proj, plain_exp, batch, seq):
    t = qk.shape[0]
    tq = _pick(seq, GQA_QUERY_BLOCK)
    tk = _pick(seq, GQA_KEY_CHUNK)
    gw = GQA_GROUP * HEAD_DIM
    k_col0 = GQA_Q_WIDTH // HEAD_DIM
    v_col0 = OFF_GQA_V // HEAD_DIM
    q_blocks = seq // tq
    grid_spec = pltpu.PrefetchScalarGridSpec(
        num_scalar_prefetch=1,
        grid=(batch, GQA_KV_HEADS, q_blocks),
        in_specs=[
            pl.BlockSpec((tq, gw), lambda b, g, i, flag: (b * q_blocks + i, g)),
            pl.BlockSpec((seq, HEAD_DIM), lambda b, g, i, flag: (b, k_col0 + g)),
            pl.BlockSpec((seq, HEAD_DIM), lambda b, g, i, flag: (b, v_col0 + g)),
        ],
        out_specs=pl.BlockSpec((tq, gw), lambda b, g, i, flag: (b * q_blocks + i, g)),
        scratch_shapes=[
            pltpu.VMEM((GQA_GROUP * tq, HEAD_DIM), BF16),
            pltpu.VMEM((seq // tk, HEAD_DIM + GQA_ONES_ROWS, tk), BF16),
            pltpu.VMEM((HEAD_DIM + GQA_ONES_ROWS, GQA_GROUP * tq), F32),
            pltpu.VMEM((2, tk, GQA_GROUP * tq), F32),
            pltpu.VMEM((1, GQA_GROUP * tq), F32),
        ],
    )
    return pl.pallas_call(
        functools.partial(_gqa_kernel, tk=tk),
        out_shape=jax.ShapeDtypeStruct((t, GQA_Q_WIDTH), BF16),
        grid_spec=grid_spec,
        compiler_params=_params(("parallel", "parallel", "arbitrary")),
        name="gqa_flash",
    )(plain_exp, qk, qk, proj)


NA_PAIR_ROWS = 2 * NA_WIN_ROWS
NA_MASK_BOTH, NA_MASK_LEFT, NA_MASK_RIGHT = 0, 1, 2


def _na_block_plan(rows):
    gr, br = NA_GROUP_ROWS, NA_BAND_ROWS
    plan = []
    for r0 in (0, min(gr, rows - gr), rows - gr):
        band_start = int(np.clip(r0 - NA_WIN_ROWS // 2, 0, rows - br))
        per_row = []
        for i in range(gr):
            r = r0 + i
            row_start = int(np.clip(r - NA_WIN_ROWS // 2, 0, rows - NA_WIN_ROWS))
            blocks = []
            for jp in range(br // 2):
                key_row = band_start + 2 * jp
                ok_l = row_start <= key_row < row_start + NA_WIN_ROWS
                ok_r = row_start <= key_row + 1 < row_start + NA_WIN_ROWS
                if not (ok_l or ok_r):
                    blocks.append(None)
                    continue
                kind = NA_MASK_BOTH if (ok_l and ok_r) else (NA_MASK_LEFT if ok_l else NA_MASK_RIGHT)
                blocks.append((key_row - r + NA_WIN_ROWS, kind))
            per_row.append(blocks)
        plan.append(per_row)
    return plan


def _na_pair_table(rpb):
    h, nr, nc = rpb.shape
    rpb = jnp.pad(rpb.astype(F32), ((0, 0), (0, 0), (0, GRID_W - nc)))
    zero = jnp.zeros((h, 1, GRID_W), F32)
    left = jnp.concatenate([zero, rpb], axis=1)
    right = jnp.concatenate([rpb, zero], axis=1)
    return jnp.concatenate([left, right], axis=2)


def _na_col_masks():
    w = np.arange(GRID_W)[:, None]
    c = np.arange(2 * GRID_W)[None, :] % GRID_W
    col_start = np.clip(w - NA_WIN_COLS // 2, 0, GRID_W - NA_WIN_COLS)
    base = np.where((c >= col_start) & (c < col_start + NA_WIN_COLS), 0.0, MASK_VALUE)
    left_half = np.arange(2 * GRID_W)[None, :] < GRID_W
    masks = np.stack([base, np.where(left_half, base, MASK_VALUE), np.where(left_half, MASK_VALUE, base)])
    return jnp.asarray(masks, F32)


def _na_kernel(q_ref, k_ref, v_ref, pair_ref, mask_ref, o_ref, tile_ref, table_ref, vaug_ref, *, rows):
    g = pl.program_id(2)
    n_groups = rows // NA_GROUP_ROWS
    blk_w = 2 * GRID_W

    @pl.when(g == 0)
    def _():
        for p in range(NA_PAIR_ROWS):
            row = jnp.broadcast_to(pair_ref[p:p + 1, :], (GRID_W, blk_w))
            tile_ref[p] = pltpu.roll(row, blk_w - (NA_WIN_COLS - 1), 1, stride=1, stride_axis=0)
        for v, per_row in enumerate(_na_block_plan(rows)):
            for i, blocks in enumerate(per_row):
                for jp, blk in enumerate(blocks):
                    dst = (v, slice(i * GRID_W, (i + 1) * GRID_W), slice(jp * blk_w, (jp + 1) * blk_w))
                    if blk is None:
                        table_ref[dst] = jnp.full((GRID_W, blk_w), MASK_VALUE, F32)
                    else:
                        table_ref[dst] = (tile_ref[blk[0]] + mask_ref[blk[1]]) * LOG2E
        vaug_ref[:, :HEAD_DIM] = v_ref[...]
        vaug_ref[:, HEAD_DIM:] = jnp.ones(v_ref.shape, BF16)

    gq = NA_GROUP_ROWS * GRID_W
    band_rows = NA_BAND_ROWS * GRID_W
    for u in range(q_ref.shape[0] // gq):
        gi = g * (q_ref.shape[0] // gq) + u
        start_row = jnp.clip(gi * NA_GROUP_ROWS - NA_WIN_ROWS // 2, 0, rows - NA_BAND_ROWS)
        band = pl.ds(pl.multiple_of(start_row * GRID_W, GRID_W), band_rows)
        variant = jnp.where(gi == 0, 0, jnp.where(gi == n_groups - 1, 2, 1))
        s = lax.dot_general(q_ref[u * gq:(u + 1) * gq, :], k_ref[band, :], (((1,), (1,)), ((), ())),
                            preferred_element_type=F32)
        s = s * (HEAD_DIM ** -0.5 * LOG2E) + table_ref[variant]
        p = jnp.exp2(s - jnp.max(s, axis=-1, keepdims=True)).astype(BF16)
        o = jnp.dot(p, vaug_ref[band, :], preferred_element_type=F32)
        o_ref[u * gq:(u + 1) * gq, :] = (o[:, :HEAD_DIM] / o[:, HEAD_DIM:]).astype(o_ref.dtype)


def _na_attention(proj, rpb, batch, seq):
    t = proj.shape[0]
    rows = seq // GRID_W
    assert NA_GROUP_ROWS % 4 == 0 and rows % NA_GROUP_ROWS == 0 and rows >= NA_BAND_ROWS
    assert 2 * GRID_W == HEAD_DIM
    n_groups = rows // NA_GROUP_ROWS
    gq = NA_GROUP_ROWS * GRID_W
    gk = NA_BAND_ROWS * GRID_W
    per_step = _pick(n_groups, NA_GROUPS_PER_STEP)
    n_steps = n_groups // per_step
    k_col0 = OFF_NA_K // HEAD_DIM
    v_col0 = OFF_NA_V // HEAD_DIM
    return pl.pallas_call(
        functools.partial(_na_kernel, rows=rows),
        out_shape=jax.ShapeDtypeStruct((t, NA_WIDTH), BF16),
        grid=(batch, NA_HEADS, n_steps),
        in_specs=[
            pl.BlockSpec((per_step * gq, HEAD_DIM), lambda b, h, g: (b * n_steps + g, h)),
            pl.BlockSpec((seq, HEAD_DIM), lambda b, h, g: (b, k_col0 + h)),
            pl.BlockSpec((seq, HEAD_DIM), lambda b, h, g: (b, v_col0 + h)),
            pl.BlockSpec((None, NA_PAIR_ROWS, 2 * GRID_W), lambda b, h, g: (h, 0, 0)),
            pl.BlockSpec((3, GRID_W, 2 * GRID_W), lambda b, h, g: (0, 0, 0)),
        ],
        out_specs=pl.BlockSpec((per_step * gq, HEAD_DIM), lambda b, h, g: (b * n_steps + g, h)),
        scratch_shapes=[
            pltpu.VMEM((NA_PAIR_ROWS, GRID_W, 2 * GRID_W), F32),
            pltpu.VMEM((3, gq, gk), F32),
            pltpu.VMEM((seq, 2 * HEAD_DIM), BF16),
        ],
        compiler_params=_params(("parallel", "parallel", "arbitrary")),
        name="na_attention",
    )(proj, proj, proj, _na_pair_table(rpb), _na_col_masks())


CONV_HALO = 16


def _branch_merge_kernel(a_na_ref, a_gqa_ref, h_ref, b_ref, c_ref, hp_ref, cp_ref, hn_ref, cn_ref,
                         g_na_ref, g_gqa_ref, g_conv_ref, cw_ref, cb_ref, w_na_ref, w_gqa_ref,
                         w_conv_ref, o_ref, a_conv_ref, part_ref, *, blocks_per_seq, col_chunk):
    tm = h_ref.shape[0]
    pos = pl.program_id(0) % blocks_per_seq
    n_chunks = o_ref.shape[1] // col_chunk

    def branch(a_ref, w_ref, gate_ref, cols):
        y = jnp.dot(a_ref[...], w_ref[:, cols], preferred_element_type=F32)
        return _sigmoid(gate_ref[:, cols].astype(F32)) * y

    for c in range(n_chunks):
        cols = slice(c * col_chunk, (c + 1) * col_chunk)
        part_ref[:, cols] = (branch(a_na_ref, w_na_ref, g_na_ref, cols)
                             + branch(a_gqa_ref, w_gqa_ref, g_gqa_ref, cols))

    u = c_ref[...].astype(F32) * h_ref[...].astype(F32)
    u_prev_row = cp_ref[CONV_HALO - 1:CONV_HALO, :].astype(F32) * hp_ref[CONV_HALO - 1:CONV_HALO, :].astype(F32)
    u_prev_row = jnp.where(pos == 0, 0.0, u_prev_row)
    u_next_row = cn_ref[0:1, :].astype(F32) * hn_ref[0:1, :].astype(F32)
    u_next_row = jnp.where(pos == blocks_per_seq - 1, 0.0, u_next_row)
    row = lax.broadcasted_iota(jnp.int32, u.shape, 0)
    u_prev = jnp.where(row == 0, u_prev_row, pltpu.roll(u, 1, 0))
    u_next = jnp.where(row == tm - 1, u_next_row, pltpu.roll(u, tm - 1, 0))
    y = cb_ref[...] + u_prev * cw_ref[0:1, :] + u * cw_ref[1:2, :] + u_next * cw_ref[2:3, :]
    a_conv_ref[...] = (b_ref[...].astype(F32) * y).astype(BF16)

    for c in range(n_chunks):
        cols = slice(c * col_chunk, (c + 1) * col_chunk)
        merged = part_ref[:, cols] + branch(a_conv_ref, w_conv_ref, g_conv_ref, cols)
        o_ref[:, cols] = merged.astype(o_ref.dtype)


def _branch_merge(a_na, a_gqa, p_conv, p_gate, conv_w, conv_b, w_na, w_gqa, w_conv, layer, seq):
    t = a_na.shape[0]
    ch = conv_w.shape[2]
    d = w_na.shape[1]
    assert CONV_WIDTH == 3
    tm = _pick(seq, TOKENS_F32_BLOCK)
    per_halo = tm // CONV_HALO
    last_halo = t // CONV_HALO - 1
    prev_map = lambda col: (lambda i: (jnp.maximum(i * per_halo - 1, 0), col))
    next_map = lambda col: (lambda i: (jnp.minimum((i + 1) * per_halo, last_halo), col))
    resident = lambda shape: pl.BlockSpec(shape, lambda i: (0, 0), pipeline_mode=pl.Buffered(1))
    return pl.pallas_call(
        functools.partial(_branch_merge_kernel, blocks_per_seq=seq // tm,
                          col_chunk=_pick(d, MERGE_COL_CHUNK)),
        out_shape=jax.ShapeDtypeStruct((t, d), BF16),
        grid=(t // tm,),
        in_specs=[
            pl.BlockSpec((tm, NA_WIDTH), lambda i: (i, 0)),
            pl.BlockSpec((tm, GQA_Q_WIDTH), lambda i: (i, 0)),
            pl.BlockSpec((tm, ch), lambda i: (i, 0)),
            pl.BlockSpec((tm, ch), lambda i: (i, 1)),
            pl.BlockSpec((tm, ch), lambda i: (i, 2)),
            pl.BlockSpec((CONV_HALO, ch), prev_map(0)),
            pl.BlockSpec((CONV_HALO, ch), prev_map(2)),
            pl.BlockSpec((CONV_HALO, ch), next_map(0)),
            pl.BlockSpec((CONV_HALO, ch), next_map(2)),
            pl.BlockSpec((tm, d), lambda i: (i, 0)),
            pl.BlockSpec((tm, d), lambda i: (i, 1)),
            pl.BlockSpec((tm, d), lambda i: (i, 2)),
            pl.BlockSpec((None, CONV_WIDTH, ch), lambda i: (layer, 0, 0)),
            pl.BlockSpec((None, 1, ch), lambda i: (layer, 0, 0)),
            resident((NA_WIDTH, d)),
            resident((GQA_Q_WIDTH, d)),
            resident((ch, d)),
        ],
        out_specs=pl.BlockSpec((tm, d), lambda i: (i, 0)),
        scratch_shapes=[pltpu.VMEM((tm, ch), BF16), pltpu.VMEM((tm, d), F32)],
        compiler_params=_params(("parallel",)),
        name="branch_merge",
    )(a_na, a_gqa, p_conv, p_conv, p_conv, p_conv, p_conv, p_conv, p_conv, p_gate, p_gate, p_gate,
      conv_w, conv_b, w_na, w_gqa, w_conv)


def _out_proj_kernel(x_ref, m_ref, w_ref, g_post_ref, g_next_ref, o_ref, h_ref):
    part = x_ref.shape[0] // OUT_PROJ_ROW_PARTS
    for r in range(OUT_PROJ_ROW_PARTS):
        rows = slice(r * part, (r + 1) * part)
        y = jnp.dot(m_ref[rows, :], w_ref[...], preferred_element_type=F32)
        x_new = x_ref[rows, :] + _rms_rows(y, g_post_ref[...])
        o_ref[rows, :] = x_new
        h_ref[rows, :] = _rms_rows(x_new, g_next_ref[...]).astype(h_ref.dtype)


def _out_proj(x, merged, w_out, post_gains, next_gains, layer):
    t, d = x.shape
    tm = _pick(t, TOKENS_F32_BLOCK)
    gain = lambda: pl.BlockSpec((None, 1, d), lambda i: (layer, 0, 0))
    return pl.pallas_call(
        _out_proj_kernel,
        out_shape=(jax.ShapeDtypeStruct((t, d), F32), jax.ShapeDtypeStruct((t, d), BF16)),
        grid=(t // tm,),
        in_specs=[
            pl.BlockSpec((tm, d), lambda i: (i, 0)),
            pl.BlockSpec((tm, d), lambda i: (i, 0)),
            pl.BlockSpec((d, d), lambda i: (0, 0), pipeline_mode=pl.Buffered(1)),
            gain(),
            gain(),
        ],
        out_specs=(pl.BlockSpec((tm, d), lambda i: (i, 0)), pl.BlockSpec((tm, d), lambda i: (i, 0))),
        compiler_params=_params(("parallel",)),
        name="out_proj_norm",
    )(x, merged, w_out, post_gains, next_gains)


def _ffn_kernel(x_ref, h_ref, wg_ref, wu_ref, wd_ref, g_post_ref, *refs, emit_next):
    (g_next_ref, o_ref, hn_ref, acc_ref) = refs if emit_next else (None, refs[0], None, refs[1])
    f = pl.program_id(1)

    @pl.when(f == 0)
    def _():
        acc_ref[...] = jnp.zeros_like(acc_ref)

    h = h_ref[...]
    a = jnp.dot(h, wg_ref[...], preferred_element_type=F32)
    b = jnp.dot(h, wu_ref[...], preferred_element_type=F32)
    act = (a * _sigmoid(a) * b).astype(BF16)
    acc_ref[...] += jnp.dot(act, wd_ref[...], preferred_element_type=F32)

    @pl.when(f == pl.num_programs(1) - 1)
    def _():
        x_new = x_ref[...] + _rms_rows(acc_ref[...], g_post_ref[...])
        o_ref[...] = x_new
        if emit_next:
            hn_ref[...] = _rms_rows(x_new, g_next_ref[...]).astype(hn_ref.dtype)


def _ffn(x, h, wg, wu, wd, post_gains, next_gains, layer, next_layer):
    t, d = x.shape
    hidden = wg.shape[1]
    tm = _pick(t, TOKENS_F32_BLOCK)
    tf = _pick(hidden, FFN_HIDDEN_CHUNK)
    emit_next = next_layer is not None
    token_block = lambda: pl.BlockSpec((tm, d), lambda i, f: (i, 0))
    gain = lambda l: pl.BlockSpec((None, 1, d), lambda i, f: (l, 0, 0))
    outs = pl.pallas_call(
        functools.partial(_ffn_kernel, emit_next=emit_next),
        out_shape=[jax.ShapeDtypeStruct((t, d), F32)] + [jax.ShapeDtypeStruct((t, d), BF16)] * emit_next,
        grid=(t // tm, hidden // tf),
        in_specs=[
            token_block(),
            token_block(),
            pl.BlockSpec((d, tf), lambda i, f: (0, f)),
            pl.BlockSpec((d, tf), lambda i, f: (0, f)),
            pl.BlockSpec((tf, d), lambda i, f: (f, 0)),
            gain(layer),
        ] + [gain(next_layer)] * emit_next,
        out_specs=[token_block()] + [token_block()] * emit_next,
        scratch_shapes=[pltpu.VMEM((tm, d), F32)],
        compiler_params=_params(("parallel", "arbitrary")),
        name="ffn_swiglu",
    )(x, h, wg, wu, wd, post_gains, *([next_gains] * emit_next))
    return (outs[0], outs[1]) if emit_next else (outs[0], None)


def kernel(x, w_in, na_rpb, q_norm_g, k_norm_g, conv_w, conv_b, w_br_na, w_br_gqa, w_br_conv, w_out,
           pre_mix_g, post_mix_g, pre_ffn_g, post_ffn_g, w_ffn_gate, w_ffn_up, w_ffn_down):
    batch, seq, d = x.shape
    depth = w_in.shape[0]
    ch = conv_w.shape[2]
    hidden = w_ffn_gate.shape[2]
    assert seq % GRID_W == 0
    assert w_in.shape == (depth, d, OFF_CONV + 3 * ch + 3 * d)
    assert na_rpb.shape == (depth, NA_HEADS, 2 * NA_WIN_ROWS - 1, 2 * NA_WIN_COLS - 1)
    assert q_norm_g.shape == k_norm_g.shape == (depth, HEAD_DIM)
    assert conv_w.shape == (depth, CONV_WIDTH, ch) and conv_b.shape == (depth, ch)
    assert w_br_na.shape == (depth, NA_WIDTH, d) and w_br_gqa.shape == (depth, GQA_Q_WIDTH, d)
    assert w_br_conv.shape == (depth, ch, d) and w_out.shape == (depth, d, d)
    assert w_ffn_gate.shape == w_ffn_up.shape == (depth, d, hidden) and w_ffn_down.shape == (depth, hidden, d)
    cos, sin = _rope_tables(seq)

    gain3 = lambda g: g.astype(F32).reshape(depth, 1, -1)
    pre_mix_g, post_mix_g, pre_ffn_g, post_ffn_g = map(gain3, (pre_mix_g, post_mix_g, pre_ffn_g, post_ffn_g))
    conv_w = conv_w.astype(F32)
    conv_b = conv_b.astype(F32).reshape(depth, 1, ch)
    q_gain = q_norm_g.astype(F32) * (HEAD_DIM ** -0.5 * LOG2E)
    qk_gains = jnp.concatenate([jnp.tile(q_gain, (1, GQA_Q_HEADS)),
                                jnp.tile(k_norm_g.astype(F32), (1, GQA_KV_HEADS))], axis=1)
    qk_gains = qk_gains.reshape(depth, -1, 1, 2 * HEAD_DIM)

    y = x.reshape(batch * seq, d)
    h = _pre_norm(y, pre_mix_g, 0)
    for l in range(depth):
        p_attn, w_na, w_gqa, w_conv, w_o = _in_proj(h, w_in, l, 0, OFF_CONV, "in_proj_attn",
                                                    side=(w_br_na, w_br_gqa, w_br_conv, w_out))
        p_conv, wd = _in_proj(h, w_in, l, OFF_CONV, 3 * ch, "in_proj_conv", side=(w_ffn_down,))
        p_gate, wg, wu = _in_proj(h, w_in, l, OFF_CONV + 3 * ch, 3 * d, "in_proj_gate",
                                  side=(w_ffn_gate, w_ffn_up))

        qk = _qk_prep(p_attn, qk_gains[l], cos, sin, seq)
        plain_exp = (_gqa_logit_bound(q_gain[l], k_norm_g[l].astype(F32)) <= GQA_PLAIN_EXP_BOUND)
        a_gqa = _gqa_attention(qk, p_attn, plain_exp.astype(jnp.int32).reshape(1), batch, seq)
        a_na = _na_attention(p_attn, na_rpb[l], batch, seq)

        merged = _branch_merge(a_na, a_gqa, p_conv, p_gate, conv_w, conv_b, w_na, w_gqa, w_conv, l, seq)
        y, h = _out_proj(y, merged, w_o, post_mix_g, pre_ffn_g, l)
        y, h = _ffn(y, h, wg, wu, wd, post_ffn_g, pre_mix_g, l, l + 1 if l + 1 < depth else None)
    return y.reshape(batch, seq, d)
```

```python
import functools
import math

import numpy as np
import jax
import jax.numpy as jnp
from jax import lax
from jax.experimental import pallas as pl
from jax.experimental.pallas import tpu as pltpu

GRID_W = 64
HEAD_DIM = 128
NA_HEADS = 8
NA_WIN_ROWS = 8
NA_WIN_COLS = 16
GQA_Q_HEADS = 8
GQA_KV_HEADS = 2
GQA_GROUP = GQA_Q_HEADS // GQA_KV_HEADS
ROPE_THETA = 10000.0
CONV_WIDTH = 3
RMS_EPS = 1e-6

NA_WIDTH = NA_HEADS * HEAD_DIM
GQA_Q_WIDTH = GQA_Q_HEADS * HEAD_DIM
GQA_KV_WIDTH = GQA_KV_HEADS * HEAD_DIM
OFF_NA_Q = 0
OFF_NA_K = NA_WIDTH
OFF_NA_V = 2 * NA_WIDTH
OFF_GQA_Q = 3 * NA_WIDTH
OFF_GQA_K = OFF_GQA_Q + GQA_Q_WIDTH
OFF_GQA_V = OFF_GQA_K + GQA_KV_WIDTH
OFF_CONV = OFF_GQA_V + GQA_KV_WIDTH

MASK_VALUE = -1e30
LOG2E = math.log2(math.e)

V7X_VMEM_BYTES = 64 * 1024 * 1024
VMEM_LIMIT = V7X_VMEM_BYTES * 7 // 8

TOKENS_F32_BLOCK = (512, 256)
TOKENS_IN_PROJ = (1024, 512, 256)
TOKENS_QK_PREP = (2048, 1024, 512, 256)
IN_PROJ_COLS = 1536
WEIGHT_CAST_ROWS = 256
GQA_QUERY_BLOCK = (512, 256, 128)
GQA_KEY_CHUNK = (512, 256, 128)
GQA_CHUNKS_PER_TRIP = (8, 4, 2)
NA_GROUP_ROWS = 4
NA_BAND_ROWS = NA_GROUP_ROWS + NA_WIN_ROWS
NA_GROUPS_PER_STEP = (16, 8, 2, 1)
MERGE_COL_CHUNK = (512, 256, 128)
OUT_PROJ_ROW_PARTS = 4
FFN_HIDDEN_CHUNK = (512, 256, 128)

BF16 = jnp.bfloat16
F32 = jnp.float32
BF16_SUBLANES = 16


def _params(semantics):
    return pltpu.CompilerParams(dimension_semantics=semantics, vmem_limit_bytes=VMEM_LIMIT)


def _pick(n, candidates):
    for c in candidates:
        if n % c == 0:
            return c
    raise ValueError(f"no tile in {candidates} divides {n}")


def _sigmoid(x):
    return 0.5 * jnp.tanh(0.5 * x) + 0.5


def _rms_rows(x, g):
    ms = jnp.mean(x * x, axis=-1, keepdims=True)
    return x * lax.rsqrt(ms + RMS_EPS) * g


def _norm_kernel(x_ref, g_ref, o_ref):
    o_ref[...] = _rms_rows(x_ref[...], g_ref[...]).astype(o_ref.dtype)


def _pre_norm(x, gains, layer):
    t, d = x.shape
    tm = _pick(t, TOKENS_F32_BLOCK)
    return pl.pallas_call(
        _norm_kernel,
        out_shape=jax.ShapeDtypeStruct((t, d), BF16),
        grid=(t // tm,),
        in_specs=[
            pl.BlockSpec((tm, d), lambda i: (i, 0)),
            pl.BlockSpec((None, 1, d), lambda i: (layer, 0, 0)),
        ],
        out_specs=pl.BlockSpec((tm, d), lambda i: (i, 0)),
        compiler_params=_params(("parallel",)),
        name="pre_norm",
    )(x, gains)


def _proj_kernel(h_ref, w_ref, *refs, row_chunk, n_side):
    side_in, o_ref, side_out, wb_ref = refs[:n_side], refs[n_side], refs[n_side + 1:-1], refs[-1]

    @pl.when(pl.program_id(1) == 0)
    def _():
        def body(c, carry):
            rows = pl.ds(pl.multiple_of(c * row_chunk, row_chunk), row_chunk)
            wb_ref[rows, :] = w_ref[rows, :].astype(BF16)
            return carry
        lax.fori_loop(0, w_ref.shape[0] // row_chunk, body, 0)

    o_ref[...] = jnp.dot(h_ref[...], wb_ref[...], preferred_element_type=F32).astype(o_ref.dtype)
    for src, dst in zip(side_in, side_out):
        dst[...] = src[...].astype(dst.dtype)


def _in_proj(h, w_in, layer, col_off, n_cols, name, side=()):
    t, d = h.shape
    tm = _pick(t, TOKENS_IN_PROJ)
    tn = IN_PROJ_COLS
    assert col_off % tn == 0 and n_cols % tn == 0
    col0 = col_off // tn
    n_i = t // tm
    n_steps = (n_cols // tn) * n_i
    side_specs, side_out_specs, side_shapes = [], [], []
    for w in side:
        rows, cols = w.shape[1:]
        slab = next(s for s in range(BF16_SUBLANES, rows + 1, BF16_SUBLANES)
                    if rows % s == 0 and rows // s <= n_steps)
        last = rows // slab - 1
        side_specs.append(pl.BlockSpec(
            (None, slab, cols), lambda j, i, last=last: (layer, jnp.minimum(j * n_i + i, last), 0)))
        side_out_specs.append(pl.BlockSpec(
            (slab, cols), lambda j, i, last=last: (jnp.minimum(j * n_i + i, last), 0)))
        side_shapes.append(jax.ShapeDtypeStruct((rows, cols), BF16))
    outs = pl.pallas_call(
        functools.partial(_proj_kernel, row_chunk=min(d, WEIGHT_CAST_ROWS), n_side=len(side)),
        out_shape=[jax.ShapeDtypeStruct((t, n_cols), BF16)] + side_shapes,
        grid=(n_cols // tn, n_i),
        in_specs=[
            pl.BlockSpec((tm, d), lambda j, i: (i, 0)),
            pl.BlockSpec((None, d, tn), lambda j, i: (layer, 0, col0 + j)),
        ] + side_specs,
        out_specs=[pl.BlockSpec((tm, tn), lambda j, i: (i, j))] + side_out_specs,
        scratch_shapes=[pltpu.VMEM((d, tn), BF16)],
        compiler_params=_params(("arbitrary", "arbitrary")),
        name=name,
    )(h, w_in, *side)
    return outs[0] if not side else outs


def _rope_tables(seq):
    half = HEAD_DIM // 2
    quarter = half // 2
    inv_freq = 1.0 / (ROPE_THETA ** (np.arange(quarter, dtype=np.float64) / quarter))
    t = np.arange(seq)
    ang_r = (t // GRID_W)[:, None] * inv_freq[None, :]
    ang_c = (t % GRID_W)[:, None] * inv_freq[None, :]
    cos = np.concatenate([np.cos(ang_r), np.cos(ang_r), np.cos(ang_c), np.cos(ang_c)], axis=1)
    sin = np.concatenate([-np.sin(ang_r), np.sin(ang_r), -np.sin(ang_c), np.sin(ang_c)], axis=1)
    return jnp.asarray(cos, F32), jnp.asarray(sin, F32)


def _qk_prep_kernel(x_ref, g_ref, cos_ref, sin_ref, o_ref):
    heads = x_ref.shape[1] // HEAD_DIM
    quarter = HEAD_DIM // 4
    cos = cos_ref[...]
    sin = sin_ref[...]
    lane = lax.broadcasted_iota(jnp.int32, cos.shape, 1)
    first = (lane % (2 * quarter)) < quarter
    for h in range(heads):
        cols = slice(h * HEAD_DIM, (h + 1) * HEAD_DIM)
        y = _rms_rows(x_ref[:, cols].astype(F32), g_ref[0, :, cols])
        partner = jnp.where(first, pltpu.roll(y, HEAD_DIM - quarter, 1), pltpu.roll(y, quarter, 1))
        o_ref[:, cols] = (y * cos + partner * sin).astype(o_ref.dtype)


def _qk_prep(proj, gains, cos, sin, seq):
    t = proj.shape[0]
    n_heads = GQA_Q_HEADS + GQA_KV_HEADS
    width = 2 * HEAD_DIM
    assert OFF_GQA_Q % width == 0 and n_heads % 2 == 0
    tm = _pick(seq, TOKENS_QK_PREP)
    blocks_per_seq = seq // tm
    col0 = OFF_GQA_Q // width
    return pl.pallas_call(
        _qk_prep_kernel,
        out_shape=jax.ShapeDtypeStruct((t, n_heads * HEAD_DIM), BF16),
        grid=(t // tm, n_heads // 2),
        in_specs=[
            pl.BlockSpec((tm, width), lambda i, j: (i, col0 + j)),
            pl.BlockSpec((1, 1, width), lambda i, j: (j, 0, 0)),
            pl.BlockSpec((tm, HEAD_DIM), lambda i, j: (i % blocks_per_seq, 0)),
            pl.BlockSpec((tm, HEAD_DIM), lambda i, j: (i % blocks_per_seq, 0)),
        ],
        out_specs=pl.BlockSpec((tm, width), lambda i, j: (i, j)),
        compiler_params=_params(("parallel", "parallel")),
        name="gqa_qk_prep",
    )(proj, gains, cos, sin)


GQA_ONES_ROWS = 16
GQA_PLAIN_EXP_BOUND = 64.0


def _gqa_logit_bound(q_gain, k_gain):
    return 1.02 * HEAD_DIM * jnp.max(jnp.abs(q_gain)) * jnp.max(jnp.abs(k_gain))


def _gqa_kernel(plain_ref, q_ref, k_ref, v_ref, o_ref, qs_ref, vt_ref, acc_ref, s_ref, m_ref, *, tk):
    tq = q_ref.shape[0]
    seq = k_ref.shape[0]
    n_chunks = seq // tk
    assert n_chunks % 2 == 0

    def chunk(c):
        return pl.ds(pl.multiple_of(c * tk, tk), tk)

    @pl.when(pl.program_id(2) == 0)
    def _():
        def body(c, carry):
            vt_ref[c, :HEAD_DIM, :] = v_ref[chunk(c), :].astype(F32).T.astype(BF16)
            vt_ref[c, HEAD_DIM:, :] = jnp.ones((GQA_ONES_ROWS, tk), BF16)
            return carry
        lax.fori_loop(0, n_chunks, body, 0)

    for h in range(GQA_GROUP):
        qs_ref[h * tq:(h + 1) * tq, :] = q_ref[:, h * HEAD_DIM:(h + 1) * HEAD_DIM]
    acc_ref[...] = jnp.zeros_like(acc_ref)

    def scores(c):
        return lax.dot_general(k_ref[chunk(c), :], qs_ref[...], (((1,), (1,)), ((), ())),
                               preferred_element_type=F32)

    @pl.when(plain_ref[0] != 0)
    def _():
        per_trip = _pick(n_chunks, GQA_CHUNKS_PER_TRIP)

        def body(t, carry):
            pv = None
            for u in range(per_trip):
                c = per_trip * t + u
                p = jnp.exp2(scores(c)).astype(BF16)
                part = jnp.dot(vt_ref[c], p, preferred_element_type=F32)
                pv = part if pv is None else pv + part
            acc_ref[...] += pv
            return carry
        lax.fori_loop(0, n_chunks // per_trip, body, 0)

    @pl.when(plain_ref[0] == 0)
    def _():
        m_ref[...] = jnp.full_like(m_ref, -jnp.inf)

        def accumulate(c, slot):
            s = s_ref[slot]
            m_old = m_ref[...]
            m_new = jnp.maximum(m_old, jnp.max(s, axis=0, keepdims=True))
            m_ref[...] = m_new
            p = jnp.exp2(s - m_new).astype(BF16)
            pv = jnp.dot(vt_ref[c], p, preferred_element_type=F32)
            acc_ref[...] = acc_ref[...] * jnp.exp2(m_old - m_new) + pv

        s_ref[0] = scores(0)

        def body(t, carry):
            s_ref[1] = scores(2 * t + 1)
            accumulate(2 * t, 0)
            s_ref[0] = scores(2 * t + 2)
            accumulate(2 * t + 1, 1)
            return carry

        lax.fori_loop(0, n_chunks // 2 - 1, body, 0)
        s_ref[1] = scores(n_chunks - 1)
        accumulate(n_chunks - 2, 0)
        accumulate(n_chunks - 1, 1)

    out_t = acc_ref[:HEAD_DIM, :] / acc_ref[HEAD_DIM:HEAD_DIM + 1, :]
    for h in range(GQA_GROUP):
        o_ref[:, h * HEAD_DIM:(h + 1) * HEAD_DIM] = out_t[:, h * tq:(h + 1) * tq].T.astype(o_ref.dtype)


def _gqa_attention(qk, proj, plain_exp, batch, seq):
    t = qk.shape[0]
    tq = _pick(seq, GQA_QUERY_BLOCK)
    tk = _pick(seq, GQA_KEY_CHUNK)
    gw = GQA_GROUP * HEAD_DIM
    k_col0 = GQA_Q_WIDTH // HEAD_DIM
    v_col0 = OFF_GQA_V // HEAD_DIM
    q_blocks = seq // tq
    grid_spec = pltpu.PrefetchScalarGridSpec(
        num_scalar_prefetch=1,
        grid=(batch, GQA_KV_HEADS, q_blocks),
        in_specs=[
            pl.BlockSpec((tq, gw), lambda b, g, i, flag: (b * q_blocks + i, g)),
            pl.BlockSpec((seq, HEAD_DIM), lambda b, g, i, flag: (b, k_col0 + g)),
            pl.BlockSpec((seq, HEAD_DIM), lambda b, g, i, flag: (b, v_col0 + g)),
        ],
        out_specs=pl.BlockSpec((tq, gw), lambda b, g, i, flag: (b * q_blocks + i, g)),
        scratch_shapes=[
            pltpu.VMEM((GQA_GROUP * tq, HEAD_DIM), BF16),
            pltpu.VMEM((seq // tk, HEAD_DIM + GQA_ONES_ROWS, tk), BF16),
            pltpu.VMEM((HEAD_DIM + GQA_ONES_ROWS, GQA_GROUP * tq), F32),
            pltpu.VMEM((2, tk, GQA_GROUP * tq), F32),
            pltpu.VMEM((1, GQA_GROUP * tq), F32),
        ],
    )
    return pl.pallas_call(
        functools.partial(_gqa_kernel, tk=tk),
        out_shape=jax.ShapeDtypeStruct((t, GQA_Q_WIDTH), BF16),
        grid_spec=grid_spec,
        compiler_params=_params(("parallel", "parallel", "arbitrary")),
        name="gqa_flash",
    )(plain_exp, qk, qk, proj)


NA_PAIR_ROWS = 2 * NA_WIN_ROWS
NA_MASK_BOTH, NA_MASK_LEFT, NA_MASK_RIGHT = 0, 1, 2


def _na_block_plan(rows):
    gr, br = NA_GROUP_ROWS, NA_BAND_ROWS
    plan = []
    for r0 in (0, min(gr, rows - gr), rows - gr):
        band_start = int(np.clip(r0 - NA_WIN_ROWS // 2, 0, rows - br))
        per_row = []
        for i in range(gr):
            r = r0 + i
            row_start = int(np.clip(r - NA_WIN_ROWS // 2, 0, rows - NA_WIN_ROWS))
            blocks = []
            for jp in range(br // 2):
                key_row = band_start + 2 * jp
                ok_l = row_start <= key_row < row_start + NA_WIN_ROWS
                ok_r = row_start <= key_row + 1 < row_start + NA_WIN_ROWS
                if not (ok_l or ok_r):
                    blocks.append(None)
                    continue
                kind = NA_MASK_BOTH if (ok_l and ok_r) else (NA_MASK_LEFT if ok_l else NA_MASK_RIGHT)
                blocks.append((key_row - r + NA_WIN_ROWS, kind))
            per_row.append(blocks)
        plan.append(per_row)
    return plan


def _na_pair_table(rpb):
    h, nr, nc = rpb.shape
    rpb = jnp.pad(rpb.astype(F32), ((0, 0), (0, 0), (0, GRID_W - nc)))
    zero = jnp.zeros((h, 1, GRID_W), F32)
    left = jnp.concatenate([zero, rpb], axis=1)
    right = jnp.concatenate([rpb, zero], axis=1)
    return jnp.concatenate([left, right], axis=2)


def _na_col_masks():
    w = np.arange(GRID_W)[:, None]
    c = np.arange(2 * GRID_W)[None, :] % GRID_W
    col_start = np.clip(w - NA_WIN_COLS // 2, 0, GRID_W - NA_WIN_COLS)
    base = np.where((c >= col_start) & (c < col_start + NA_WIN_COLS), 0.0, MASK_VALUE)
    left_half = np.arange(2 * GRID_W)[None, :] < GRID_W
    masks = np.stack([base, np.where(left_half, base, MASK_VALUE), np.where(left_half, MASK_VALUE, base)])
    return jnp.asarray(masks, F32)


def _na_kernel(q_ref, k_ref, v_ref, pair_ref, mask_ref, o_ref, tile_ref, table_ref, vaug_ref, *, rows):
    g = pl.program_id(2)
    n_groups = rows // NA_GROUP_ROWS
    blk_w = 2 * GRID_W

    @pl.when(g == 0)
    def _():
        for p in range(NA_PAIR_ROWS):
            row = jnp.broadcast_to(pair_ref[p:p + 1, :], (GRID_W, blk_w))
            tile_ref[p] = pltpu.roll(row, blk_w - (NA_WIN_COLS - 1), 1, stride=1, stride_axis=0)
        for v, per_row in enumerate(_na_block_plan(rows)):
            for i, blocks in enumerate(per_row):
                for jp, blk in enumerate(blocks):
                    dst = (v, slice(i * GRID_W, (i + 1) * GRID_W), slice(jp * blk_w, (jp + 1) * blk_w))
                    if blk is None:
                        table_ref[dst] = jnp.full((GRID_W, blk_w), MASK_VALUE, F32)
                    else:
                        table_ref[dst] = (tile_ref[blk[0]] + mask_ref[blk[1]]) * LOG2E
        vaug_ref[:, :HEAD_DIM] = v_ref[...]
        vaug_ref[:, HEAD_DIM:] = jnp.ones(v_ref.shape, BF16)

    gq = NA_GROUP_ROWS * GRID_W
    band_rows = NA_BAND_ROWS * GRID_W
    for u in range(q_ref.shape[0] // gq):
        gi = g * (q_ref.shape[0] // gq) + u
        start_row = jnp.clip(gi * NA_GROUP_ROWS - NA_WIN_ROWS // 2, 0, rows - NA_BAND_ROWS)
        band = pl.ds(pl.multiple_of(start_row * GRID_W, GRID_W), band_rows)
        variant = jnp.where(gi == 0, 0, jnp.where(gi == n_groups - 1, 2, 1))
        s = lax.dot_general(q_ref[u * gq:(u + 1) * gq, :], k_ref[band, :], (((1,), (1,)), ((), ())),
                            preferred_element_type=F32)
        s = s * (HEAD_DIM ** -0.5 * LOG2E) + table_ref[variant]
        p = jnp.exp2(s - jnp.max(s, axis=-1, keepdims=True)).astype(BF16)
        o = jnp.dot(p, vaug_ref[band, :], preferred_element_type=F32)
        o_ref[u * gq:(u + 1) * gq, :] = (o[:, :HEAD_DIM] / o[:, HEAD_DIM:]).astype(o_ref.dtype)


def _na_attention(proj, rpb, batch, seq):
    t = proj.shape[0]
    rows = seq // GRID_W
    assert NA_GROUP_ROWS % 4 == 0 and rows % NA_GROUP_ROWS == 0 and rows >= NA_BAND_ROWS
    assert 2 * GRID_W == HEAD_DIM
    n_groups = rows // NA_GROUP_ROWS
    gq = NA_GROUP_ROWS * GRID_W
    gk = NA_BAND_ROWS * GRID_W
    per_step = _pick(n_groups, NA_GROUPS_PER_STEP)
    n_steps = n_groups // per_step
    k_col0 = OFF_NA_K // HEAD_DIM
    v_col0 = OFF_NA_V // HEAD_DIM
    return pl.pallas_call(
        functools.partial(_na_kernel, rows=rows),
        out_shape=jax.ShapeDtypeStruct((t, NA_WIDTH), BF16),
        grid=(batch, NA_HEADS, n_steps),
        in_specs=[
            pl.BlockSpec((per_step * gq, HEAD_DIM), lambda b, h, g: (b * n_steps + g, h)),
            pl.BlockSpec((seq, HEAD_DIM), lambda b, h, g: (b, k_col0 + h)),
            pl.BlockSpec((seq, HEAD_DIM), lambda b, h, g: (b, v_col0 + h)),
            pl.BlockSpec((None, NA_PAIR_ROWS, 2 * GRID_W), lambda b, h, g: (h, 0, 0)),
            pl.BlockSpec((3, GRID_W, 2 * GRID_W), lambda b, h, g: (0, 0, 0)),
        ],
        out_specs=pl.BlockSpec((per_step * gq, HEAD_DIM), lambda b, h, g: (b * n_steps + g, h)),
        scratch_shapes=[
            pltpu.VMEM((NA_PAIR_ROWS, GRID_W, 2 * GRID_W), F32),
            pltpu.VMEM((3, gq, gk), F32),
            pltpu.VMEM((seq, 2 * HEAD_DIM), BF16),
        ],
        compiler_params=_params(("parallel", "parallel", "arbitrary")),
        name="na_attention",
    )(proj, proj, proj, _na_pair_table(rpb), _na_col_masks())


CONV_HALO = 16


def _branch_merge_kernel(a_na_ref, a_gqa_ref, h_ref, b_ref, c_ref, hp_ref, cp_ref, hn_ref, cn_ref,
                         g_na_ref, g_gqa_ref, g_conv_ref, cw_ref, cb_ref, w_na_ref, w_gqa_ref,
                         w_conv_ref, o_ref, a_conv_ref, part_ref, *, blocks_per_seq, col_chunk):
    tm = h_ref.shape[0]
    pos = pl.program_id(0) % blocks_per_seq
    n_chunks = o_ref.shape[1] // col_chunk

    def branch(a_ref, w_ref, gate_ref, cols):
        y = jnp.dot(a_ref[...], w_ref[:, cols], preferred_element_type=F32)
        return _sigmoid(gate_ref[:, cols].astype(F32)) * y

    for c in range(n_chunks):
        cols = slice(c * col_chunk, (c + 1) * col_chunk)
        part_ref[:, cols] = (branch(a_na_ref, w_na_ref, g_na_ref, cols)
                             + branch(a_gqa_ref, w_gqa_ref, g_gqa_ref, cols))

    u = c_ref[...].astype(F32) * h_ref[...].astype(F32)
    u_prev_row = cp_ref[CONV_HALO - 1:CONV_HALO, :].astype(F32) * hp_ref[CONV_HALO - 1:CONV_HALO, :].astype(F32)
    u_prev_row = jnp.where(pos == 0, 0.0, u_prev_row)
    u_next_row = cn_ref[0:1, :].astype(F32) * hn_ref[0:1, :].astype(F32)
    u_next_row = jnp.where(pos == blocks_per_seq - 1, 0.0, u_next_row)
    row = lax.broadcasted_iota(jnp.int32, u.shape, 0)
    u_prev = jnp.where(row == 0, u_prev_row, pltpu.roll(u, 1, 0))
    u_next = jnp.where(row == tm - 1, u_next_row, pltpu.roll(u, tm - 1, 0))
    y = cb_ref[...] + u_prev * cw_ref[0:1, :] + u * cw_ref[1:2, :] + u_next * cw_ref[2:3, :]
    a_conv_ref[...] = (b_ref[...].astype(F32) * y).astype(BF16)

    for c in range(n_chunks):
        cols = slice(c * col_chunk, (c + 1) * col_chunk)
        merged = part_ref[:, cols] + branch(a_conv_ref, w_conv_ref, g_conv_ref, cols)
        o_ref[:, cols] = merged.astype(o_ref.dtype)


def _branch_merge(a_na, a_gqa, p_conv, p_gate, conv_w, conv_b, w_na, w_gqa, w_conv, layer, seq):
    t = a_na.shape[0]
    ch = conv_w.shape[2]
    d = w_na.shape[1]
    assert CONV_WIDTH == 3
    tm = _pick(seq, TOKENS_F32_BLOCK)
    per_halo = tm // CONV_HALO
    last_halo = t // CONV_HALO - 1
    prev_map = lambda col: (lambda i: (jnp.maximum(i * per_halo - 1, 0), col))
    next_map = lambda col: (lambda i: (jnp.minimum((i + 1) * per_halo, last_halo), col))
    resident = lambda shape: pl.BlockSpec(shape, lambda i: (0, 0), pipeline_mode=pl.Buffered(1))
    return pl.pallas_call(
        functools.partial(_branch_merge_kernel, blocks_per_seq=seq // tm,
                          col_chunk=_pick(d, MERGE_COL_CHUNK)),
        out_shape=jax.ShapeDtypeStruct((t, d), BF16),
        grid=(t // tm,),
        in_specs=[
            pl.BlockSpec((tm, NA_WIDTH), lambda i: (i, 0)),
            pl.BlockSpec((tm, GQA_Q_WIDTH), lambda i: (i, 0)),
            pl.BlockSpec((tm, ch), lambda i: (i, 0)),
            pl.BlockSpec((tm, ch), lambda i: (i, 1)),
            pl.BlockSpec((tm, ch), lambda i: (i, 2)),
            pl.BlockSpec((CONV_HALO, ch), prev_map(0)),
            pl.BlockSpec((CONV_HALO, ch), prev_map(2)),
            pl.BlockSpec((CONV_HALO, ch), next_map(0)),
            pl.BlockSpec((CONV_HALO, ch), next_map(2)),
            pl.BlockSpec((tm, d), lambda i: (i, 0)),
            pl.BlockSpec((tm, d), lambda i: (i, 1)),
            pl.BlockSpec((tm, d), lambda i: (i, 2)),
            pl.BlockSpec((None, CONV_WIDTH, ch), lambda i: (layer, 0, 0)),
            pl.BlockSpec((None, 1, ch), lambda i: (layer, 0, 0)),
            resident((NA_WIDTH, d)),
            resident((GQA_Q_WIDTH, d)),
            resident((ch, d)),
        ],
        out_specs=pl.BlockSpec((tm, d), lambda i: (i, 0)),
        scratch_shapes=[pltpu.VMEM((tm, ch), BF16), pltpu.VMEM((tm, d), F32)],
        compiler_params=_params(("parallel",)),
        name="branch_merge",
    )(a_na, a_gqa, p_conv, p_conv, p_conv, p_conv, p_conv, p_conv, p_conv, p_gate, p_gate, p_gate,
      conv_w, conv_b, w_na, w_gqa, w_conv)


def _out_proj_kernel(x_ref, m_ref, w_ref, g_post_ref, g_next_ref, o_ref, h_ref):
    part = x_ref.shape[0] // OUT_PROJ_ROW_PARTS
    for r in range(OUT_PROJ_ROW_PARTS):
        rows = slice(r * part, (r + 1) * part)
        y = jnp.dot(m_ref[rows, :], w_ref[...], preferred_element_type=F32)
        x_new = x_ref[rows, :] + _rms_rows(y, g_post_ref[...])
        o_ref[rows, :] = x_new
        h_ref[rows, :] = _rms_rows(x_new, g_next_ref[...]).astype(h_ref.dtype)


def _out_proj(x, merged, w_out, post_gains, next_gains, layer):
    t, d = x.shape
    tm = _pick(t, TOKENS_F32_BLOCK)
    gain = lambda: pl.BlockSpec((None, 1, d), lambda i: (layer, 0, 0))
    return pl.pallas_call(
        _out_proj_kernel,
        out_shape=(jax.ShapeDtypeStruct((t, d), F32), jax.ShapeDtypeStruct((t, d), BF16)),
        grid=(t // tm,),
        in_specs=[
            pl.BlockSpec((tm, d), lambda i: (i, 0)),
            pl.BlockSpec((tm, d), lambda i: (i, 0)),
            pl.BlockSpec((d, d), lambda i: (0, 0), pipeline_mode=pl.Buffered(1)),
            gain(),
            gain(),
        ],
        out_specs=(pl.BlockSpec((tm, d), lambda i: (i, 0)), pl.BlockSpec((tm, d), lambda i: (i, 0))),
        compiler_params=_params(("parallel",)),
        name="out_proj_norm",
    )(x, merged, w_out, post_gains, next_gains)


def _ffn_kernel(x_ref, h_ref, wg_ref, wu_ref, wd_ref, g_post_ref, *refs, emit_next):
    (g_next_ref, o_ref, hn_ref, acc_ref) = refs if emit_next else (None, refs[0], None, refs[1])
    f = pl.program_id(1)

    @pl.when(f == 0)
    def _():
        acc_ref[...] = jnp.zeros_like(acc_ref)

    h = h_ref[...]
    a = jnp.dot(h, wg_ref[...], preferred_element_type=F32)
    b = jnp.dot(h, wu_ref[...], preferred_element_type=F32)
    act = (a * _sigmoid(a) * b).astype(BF16)
    acc_ref[...] += jnp.dot(act, wd_ref[...], preferred_element_type=F32)

    @pl.when(f == pl.num_programs(1) - 1)
    def _():
        x_new = x_ref[...] + _rms_rows(acc_ref[...], g_post_ref[...])
        o_ref[...] = x_new
        if emit_next:
            hn_ref[...] = _rms_rows(x_new, g_next_ref[...]).astype(hn_ref.dtype)


def _ffn(x, h, wg, wu, wd, post_gains, next_gains, layer, next_layer):
    t, d = x.shape
    hidden = wg.shape[1]
    tm = _pick(t, TOKENS_F32_BLOCK)
    tf = _pick(hidden, FFN_HIDDEN_CHUNK)
    emit_next = next_layer is not None
    token_block = lambda: pl.BlockSpec((tm, d), lambda i, f: (i, 0))
    gain = lambda l: pl.BlockSpec((None, 1, d), lambda i, f: (l, 0, 0))
    outs = pl.pallas_call(
        functools.partial(_ffn_kernel, emit_next=emit_next),
        out_shape=[jax.ShapeDtypeStruct((t, d), F32)] + [jax.ShapeDtypeStruct((t, d), BF16)] * emit_next,
        grid=(t // tm, hidden // tf),
        in_specs=[
            token_block(),
            token_block(),
            pl.BlockSpec((d, tf), lambda i, f: (0, f)),
            pl.BlockSpec((d, tf), lambda i, f: (0, f)),
            pl.BlockSpec((tf, d), lambda i, f: (f, 0)),
            gain(layer),
        ] + [gain(next_layer)] * emit_next,
        out_specs=[token_block()] + [token_block()] * emit_next,
        scratch_shapes=[pltpu.VMEM((tm, d), F32)],
        compiler_params=_params(("parallel", "arbitrary")),
        name="ffn_swiglu",
    )(x, h, wg, wu, wd, post_gains, *([next_gains] * emit_next))
    return (outs[0], outs[1]) if emit_next else (outs[0], None)


def kernel(x, w_in, na_rpb, q_norm_g, k_norm_g, conv_w, conv_b, w_br_na, w_br_gqa, w_br_conv, w_out,
           pre_mix_g, post_mix_g, pre_ffn_g, post_ffn_g, w_ffn_gate, w_ffn_up, w_ffn_down):
    batch, seq, d = x.shape
    depth = w_in.shape[0]
    ch = conv_w.shape[2]
    hidden = w_ffn_gate.shape[2]
    assert seq % GRID_W == 0
    assert w_in.shape == (depth, d, OFF_CONV + 3 * ch + 3 * d)
    assert na_rpb.shape == (depth, NA_HEADS, 2 * NA_WIN_ROWS - 1, 2 * NA_WIN_COLS - 1)
    assert q_norm_g.shape == k_norm_g.shape == (depth, HEAD_DIM)
    assert conv_w.shape == (depth, CONV_WIDTH, ch) and conv_b.shape == (depth, ch)
    assert w_br_na.shape == (depth, NA_WIDTH, d) and w_br_gqa.shape == (depth, GQA_Q_WIDTH, d)
    assert w_br_conv.shape == (depth, ch, d) and w_out.shape == (depth, d, d)
    assert w_ffn_gate.shape == w_ffn_up.shape == (depth, d, hidden) and w_ffn_down.shape == (depth, hidden, d)
    cos, sin = _rope_tables(seq)

    gain3 = lambda g: g.astype(F32).reshape(depth, 1, -1)
    pre_mix_g, post_mix_g, pre_ffn_g, post_ffn_g = map(gain3, (pre_mix_g, post_mix_g, pre_ffn_g, post_ffn_g))
    conv_w = conv_w.astype(F32)
    conv_b = conv_b.astype(F32).reshape(depth, 1, ch)
    q_gain = q_norm_g.astype(F32) * (HEAD_DIM ** -0.5 * LOG2E)
    qk_gains = jnp.concatenate([jnp.tile(q_gain, (1, GQA_Q_HEADS)),
                                jnp.tile(k_norm_g.astype(F32), (1, GQA_KV_HEADS))], axis=1)
    qk_gains = qk_gains.reshape(depth, -1, 1, 2 * HEAD_DIM)

    y = x.reshape(batch * seq, d)
    h = _pre_norm(y, pre_mix_g, 0)
    for l in range(depth):
        p_attn, w_na, w_gqa, w_conv, w_o = _in_proj(h, w_in, l, 0, OFF_CONV, "in_proj_attn",
                                                    side=(w_br_na, w_br_gqa, w_br_conv, w_out))
        p_conv, wd = _in_proj(h, w_in, l, OFF_CONV, 3 * ch, "in_proj_conv", side=(w_ffn_down,))
        p_gate, wg, wu = _in_proj(h, w_in, l, OFF_CONV + 3 * ch, 3 * d, "in_proj_gate",
                                  side=(w_ffn_gate, w_ffn_up))

        qk = _qk_prep(p_attn, qk_gains[l], cos, sin, seq)
        plain_exp = (_gqa_logit_bound(q_gain[l], k_norm_g[l].astype(F32)) <= GQA_PLAIN_EXP_BOUND)
        a_gqa = _gqa_attention(qk, p_attn, plain_exp.astype(jnp.int32).reshape(1), batch, seq)
        a_na = _na_attention(p_attn, na_rpb[l], batch, seq)

        merged = _branch_merge(a_na, a_gqa, p_conv, p_gate, conv_w, conv_b, w_na, w_gqa, w_conv, l, seq)
        y, h = _out_proj(y, merged, w_o, post_mix_g, pre_ffn_g, l)
        y, h = _ffn(y, h, wg, wu, wd, post_ffn_g, pre_mix_g, l, l + 1 if l + 1 < depth else None)
    return y.reshape(batch, seq, d)
```

```python
import functools
import math

import numpy as np
import jax
import jax.numpy as jnp
from jax import lax
from jax.experimental import pallas as pl
from jax.experimental.pallas import tpu as pltpu

GRID_W = 64
HEAD_DIM = 128
NA_HEADS = 8
NA_WIN_ROWS = 8
NA_WIN_COLS = 16
GQA_Q_HEADS = 8
GQA_KV_HEADS = 2
GQA_GROUP = GQA_Q_HEADS // GQA_KV_HEADS
ROPE_THETA = 10000.0
CONV_WIDTH = 3
RMS_EPS = 1e-6

NA_WIDTH = NA_HEADS * HEAD_DIM
GQA_Q_WIDTH = GQA_Q_HEADS * HEAD_DIM
GQA_KV_WIDTH = GQA_KV_HEADS * HEAD_DIM
OFF_NA_Q = 0
OFF_NA_K = NA_WIDTH
OFF_NA_V = 2 * NA_WIDTH
OFF_GQA_Q = 3 * NA_WIDTH
OFF_GQA_K = OFF_GQA_Q + GQA_Q_WIDTH
OFF_GQA_V = OFF_GQA_K + GQA_KV_WIDTH
OFF_CONV = OFF_GQA_V + GQA_KV_WIDTH

MASK_VALUE = -1e30
LOG2E = math.log2(math.e)

V7X_VMEM_BYTES = 64 * 1024 * 1024
VMEM_LIMIT = V7X_VMEM_BYTES * 7 // 8

TOKENS_F32_BLOCK = (512, 256)
TOKENS_IN_PROJ = (1024, 512, 256)
TOKENS_QK_PREP = (2048, 1024, 512, 256)
IN_PROJ_COLS = 1536
WEIGHT_CAST_ROWS = 256
GQA_QUERY_BLOCK = (512, 256, 128)
GQA_KEY_CHUNK = (512, 256, 128)
GQA_CHUNKS_PER_TRIP = (8, 4, 2)
NA_GROUP_ROWS = 4
NA_BAND_ROWS = NA_GROUP_ROWS + NA_WIN_ROWS
NA_GROUPS_PER_STEP = (16, 8, 2, 1)
MERGE_COL_CHUNK = (512, 256, 128)
OUT_PROJ_ROW_PARTS = 4
FFN_HIDDEN_CHUNK = (512, 256, 128)

BF16 = jnp.bfloat16
F32 = jnp.float32
BF16_SUBLANES = 16


def _params(semantics):
    return pltpu.CompilerParams(dimension_semantics=semantics, vmem_limit_bytes=VMEM_LIMIT)


def _pick(n, candidates):
    for c in candidates:
        if n % c == 0:
            return c
    raise ValueError(f"no tile in {candidates} divides {n}")


def _sigmoid(x):
    return 0.5 * jnp.tanh(0.5 * x) + 0.5


def _rms_rows(x, g):
    ms = jnp.mean(x * x, axis=-1, keepdims=True)
    return x * lax.rsqrt(ms + RMS_EPS) * g


def _norm_kernel(x_ref, g_ref, o_ref):
    o_ref[...] = _rms_rows(x_ref[...], g_ref[...]).astype(o_ref.dtype)


def _pre_norm(x, gains, layer):
    t, d = x.shape
    tm = _pick(t, TOKENS_F32_BLOCK)
    return pl.pallas_call(
        _norm_kernel,
        out_shape=jax.ShapeDtypeStruct((t, d), BF16),
        grid=(t // tm,),
        in_specs=[
            pl.BlockSpec((tm, d), lambda i: (i, 0)),
            pl.BlockSpec((None, 1, d), lambda i: (layer, 0, 0)),
        ],
        out_specs=pl.BlockSpec((tm, d), lambda i: (i, 0)),
        compiler_params=_params(("parallel",)),
        name="pre_norm",
    )(x, gains)


def _proj_kernel(h_ref, w_ref, *refs, row_chunk, n_side):
    side_in, o_ref, side_out, wb_ref = refs[:n_side], refs[n_side], refs[n_side + 1:-1], refs[-1]

    @pl.when(pl.program_id(1) == 0)
    def _():
        def body(c, carry):
            rows = pl.ds(pl.multiple_of(c * row_chunk, row_chunk), row_chunk)
            wb_ref[rows, :] = w_ref[rows, :].astype(BF16)
            return carry
        lax.fori_loop(0, w_ref.shape[0] // row_chunk, body, 0)

    o_ref[...] = jnp.dot(h_ref[...], wb_ref[...], preferred_element_type=F32).astype(o_ref.dtype)
    for src, dst in zip(side_in, side_out):
        dst[...] = src[...].astype(dst.dtype)


def _in_proj(h, w_in, layer, col_off, n_cols, name, side=()):
    t, d = h.shape
    tm = _pick(t, TOKENS_IN_PROJ)
    tn = IN_PROJ_COLS
    assert col_off % tn == 0 and n_cols % tn == 0
    col0 = col_off // tn
    n_i = t // tm
    n_steps = (n_cols // tn) * n_i
    side_specs, side_out_specs, side_shapes = [], [], []
    for w in side:
        rows, cols = w.shape[1:]
        slab = next(s for s in range(BF16_SUBLANES, rows + 1, BF16_SUBLANES)
                    if rows % s == 0 and rows // s <= n_steps)
        last = rows // slab - 1
        side_specs.append(pl.BlockSpec(
            (None, slab, cols), lambda j, i, last=last: (layer, jnp.minimum(j * n_i + i, last), 0)))
        side_out_specs.append(pl.BlockSpec(
            (slab, cols), lambda j, i, last=last: (jnp.minimum(j * n_i + i, last), 0)))
        side_shapes.append(jax.ShapeDtypeStruct((rows, cols), BF16))
    outs = pl.pallas_call(
        functools.partial(_proj_kernel, row_chunk=min(d, WEIGHT_CAST_ROWS), n_side=len(side)),
        out_shape=[jax.ShapeDtypeStruct((t, n_cols), BF16)] + side_shapes,
        grid=(n_cols // tn, n_i),
        in_specs=[
            pl.BlockSpec((tm, d), lambda j, i: (i, 0)),
            pl.BlockSpec((None, d, tn), lambda j, i: (layer, 0, col0 + j)),
        ] + side_specs,
        out_specs=[pl.BlockSpec((tm, tn), lambda j, i: (i, j))] + side_out_specs,
        scratch_shapes=[pltpu.VMEM((d, tn), BF16)],
        compiler_params=_params(("arbitrary", "arbitrary")),
        name=name,
    )(h, w_in, *side)
    return outs[0] if not side else outs


def _rope_tables(seq):
    half = HEAD_DIM // 2
    quarter = half // 2
    inv_freq = 1.0 / (ROPE_THETA ** (np.arange(quarter, dtype=np.float64) / quarter))
    t = np.arange(seq)
    ang_r = (t // GRID_W)[:, None] * inv_freq[None, :]
    ang_c = (t % GRID_W)[:, None] * inv_freq[None, :]
    cos = np.concatenate([np.cos(ang_r), np.cos(ang_r), np.cos(ang_c), np.cos(ang_c)], axis=1)
    sin = np.concatenate([-np.sin(ang_r), np.sin(ang_r), -np.sin(ang_c), np.sin(ang_c)], axis=1)
    return jnp.asarray(cos, F32), jnp.asarray(sin, F32)


def _qk_prep_kernel(x_ref, g_ref, cos_ref, sin_ref, o_ref):
    heads = x_ref.shape[1] // HEAD_DIM
    quarter = HEAD_DIM // 4
    cos = cos_ref[...]
    sin = sin_ref[...]
    lane = lax.broadcasted_iota(jnp.int32, cos.shape, 1)
    first = (lane % (2 * quarter)) < quarter
    for h in range(heads):
        cols = slice(h * HEAD_DIM, (h + 1) * HEAD_DIM)
        y = _rms_rows(x_ref[:, cols].astype(F32), g_ref[0, :, cols])
        partner = jnp.where(first, pltpu.roll(y, HEAD_DIM - quarter, 1), pltpu.roll(y, quarter, 1))
        o_ref[:, cols] = (y * cos + partner * sin).astype(o_ref.dtype)


def _qk_prep(proj, gains, cos, sin, seq):
    t = proj.shape[0]
    n_heads = GQA_Q_HEADS + GQA_KV_HEADS
    width = 2 * HEAD_DIM
    assert OFF_GQA_Q % width == 0 and n_heads % 2 == 0
    tm = _pick(seq, TOKENS_QK_PREP)
    blocks_per_seq = seq // tm
    col0 = OFF_GQA_Q // width
    return pl.pallas_call(
        _qk_prep_kernel,
        out_shape=jax.ShapeDtypeStruct((t, n_heads * HEAD_DIM), BF16),
        grid=(t // tm, n_heads // 2),
        in_specs=[
            pl.BlockSpec((tm, width), lambda i, j: (i, col0 + j)),
            pl.BlockSpec((1, 1, width), lambda i, j: (j, 0, 0)),
            pl.BlockSpec((tm, HEAD_DIM), lambda i, j: (i % blocks_per_seq, 0)),
            pl.BlockSpec((tm, HEAD_DIM), lambda i, j: (i % blocks_per_seq, 0)),
        ],
        out_specs=pl.BlockSpec((tm, width), lambda i, j: (i, j)),
        compiler_params=_params(("parallel", "parallel")),
        name="gqa_qk_prep",
    )(proj, gains, cos, sin)


GQA_ONES_ROWS = 16
GQA_PLAIN_EXP_BOUND = 64.0


def _gqa_logit_bound(q_gain, k_gain):
    return 1.02 * HEAD_DIM * jnp.max(jnp.abs(q_gain)) * jnp.max(jnp.abs(k_gain))


def _gqa_kernel(plain_ref, q_ref, k_ref, v_ref, o_ref, qs_ref, vt_ref, acc_ref, s_ref, m_ref, *, tk):
    tq = q_ref.shape[0]
    seq = k_ref.shape[0]
    n_chunks = seq // tk
    assert n_chunks % 2 == 0

    def chunk(c):
        return pl.ds(pl.multiple_of(c * tk, tk), tk)

    @pl.when(pl.program_id(2) == 0)
    def _():
        def body(c, carry):
            vt_ref[c, :HEAD_DIM, :] = v_ref[chunk(c), :].astype(F32).T.astype(BF16)
            vt_ref[c, HEAD_DIM:, :] = jnp.ones((GQA_ONES_ROWS, tk), BF16)
            return carry
        lax.fori_loop(0, n_chunks, body, 0)

    for h in range(GQA_GROUP):
        qs_ref[h * tq:(h + 1) * tq, :] = q_ref[:, h * HEAD_DIM:(h + 1) * HEAD_DIM]
    acc_ref[...] = jnp.zeros_like(acc_ref)

    def scores(c):
        return lax.dot_general(k_ref[chunk(c), :], qs_ref[...], (((1,), (1,)), ((), ())),
                               preferred_element_type=F32)

    @pl.when(plain_ref[0] != 0)
    def _():
        per_trip = _pick(n_chunks, GQA_CHUNKS_PER_TRIP)

        def body(t, carry):
            pv = None
            for u in range(per_trip):
                c = per_trip * t + u
                p = jnp.exp2(scores(c)).astype(BF16)
                part = jnp.dot(vt_ref[c], p, preferred_element_type=F32)
                pv = part if pv is None else pv + part
            acc_ref[...] += pv
            return carry
        lax.fori_loop(0, n_chunks // per_trip, body, 0)

    @pl.when(plain_ref[0] == 0)
    def _():
        m_ref[...] = jnp.full_like(m_ref, -jnp.inf)

        def accumulate(c, slot):
            s = s_ref[slot]
            m_old = m_ref[...]
            m_new = jnp.maximum(m_old, jnp.max(s, axis=0, keepdims=True))
            m_ref[...] = m_new
            p = jnp.exp2(s - m_new).astype(BF16)
            pv = jnp.dot(vt_ref[c], p, preferred_element_type=F32)
            acc_ref[...] = acc_ref[...] * jnp.exp2(m_old - m_new) + pv

        s_ref[0] = scores(0)

        def body(t, carry):
            s_ref[1] = scores(2 * t + 1)
            accumulate(2 * t, 0)
            s_ref[0] = scores(2 * t + 2)
            accumulate(2 * t + 1, 1)
            return carry

        lax.fori_loop(0, n_chunks // 2 - 1, body, 0)
        s_ref[1] = scores(n_chunks - 1)
        accumulate(n_chunks - 2, 0)
        accumulate(n_chunks - 1, 1)

    out_t = acc_ref[:HEAD_DIM, :] / acc_ref[HEAD_DIM:HEAD_DIM + 1, :]
    for h in range(GQA_GROUP):
        o_ref[:, h * HEAD_DIM:(h + 1) * HEAD_DIM] = out_t[:, h * tq:(h + 1) * tq].T.astype(o_ref.dtype)


def _gqa_attention(qk, proj, plain_exp, batch, seq):
    t = qk.shape[0]
    tq = _pick(seq, GQA_QUERY_BLOCK)
    tk = _pick(seq, GQA_KEY_CHUNK)
    gw = GQA_GROUP * HEAD_DIM
    k_col0 = GQA_Q_WIDTH // HEAD_DIM
    v_col0 = OFF_GQA_V // HEAD_DIM
    q_blocks = seq // tq
    grid_spec = pltpu.PrefetchScalarGridSpec(
        num_scalar_prefetch=1,
        grid=(batch, GQA_KV_HEADS, q_blocks),
        in_specs=[
            pl.BlockSpec((tq, gw), lambda b, g, i, flag: (b * q_blocks + i, g)),
            pl.BlockSpec((seq, HEAD_DIM), lambda b, g, i, flag: (b, k_col0 + g)),
            pl.BlockSpec((seq, HEAD_DIM), lambda b, g, i, flag: (b, v_col0 + g)),
        ],
        out_specs=pl.BlockSpec((tq, gw), lambda b, g, i, flag: (b * q_blocks + i, g)),
        scratch_shapes=[
            pltpu.VMEM((GQA_GROUP * tq, HEAD_DIM), BF16),
            pltpu.VMEM((seq // tk, HEAD_DIM + GQA_ONES_ROWS, tk), BF16),
            pltpu.VMEM((HEAD_DIM + GQA_ONES_ROWS, GQA_GROUP * tq), F32),
            pltpu.VMEM((2, tk, GQA_GROUP * tq), F32),
            pltpu.VMEM((1, GQA_GROUP * tq), F32),
        ],
    )
    return pl.pallas_call(
        functools.partial(_gqa_kernel, tk=tk),
        out_shape=jax.ShapeDtypeStruct((t, GQA_Q_WIDTH), BF16),
        grid_spec=grid_spec,
        compiler_params=_params(("parallel", "parallel", "arbitrary")),
        name="gqa_flash",
    )(plain_exp, qk, qk, proj)


NA_PAIR_ROWS = 2 * NA_WIN_ROWS
NA_MASK_BOTH, NA_MASK_LEFT, NA_MASK_RIGHT = 0, 1, 2


def _na_block_plan(rows):
    gr, br = NA_GROUP_ROWS, NA_BAND_ROWS
    plan = []
    for r0 in (0, min(gr, rows - gr), rows - gr):
        band_start = int(np.clip(r0 - NA_WIN_ROWS // 2, 0, rows - br))
        per_row = []
        for i in range(gr):
            r = r0 + i
            row_start = int(np.clip(r - NA_WIN_ROWS // 2, 0, rows - NA_WIN_ROWS))
            blocks = []
            for jp in range(br // 2):
                key_row = band_start + 2 * jp
                ok_l = row_start <= key_row < row_start + NA_WIN_ROWS
                ok_r = row_start <= key_row + 1 < row_start + NA_WIN_ROWS
                if not (ok_l or ok_r):
                    blocks.append(None)
                    continue
                kind = NA_MASK_BOTH if (ok_l and ok_r) else (NA_MASK_LEFT if ok_l else NA_MASK_RIGHT)
                blocks.append((key_row - r + NA_WIN_ROWS, kind))
            per_row.append(blocks)
        plan.append(per_row)
    return plan


def _na_pair_table(rpb):
    h, nr, nc = rpb.shape
    rpb = jnp.pad(rpb.astype(F32), ((0, 0), (0, 0), (0, GRID_W - nc)))
    zero = jnp.zeros((h, 1, GRID_W), F32)
    left = jnp.concatenate([zero, rpb], axis=1)
    right = jnp.concatenate([rpb, zero], axis=1)
    return jnp.concatenate([left, right], axis=2)


def _na_col_masks():
    w = np.arange(GRID_W)[:, None]
    c = np.arange(2 * GRID_W)[None, :] % GRID_W
    col_start = np.clip(w - NA_WIN_COLS // 2, 0, GRID_W - NA_WIN_COLS)
    base = np.where((c >= col_start) & (c < col_start + NA_WIN_COLS), 0.0, MASK_VALUE)
    left_half = np.arange(2 * GRID_W)[None, :] < GRID_W
    masks = np.stack([base, np.where(left_half, base, MASK_VALUE), np.where(left_half, MASK_VALUE, base)])
    return jnp.asarray(masks, F32)


def _na_kernel(q_ref, k_ref, v_ref, pair_ref, mask_ref, o_ref, tile_ref, table_ref, vaug_ref, *, rows):
    g = pl.program_id(2)
    n_groups = rows // NA_GROUP_ROWS
    blk_w = 2 * GRID_W

    @pl.when(g == 0)
    def _():
        for p in range(NA_PAIR_ROWS):
            row = jnp.broadcast_to(pair_ref[p:p + 1, :], (GRID_W, blk_w))
            tile_ref[p] = pltpu.roll(row, blk_w - (NA_WIN_COLS - 1), 1, stride=1, stride_axis=0)
        for v, per_row in enumerate(_na_block_plan(rows)):
            for i, blocks in enumerate(per_row):
                for jp, blk in enumerate(blocks):
                    dst = (v, slice(i * GRID_W, (i + 1) * GRID_W), slice(jp * blk_w, (jp + 1) * blk_w))
                    if blk is None:
                        table_ref[dst] = jnp.full((GRID_W, blk_w), MASK_VALUE, F32)
                    else:
                        table_ref[dst] = (tile_ref[blk[0]] + mask_ref[blk[1]]) * LOG2E
        vaug_ref[:, :HEAD_DIM] = v_ref[...]
        vaug_ref[:, HEAD_DIM:] = jnp.ones(v_ref.shape, BF16)

    gq = NA_GROUP_ROWS * GRID_W
    band_rows = NA_BAND_ROWS * GRID_W
    for u in range(q_ref.shape[0] // gq):
        gi = g * (q_ref.shape[0] // gq) + u
        start_row = jnp.clip(gi * NA_GROUP_ROWS - NA_WIN_ROWS // 2, 0, rows - NA_BAND_ROWS)
        band = pl.ds(pl.multiple_of(start_row * GRID_W, GRID_W), band_rows)
        variant = jnp.where(gi == 0, 0, jnp.where(gi == n_groups - 1, 2, 1))
        s = lax.dot_general(q_ref[u * gq:(u + 1) * gq, :], k_ref[band, :], (((1,), (1,)), ((), ())),
                            preferred_element_type=F32)
        s = s * (HEAD_DIM ** -0.5 * LOG2E) + table_ref[variant]
        p = jnp.exp2(s - jnp.max(s, axis=-1, keepdims=True)).astype(BF16)
        o = jnp.dot(p, vaug_ref[band, :], preferred_element_type=F32)
        o_ref[u * gq:(u + 1) * gq, :] = (o[:, :HEAD_DIM] / o[:, HEAD_DIM:]).astype(o_ref.dtype)


def _na_attention(proj, rpb, batch, seq):
    t = proj.shape[0]
    rows = seq // GRID_W
    assert NA_GROUP_ROWS % 4 == 0 and rows % NA_GROUP_ROWS == 0 and rows >= NA_BAND_ROWS
    assert 2 * GRID_W == HEAD_DIM
    n_groups = rows // NA_GROUP_ROWS
    gq = NA_GROUP_ROWS * GRID_W
    gk = NA_BAND_ROWS * GRID_W
    per_step = _pick(n_groups, NA_GROUPS_PER_STEP)
    n_steps = n_groups // per_step
    k_col0 = OFF_NA_K // HEAD_DIM
    v_col0 = OFF_NA_V // HEAD_DIM
    return pl.pallas_call(
        functools.partial(_na_kernel, rows=rows),
        out_shape=jax.ShapeDtypeStruct((t, NA_WIDTH), BF16),
        grid=(batch, NA_HEADS, n_steps),
        in_specs=[
            pl.BlockSpec((per_step * gq, HEAD_DIM), lambda b, h, g: (b * n_steps + g, h)),
            pl.BlockSpec((seq, HEAD_DIM), lambda b, h, g: (b, k_col0 + h)),
            pl.BlockSpec((seq, HEAD_DIM), lambda b, h, g: (b, v_col0 + h)),
            pl.BlockSpec((None, NA_PAIR_ROWS, 2 * GRID_W), lambda b, h, g: (h, 0, 0)),
            pl.BlockSpec((3, GRID_W, 2 * GRID_W), lambda b, h, g: (0, 0, 0)),
        ],
        out_specs=pl.BlockSpec((per_step * gq, HEAD_DIM), lambda b, h, g: (b * n_steps + g, h)),
        scratch_shapes=[
            pltpu.VMEM((NA_PAIR_ROWS, GRID_W, 2 * GRID_W), F32),
            pltpu.VMEM((3, gq, gk), F32),
            pltpu.VMEM((seq, 2 * HEAD_DIM), BF16),
        ],
        compiler_params=_params(("parallel", "parallel", "arbitrary")),
        name="na_attention",
    )(proj, proj, proj, _na_pair_table(rpb), _na_col_masks())


CONV_HALO = 16


def _branch_merge_kernel(a_na_ref, a_gqa_ref, h_ref, b_ref, c_ref, hp_ref, cp_ref, hn_ref, cn_ref,
                         g_na_ref, g_gqa_ref, g_conv_ref, cw_ref, cb_ref, w_na_ref, w_gqa_ref,
                         w_conv_ref, o_ref, a_conv_ref, part_ref, *, blocks_per_seq, col_chunk):
    tm = h_ref.shape[0]
    pos = pl.program_id(0) % blocks_per_seq
    n_chunks = o_ref.shape[1] // col_chunk

    def branch(a_ref, w_ref, gate_ref, cols):
        y = jnp.dot(a_ref[...], w_ref[:, cols], preferred_element_type=F32)
        return _sigmoid(gate_ref[:, cols]).astype(F32) * y

    for c in range(n_chunks):
        cols = slice(c * col_chunk, (c + 1) * col_chunk)
        part_ref[:, cols] = (branch(a_na_ref, w_na_ref, g_na_ref, cols)
                             + branch(a_gqa_ref, w_gqa_ref, g_gqa_ref, cols))

    u = c_ref[...].astype(F32) * h_ref[...].astype(F32)
    u_prev_row = cp_ref[CONV_HALO - 1:CONV_HALO, :].astype(F32) * hp_ref[CONV_HALO - 1:CONV_HALO, :].astype(F32)
    u_prev_row = jnp.where(pos == 0, 0.0, u_prev_row)
    u_next_row = cn_ref[0:1, :].astype(F32) * hn_ref[0:1, :].astype(F32)
    u_next_row = jnp.where(pos == blocks_per_seq - 1, 0.0, u_next_row)
    row = lax.broadcasted_iota(jnp.int32, u.shape, 0)
    u_prev = jnp.where(row == 0, u_prev_row, pltpu.roll(u, 1, 0))
    u_next = jnp.where(row == tm - 1, u_next_row, pltpu.roll(u, tm - 1, 0))
    y = cb_ref[...] + u_prev * cw_ref[0:1, :] + u * cw_ref[1:2, :] + u_next * cw_ref[2:3, :]
    a_conv_ref[...] = (b_ref[...].astype(F32) * y).astype(BF16)

    for c in range(n_chunks):
        cols = slice(c * col_chunk, (c + 1) * col_chunk)
        merged = part_ref[:, cols] + branch(a_conv_ref, w_conv_ref, g_conv_ref, cols)
        o_ref[:, cols] = merged.astype(o_ref.dtype)


def _branch_merge(a_na, a_gqa, p_conv, p_gate, conv_w, conv_b, w_na, w_gqa, w_conv, layer, seq):
    t = a_na.shape[0]
    ch = conv_w.shape[2]
    d = w_na.shape[1]
    assert CONV_WIDTH == 3
    tm = _pick(seq, TOKENS_F32_BLOCK)
    per_halo = tm // CONV_HALO
    last_halo = t // CONV_HALO - 1
    prev_map = lambda col: (lambda i: (jnp.maximum(i * per_halo - 1, 0), col))
    next_map = lambda col: (lambda i: (jnp.minimum((i + 1) * per_halo, last_halo), col))
    resident = lambda shape: pl.BlockSpec(shape, lambda i: (0, 0), pipeline_mode=pl.Buffered(1))
    return pl.pallas_call(
        functools.partial(_branch_merge_kernel, blocks_per_seq=seq // tm,
                          col_chunk=_pick(d, MERGE_COL_CHUNK)),
        out_shape=jax.ShapeDtypeStruct((t, d), BF16),
        grid=(t // tm,),
        in_specs=[
            pl.BlockSpec((tm, NA_WIDTH), lambda i: (i, 0)),
            pl.BlockSpec((tm, GQA_Q_WIDTH), lambda i: (i, 0)),
            pl.BlockSpec((tm, ch), lambda i: (i, 0)),
            pl.BlockSpec((tm, ch), lambda i: (i, 1)),
            pl.BlockSpec((tm, ch), lambda i: (i, 2)),
            pl.BlockSpec((CONV_HALO, ch), prev_map(0)),
            pl.BlockSpec((CONV_HALO, ch), prev_map(2)),
            pl.BlockSpec((CONV_HALO, ch), next_map(0)),
            pl.BlockSpec((CONV_HALO, ch), next_map(2)),
            pl.BlockSpec((tm, d), lambda i: (i, 0)),
            pl.BlockSpec((tm, d), lambda i: (i, 1)),
            pl.BlockSpec((tm, d), lambda i: (i, 2)),
            pl.BlockSpec((None, CONV_WIDTH, ch), lambda i: (layer, 0, 0)),
            pl.BlockSpec((None, 1, ch), lambda i: (layer, 0, 0)),
            resident((NA_WIDTH, d)),
            resident((GQA_Q_WIDTH, d)),
            resident((ch, d)),
        ],
        out_specs=pl.BlockSpec((tm, d), lambda i: (i, 0)),
        scratch_shapes=[pltpu.VMEM((tm, ch), BF16), pltpu.VMEM((tm, d), F32)],
        compiler_params=_params(("parallel",)),
        name="branch_merge",
    )(a_na, a_gqa, p_conv, p_conv, p_conv, p_conv, p_conv, p_conv, p_conv, p_gate, p_gate, p_gate,
      conv_w, conv_b, w_na, w_gqa, w_conv)


def _out_proj_kernel(x_ref, m_ref, w_ref, g_post_ref, g_next_ref, o_ref, h_ref):
    part = x_ref.shape[0] // OUT_PROJ_ROW_PARTS
    for r in range(OUT_PROJ_ROW_PARTS):
        rows = slice(r * part, (r + 1) * part)
        y = jnp.dot(m_ref[rows, :], w_ref[...], preferred_element_type=F32)
        x_new = x_ref[rows, :] + _rms_rows(y, g_post_ref[...])
        o_ref[rows, :] = x_new
        h_ref[rows, :] = _rms_rows(x_new, g_next_ref[...]).astype(h_ref.dtype)


def _out_proj(x, merged, w_out, post_gains, next_gains, layer):
    t, d = x.shape
    tm = _pick(t, TOKENS_F32_BLOCK)
    gain = lambda: pl.BlockSpec((None, 1, d), lambda i: (layer, 0, 0))
    return pl.pallas_call(
        _out_proj_kernel,
        out_shape=(jax.ShapeDtypeStruct((t, d), F32), jax.ShapeDtypeStruct((t, d), BF16)),
        grid=(t // tm,),
        in_specs=[
            pl.BlockSpec((tm, d), lambda i: (i, 0)),
            pl.BlockSpec((tm, d), lambda i: (i, 0)),
            pl.BlockSpec((d, d), lambda i: (0, 0), pipeline_mode=pl.Buffered(1)),
            gain(),
            gain(),
        ],
        out_specs=(pl.BlockSpec((tm, d), lambda i: (i, 0)), pl.BlockSpec((tm, d), lambda i: (i, 0))),
        compiler_params=_params(("parallel",)),
        name="out_proj_norm",
    )(x, merged, w_out, post_gains, next_gains)


def _ffn_kernel(x_ref, h_ref, wg_ref, wu_ref, wd_ref, g_post_ref, *refs, emit_next):
    (g_next_ref, o_ref, hn_ref, acc_ref) = refs if emit_next else (None, refs[0], None, refs[1])
    f = pl.program_id(1)

    @pl.when(f == 0)
    def _():
        acc_ref[...] = jnp.zeros_like(acc_ref)

    h = h_ref[...]
    a = jnp.dot(h, wg_ref[...], preferred_element_type=F32)
    b = jnp.dot(h, wu_ref[...], preferred_element_type=F32)
    act = (a * _sigmoid(a) * b).astype(BF16)
    acc_ref[...] += jnp.dot(act, wd_ref[...], preferred_element_type=F32)

    @pl.when(f == pl.num_programs(1) - 1)
    def _():
        x_new = x_ref[...] + _rms_rows(acc_ref[...], g_post_ref[...])
        o_ref[...] = x_new
        if emit_next:
            hn_ref[...] = _rms_rows(x_new, g_next_ref[...]).astype(hn_ref.dtype)


def _ffn(x, h, wg, wu, wd, post_gains, next_gains, layer, next_layer):
    t, d = x.shape
    hidden = wg.shape[1]
    tm = _pick(t, TOKENS_F32_BLOCK)
    tf = _pick(hidden, FFN_HIDDEN_CHUNK)
    emit_next = next_layer is not None
    token_block = lambda: pl.BlockSpec((tm, d), lambda i, f: (i, 0))
    gain = lambda l: pl.BlockSpec((None, 1, d), lambda i, f: (l, 0, 0))
    outs = pl.pallas_call(
        functools.partial(_ffn_kernel, emit_next=emit_next),
        out_shape=[jax.ShapeDtypeStruct((t, d), F32)] + [jax.ShapeDtypeStruct((t, d), BF16)] * emit_next,
        grid=(t // tm, hidden // tf),
        in_specs=[
            token_block(),
            token_block(),
            pl.BlockSpec((d, tf), lambda i, f: (0, f)),
            pl.BlockSpec((d, tf), lambda i, f: (0, f)),
            pl.BlockSpec((tf, d), lambda i, f: (f, 0)),
            gain(layer),
        ] + [gain(next_layer)] * emit_next,
        out_specs=[token_block()] + [token_block()] * emit_next,
        scratch_shapes=[pltpu.VMEM((tm, d), F32)],
        compiler_params=_params(("parallel", "arbitrary")),
        name="ffn_swiglu",
    )(x, h, wg, wu, wd, post_gains, *([next_gains] * emit_next))
    return (outs[0], outs[1]) if emit_next else (outs[0], None)


def kernel(x, w_in, na_rpb, q_norm_g, k_norm_g, conv_w, conv_b, w_br_na, w_br_gqa, w_br_conv, w_out,
           pre_mix_g, post_mix_g, pre_ffn_g, post_ffn_g, w_ffn_gate, w_ffn_up, w_ffn_down):
    batch, seq, d = x.shape
    depth = w_in.shape[0]
    ch = conv_w.shape[2]
    hidden = w_ffn_gate.shape[2]
    assert seq % GRID_W == 0
    assert w_in.shape == (depth, d, OFF_CONV + 3 * ch + 3 * d)
    assert na_rpb.shape == (depth, NA_HEADS, 2 * NA_WIN_ROWS - 1, 2 * NA_WIN_COLS - 1)
    assert q_norm_g.shape == k_norm_g.shape == (depth, HEAD_DIM)
    assert conv_w.shape == (depth, CONV_WIDTH, ch) and conv_b.shape == (depth, ch)
    assert w_br_na.shape == (depth, NA_WIDTH, d) and w_br_gqa.shape == (depth, GQA_Q_WIDTH, d)
    assert w_br_conv.shape == (depth, ch, d) and w_out.shape == (depth, d, d)
    assert w_ffn_gate.shape == w_ffn_up.shape == (depth, d, hidden) and w_ffn_down.shape == (depth, hidden, d)
    cos, sin = _rope_tables(seq)

    gain3 = lambda g: g.astype(F32).reshape(depth, 1, -1)
    pre_mix_g, post_mix_g, pre_ffn_g, post_ffn_g = map(gain3, (pre_mix_g, post_mix_g, pre_ffn_g, post_ffn_g))
    conv_w = conv_w.astype(F32)
    conv_b = conv_b.astype(F32).reshape(depth, 1, ch)
    q_gain = q_norm_g.astype(F32) * (HEAD_DIM ** -0.5 * LOG2E)
    qk_gains = jnp.concatenate([jnp.tile(q_gain, (1, GQA_Q_HEADS)),
                                jnp.tile(k_norm_g.astype(F32), (1, GQA_KV_HEADS))], axis=1)
    qk_gains = qk_gains.reshape(depth, -1, 1, 2 * HEAD_DIM)

    y = x.reshape(batch * seq, d)
    h = _pre_norm(y, pre_mix_g, 0)
    for l in range(depth):
        p_attn, w_na, w_gqa, w_conv, w_o = _in_proj(h, w_in, l, 0, OFF_CONV, "in_proj_attn",
                                                    side=(w_br_na, w_br_gqa, w_br_conv, w_out))
        p_conv, wd = _in_proj(h, w_in, l, OFF_CONV, 3 * ch, "in_proj_conv", side=(w_ffn_down,))
        p_gate, wg, wu = _in_proj(h, w_in, l, OFF_CONV + 3 * ch, 3 * d, "in_proj_gate",
                                  side=(w_ffn_gate, w_ffn_up))

        qk = _qk_prep(p_attn, qk_gains[l], cos, sin, seq)
        plain_exp = (_gqa_logit_bound(q_gain[l], k_norm_g[l].astype(F32)) <= GQA_PLAIN_EXP_BOUND)
        a_gqa = _gqa_attention(qk, p_attn, plain_exp.astype(jnp.int32).reshape(1), batch, seq)
        a_na = _na_attention(p_attn, na_rpb[l], batch, seq)

        merged = _branch_merge(a_na, a_gqa, p_conv, p_gate, conv_w, conv_b, w_na, w_gqa, w_conv, l, seq)
        y, h = _out_proj(y, merged, w_o, post_mix_g, pre_ffn_g, l)
        y, h = _ffn(y, h, wg, wu, wd, post_ffn_g, pre_mix_g, l, l + 1 if l + 1 < depth else None)
    return y.reshape(batch, seq, d)
```
